```python
import math
import jax, jax.numpy as jnp
from jax import lax
import numpy as np

D_MODEL = 2048
BATCH = 4
SEQ = 4096
DEPTH = 4
DEC_BATCH = 8
DEC_SEQ = 32
PAST_LEN = 1024

CHUNK = 64
QBLOCK = 128
N_MIXERS = 3
D_INNER = D_MODEL
NORM_EPS = 1e-6
GLA_HEADS = 4
GLA_KD = D_INNER // 2
GLA_DK = GLA_KD // GLA_HEADS
GLA_DV = D_INNER // GLA_HEADS
GLA_GATE_RANK = 16
GLA_GATE_NORMALIZER = 16.0
HGRN_EXPAND = 128
HGRN_HEADS = D_INNER // HGRN_EXPAND
HGRN_DK = HGRN_EXPAND
HGRN_DV = D_INNER // HGRN_HEADS
DIFF_HD = 128
DIFF_HEADS = D_INNER // (2 * DIFF_HD)
N_GLA = (DEPTH + 2) // 3
N_HGRN = (DEPTH + 1) // 3
N_DIFF = DEPTH // 3

kernel_name = 'hybrid_gla_hgrn2_diffattn_stream_step'


def _rmsnorm(x, w):
    xf = x.astype(jnp.float32)
    y = xf * lax.rsqrt(jnp.mean(xf * xf, axis=-1, keepdims=True) + NORM_EPS)
    return (y * w.astype(jnp.float32)).astype(x.dtype)


def _gated_linear_chunk(S, q, k, v, g):
    c = q.shape[2]
    b = jnp.cumsum(g, axis=2)
    o_inter = jnp.einsum('bhck,bhkv->bhcv', q * jnp.exp(b), S)
    causal = jnp.tril(jnp.ones((c, c), dtype=bool))[:, :, None]
    rel = jnp.where(causal, b[:, :, :, None, :] - b[:, :, None, :, :], -jnp.inf)
    attn = jnp.einsum('bhtk,bhsk,bhtsk->bhts', q, k, jnp.exp(rel))
    o_intra = jnp.einsum('bhts,bhsv->bhtv', attn, v)
    b_last = b[:, :, -1, :]
    k_dec = k * jnp.exp(b_last[:, :, None, :] - b)
    S_new = jnp.exp(b_last)[..., None] * S + jnp.einsum('bhck,bhcv->bhkv', k_dec, v)
    return S_new, o_inter + o_intra


def _gated_linear_recurrence(q, k, v, g, S0):
    B, T, H, _ = q.shape
    c = min(CHUNK, T)
    n = T // c

    def to_chunks(a):
        return a.astype(jnp.float32).reshape(B, n, c, H, a.shape[-1]).transpose(1, 0, 3, 2, 4)

    S, o = lax.scan(lambda s, inp: _gated_linear_chunk(s, *inp), S0.astype(jnp.float32),
                    (to_chunks(q), to_chunks(k), to_chunks(v), to_chunks(g)))
    o = o.transpose(1, 0, 3, 2, 4).reshape(B, T, H, v.shape[-1])
    return o, S


def _gla_branch(h, S0, w_in, w_a1, w_a2, b_a, norm_w, w_out):
    B, T, _ = h.shape
    q, k, v, z = jnp.split(h @ w_in, [GLA_KD, 2 * GLA_KD, 2 * GLA_KD + D_INNER], axis=-1)
    glog = jax.nn.log_sigmoid(((h @ w_a1) @ w_a2 + b_a).astype(jnp.float32)) / GLA_GATE_NORMALIZER
    q = q.reshape(B, T, GLA_HEADS, GLA_DK) * (GLA_DK ** -0.5)
    k = k.reshape(B, T, GLA_HEADS, GLA_DK)
    v = v.reshape(B, T, GLA_HEADS, GLA_DV)
    glog = glog.reshape(B, T, GLA_HEADS, GLA_DK)
    o, S = _gated_linear_recurrence(q, k, v, glog, S0)
    o = _rmsnorm(o, norm_w).reshape(B, T, D_INNER).astype(h.dtype)
    return (o * jax.nn.silu(z)) @ w_out, S


def _hgrn2_branch(h, S0, lower_bound, w_in, norm_w, w_out):
    B, T, _ = h.shape
    q, f, i, z = jnp.split(h @ w_in, 4, axis=-1)
    q = jax.nn.silu(q)
    fgate = lower_bound + (1.0 - lower_bound) * jax.nn.sigmoid(f.astype(jnp.float32))
    k = 1.0 - fgate
    g = jnp.log(fgate)
    q = q.reshape(B, T, HGRN_HEADS, HGRN_DK) * (HGRN_DK ** -0.5)
    k = k.reshape(B, T, HGRN_HEADS, HGRN_DK)
    g = g.reshape(B, T, HGRN_HEADS, HGRN_DK)
    i = i.reshape(B, T, HGRN_HEADS, HGRN_DV)
    o, S = _gated_linear_recurrence(q, k, i, g, S0)
    o = _rmsnorm(o.reshape(B, T, D_INNER), norm_w).astype(h.dtype)
    return (o * jax.nn.silu(z)) @ w_out, S


def _diff_project(h, w_in):
    B, T, _ = h.shape
    q, k, v, z = jnp.split(h @ w_in, 4, axis=-1)
    q = q.reshape(B, T, 2 * DIFF_HEADS, DIFF_HD)
    k = k.reshape(B, T, 2 * DIFF_HEADS, DIFF_HD)
    v = v.reshape(B, T, DIFF_HEADS, 2 * DIFF_HD)
    return q, k, v, z


def _diff_lambda(lam_p, lam_init):
    lp = lam_p.astype(jnp.float32)
    return jnp.exp(jnp.sum(lp[0] * lp[1])) - jnp.exp(jnp.sum(lp[2] * lp[3])) + lam_init


def _diff_attend(q, k, v, lam, mask):
    B, Tq = q.shape[:2]
    Tk = k.shape[1]
    s = jnp.einsum('bqnd,bknd->bnqk', q, k, preferred_element_type=jnp.float32) * (DIFF_HD ** -0.5)
    if mask is not None:
        s = jnp.where(mask, s, -jnp.inf)
    p = jax.nn.softmax(s, axis=-1).reshape(B, DIFF_HEADS, 2, Tq, Tk)
    a = p[:, :, 0] - lam * p[:, :, 1]
    return jnp.einsum('bhqk,bkhe->bqhe', a, v.astype(jnp.float32))


def _diff_attend_prompt(q, k, v, lam):
    B, T = q.shape[:2]
    nb = T // QBLOCK
    qb = q.reshape(B, nb, QBLOCK, 2 * DIFF_HEADS, DIFF_HD).swapaxes(0, 1)
    key_chunk = jnp.arange(T) // CHUNK

    def block(args):
        qblk, start = args
        q_chunk = (start + jnp.arange(QBLOCK)) // CHUNK
        mask = key_chunk[None, :] <= q_chunk[:, None]
        return _diff_attend(qblk, k, v, lam, mask)

    o = lax.map(block, (qb, jnp.arange(nb) * QBLOCK))
    return o.swapaxes(0, 1).reshape(B, T, DIFF_HEADS, 2 * DIFF_HD)


def _diff_output(o, z, lam_init, subln_w, w_out):
    B, T = o.shape[:2]
    o = _rmsnorm(o, subln_w) * (1.0 - lam_init)
    o = o.reshape(B, T, D_INNER).astype(z.dtype)
    return (o * jax.nn.silu(z)) @ w_out


def setup_inputs(seed: int = 0) -> dict:
    key = jax.random.key(seed)
    ks = jax.random.split(key, 24)

    def nrm(k, shape, scale=1.0):
        return jax.random.normal(k, shape, jnp.float32) * scale

    gla_cols = 2 * GLA_KD + 2 * D_INNER
    return {
        'x_prompt': nrm(ks[0], (BATCH, SEQ, D_MODEL)),
        'x_sample': nrm(ks[1], (DEC_BATCH, DEC_SEQ, D_MODEL)),
        'state_gla': nrm(ks[2], (N_GLA, DEC_BATCH, GLA_HEADS, GLA_DK, GLA_DV)),
        'state_hgrn': nrm(ks[3], (N_HGRN, DEC_BATCH, HGRN_HEADS, HGRN_DK, HGRN_DV), 0.5),
        'cache_k': nrm(ks[4], (N_DIFF, DEC_BATCH, PAST_LEN, 2 * DIFF_HEADS, DIFF_HD)),
        'cache_v': nrm(ks[5], (N_DIFF, DEC_BATCH, PAST_LEN, DIFF_HEADS, 2 * DIFF_HD)),
        'norm_w': 1.0 + nrm(ks[6], (DEPTH, D_MODEL), 0.02),
        'final_norm_w': 1.0 + nrm(ks[7], (D_MODEL,), 0.02),
        'gla_w_in': nrm(ks[8], (N_GLA, D_MODEL, gla_cols), D_MODEL ** -0.5),
        'gla_w_a1': nrm(ks[9], (N_GLA, D_MODEL, GLA_GATE_RANK), D_MODEL ** -0.5),
        'gla_w_a2': nrm(ks[10], (N_GLA, GLA_GATE_RANK, GLA_KD), GLA_GATE_RANK ** -0.5),
        'gla_b_a': nrm(ks[11], (N_GLA, GLA_KD), 0.1),
        'gla_norm_w': 1.0 + nrm(ks[12], (N_GLA, GLA_DV), 0.02),
        'gla_w_out': nrm(ks[13], (N_GLA, D_INNER, D_MODEL), D_INNER ** -0.5),
        'hgrn_w_in': nrm(ks[14], (N_HGRN, D_MODEL, 4 * D_INNER), D_MODEL ** -0.5),
        'hgrn_lower_bounds': nrm(ks[15], (DEPTH, D_INNER), 0.1),
        'hgrn_norm_w': 1.0 + nrm(ks[16], (N_HGRN, D_INNER), 0.02),
        'hgrn_w_out': nrm(ks[17], (N_HGRN, D_INNER, D_MODEL), D_INNER ** -0.5),
        'diff_w_in': nrm(ks[18], (N_DIFF, D_MODEL, 4 * D_INNER), D_MODEL ** -0.5),
        'diff_lambda': nrm(ks[19], (N_DIFF, 4, DIFF_HD), 0.1),
        'diff_subln_w': 1.0 + nrm(ks[20], (N_DIFF, 2 * DIFF_HD), 0.02),
        'diff_w_out': nrm(ks[21], (N_DIFF, D_INNER, D_MODEL), D_INNER ** -0.5),
    }


def reference(x_prompt, x_sample, state_gla, state_hgrn, cache_k, cache_v,
              norm_w, final_norm_w,
              gla_w_in, gla_w_a1, gla_w_a2, gla_b_a, gla_norm_w, gla_w_out,
              hgrn_w_in, hgrn_lower_bounds, hgrn_norm_w, hgrn_w_out,
              diff_w_in, diff_lambda, diff_subln_w, diff_w_out):
    lb = jax.nn.softmax(hgrn_lower_bounds.astype(jnp.float32), axis=0)
    lb = jnp.cumsum(lb, axis=0) - lb[0]

    xp, xs = x_prompt, x_sample
    Bp = xp.shape[0]
    gla_p, gla_s, hg_p, hg_s, kp_l, vp_l, ks_l, vs_l = [], [], [], [], [], [], [], []
    ia = ib = ic = 0
    for i in range(DEPTH):
        hp = _rmsnorm(xp, norm_w[i])
        hs = _rmsnorm(xs, norm_w[i])
        kind = i % N_MIXERS
        if kind == 0:
            j = ia
            ia += 1
            w = (gla_w_in[j], gla_w_a1[j], gla_w_a2[j], gla_b_a[j], gla_norm_w[j], gla_w_out[j])
            S0 = jnp.zeros((Bp, GLA_HEADS, GLA_DK, GLA_DV), jnp.float32)
            yp, Sp = _gla_branch(hp, S0, *w)
            ys, Ss = _gla_branch(hs, state_gla[j], *w)
            gla_p.append(Sp)
            gla_s.append(Ss)
        elif kind == 1:
            j = ib
            ib += 1
            w = (hgrn_w_in[j], hgrn_norm_w[j], hgrn_w_out[j])
            S0 = jnp.zeros((Bp, HGRN_HEADS, HGRN_DK, HGRN_DV), jnp.float32)
            yp, Sp = _hgrn2_branch(hp, S0, lb[i], *w)
            ys, Ss = _hgrn2_branch(hs, state_hgrn[j], lb[i], *w)
            hg_p.append(Sp)
            hg_s.append(Ss)
        else:
            j = ic
            ic += 1
            lam_init = 0.8 - 0.6 * math.exp(-0.3 * i)
            lam = _diff_lambda(diff_lambda[j], lam_init)
            qp, kp, vp, zp = _diff_project(hp, diff_w_in[j])
            op = _diff_attend_prompt(qp, kp, vp, lam)
            yp = _diff_output(op, zp, lam_init, diff_subln_w[j], diff_w_out[j])
            qs, kn, vn, zs = _diff_project(hs, diff_w_in[j])
            k_all = jnp.concatenate([cache_k[j].astype(kn.dtype), kn], axis=1)
            v_all = jnp.concatenate([cache_v[j].astype(vn.dtype), vn], axis=1)
            os_ = _diff_attend(qs, k_all, v_all, lam, None)
            ys = _diff_output(os_, zs, lam_init, diff_subln_w[j], diff_w_out[j])
            kp_l.append(kp)
            vp_l.append(vp)
            ks_l.append(kn)
            vs_l.append(vn)
        xp = xp + yp.astype(xp.dtype)
        xs = xs + ys.astype(xs.dtype)

    y_prompt = _rmsnorm(xp, final_norm_w)
    y_sample = _rmsnorm(xs, final_norm_w)
    gla_state_p = jnp.stack(gla_p)
    gla_state_s = jnp.stack(gla_s)
    hgrn_state_p = jnp.stack(hg_p)
    hgrn_state_s = jnp.stack(hg_s)
    k_rows_p = jnp.stack(kp_l)
    v_rows_p = jnp.stack(vp_l)
    k_rows_s = jnp.stack(ks_l)
    v_rows_s = jnp.stack(vs_l)
    return (y_prompt, y_sample, gla_state_p, gla_state_s, hgrn_state_p, hgrn_state_s,
            k_rows_p, v_rows_p, k_rows_s, v_rows_s)
```

```python
import functools
import math

import numpy as np
import jax
import jax.numpy as jnp
from jax import lax
from jax.experimental import pallas as pl
from jax.experimental.pallas import tpu as pltpu

D_MODEL = 2048
DEPTH = 4
CHUNK = 64
N_MIXERS = 3
D_INNER = D_MODEL
NORM_EPS = 1e-6
GLA_HEADS = 4
GLA_KD = D_INNER // 2
GLA_DK = GLA_KD // GLA_HEADS
GLA_DV = D_INNER // GLA_HEADS
GLA_GATE_RANK = 16
GLA_GATE_NORMALIZER = 16.0
HGRN_HEADS = 16
HGRN_DK = 128
HGRN_DV = 128
DIFF_HD = 128
DIFF_HEADS = 8

F32 = jnp.float32
BF16 = jnp.bfloat16

V7X_LANES = 128
V7X_VMEM_BYTES = 64 * 1024 * 1024
VMEM_LIMIT_BYTES = (V7X_VMEM_BYTES * 3) // 4

PROJ_TM = 1024
PROJ_TN = 512
GATE_TM = 512
REC_TOKENS = 256
HGRN_HEADS_PER_STEP = 4
ATT_BQ = 512
ATT_BK = 512
NORM_TM = 512


def _params(sem):
    return pltpu.CompilerParams(dimension_semantics=sem, vmem_limit_bytes=VMEM_LIMIT_BYTES)


def _dot(a, b):
    return jnp.dot(a, b, preferred_element_type=F32)


def _dot_nt(a, b):
    return lax.dot_general(a, b, (((1,), (1,)), ((), ())), preferred_element_type=F32)


def _dot_tn(a, b):
    return lax.dot_general(a, b, (((0,), (0,)), ((), ())), preferred_element_type=F32)


def _sigmoid(x):
    return 1.0 / (1.0 + jnp.exp(-x))


def _silu(x):
    return x * _sigmoid(x)


def _rms(x, w):
    ms = jnp.mean(x * x, axis=-1, keepdims=True)
    return x * lax.rsqrt(ms + NORM_EPS) * w


def _norm_proj_kernel(*refs, has_aux):
    if has_aux:
        x_ref, nw_ref, w_ref, aw_ref, o_ref, ao_ref, h_scr = refs
    else:
        x_ref, nw_ref, w_ref, o_ref, h_scr = refs

    @pl.when(pl.program_id(1) == 0)
    def _():
        h_scr[...] = _rms(x_ref[...], nw_ref[...]).astype(BF16)
        if has_aux:
            ao_ref[...] = _dot(h_scr[...], aw_ref[...])

    o_ref[...] = _dot(h_scr[...], w_ref[...])


def _norm_proj(x, nw, w_bf, aux_w_bf=None):
    m, d = x.shape
    n = w_bf.shape[1]
    tm = min(PROJ_TM, m)
    tn = PROJ_TN
    in_specs = [
        pl.BlockSpec((tm, d), lambda i, j: (i, 0)),
        pl.BlockSpec((1, d), lambda i, j: (0, 0)),
        pl.BlockSpec((d, tn), lambda i, j: (0, j)),
    ]
    out_shape = [jax.ShapeDtypeStruct((m, n), F32)]
    out_specs = [pl.BlockSpec((tm, tn), lambda i, j: (i, j))]
    args = [x, nw.reshape(1, d), w_bf]
    if aux_w_bf is not None:
        na = aux_w_bf.shape[1]
        in_specs.append(pl.BlockSpec((d, na), lambda i, j: (0, 0)))
        out_shape.append(jax.ShapeDtypeStruct((m, na), F32))
        out_specs.append(pl.BlockSpec((tm, na), lambda i, j: (i, 0)))
        args.append(aux_w_bf)
    outs = pl.pallas_call(
        functools.partial(_norm_proj_kernel, has_aux=aux_w_bf is not None),
        grid=(m // tm, n // tn),
        in_specs=in_specs,
        out_specs=out_specs,
        out_shape=out_shape,
        scratch_shapes=[pltpu.VMEM((tm, d), BF16)],
        compiler_params=_params(("parallel", "arbitrary")),
        name="norm_proj",
    )(*args)
    return outs if aux_w_bf is not None else outs[0]


def _gate_proj_kernel(*refs, has_norm):
    if has_norm:
        o_ref, z_ref, nw_ref, res_ref, w_ref, y_ref, h_scr = refs
    else:
        o_ref, z_ref, res_ref, w_ref, y_ref, h_scr = refs

    @pl.when(pl.program_id(1) == 0)
    def _():
        o = o_ref[...]
        if has_norm:
            o = _rms(o, nw_ref[...])
        h_scr[...] = (o * _silu(z_ref[...])).astype(BF16)

    y_ref[...] = res_ref[...] + _dot(h_scr[...], w_ref[...])


def _gate_proj(o, proj, z_col_block, res, w_bf, norm_w=None):
    m, d = o.shape
    n = w_bf.shape[1]
    tm = min(GATE_TM, m)
    tn = PROJ_TN
    in_specs = [
        pl.BlockSpec((tm, d), lambda i, j: (i, 0)),
        pl.BlockSpec((tm, d), lambda i, j: (i, z_col_block)),
    ]
    args = [o, proj]
    if norm_w is not None:
        in_specs.append(pl.BlockSpec((1, d), lambda i, j: (0, 0)))
        args.append(norm_w.reshape(1, d))
    in_specs += [
        pl.BlockSpec((tm, tn), lambda i, j: (i, j)),
        pl.BlockSpec((d, tn), lambda i, j: (0, j)),
    ]
    args += [res, w_bf]
    return pl.pallas_call(
        functools.partial(_gate_proj_kernel, has_norm=norm_w is not None),
        grid=(m // tm, n // tn),
        in_specs=in_specs,
        out_specs=pl.BlockSpec((tm, tn), lambda i, j: (i, j)),
        out_shape=jax.ShapeDtypeStruct((m, n), F32),
        scratch_shapes=[pltpu.VMEM((tm, d), BF16)],
        compiler_params=_params(("parallel", "arbitrary")),
        name="gate_proj",
    )(*args)


def _chunk_step(q, k, v_bf, g, s, tri_bf, causal):
    c, kd = q.shape
    g_hi = g.astype(BF16)
    g_lo = (g - g_hi.astype(F32)).astype(BF16)
    b = _dot(tri_bf, g_hi) + _dot(tri_bf, g_lo)
    b_last = b[c - 1:c, :]
    b_mid = b[c // 2 - 1:c // 2, :]
    e_fwd = jnp.exp(b - b_mid)
    e_bwd = jnp.exp(b_mid - b)
    q_in = q * (e_fwd * jnp.exp(b_mid))
    k_out = k * (e_bwd * jnp.exp(b_last - b_mid))
    attn = _dot_nt((q * e_fwd).astype(BF16), (k * e_bwd).astype(BF16))
    attn = jnp.where(causal, attn, 0.0)
    o = _dot(q_in.astype(BF16), s.astype(BF16)) + _dot(attn.astype(BF16), v_bf)
    decay_rows = jnp.transpose(jnp.broadcast_to(jnp.exp(b_last), (V7X_LANES, kd)))
    s_new = decay_rows[:, 0:1] * s + _dot_tn(k_out.astype(BF16), v_bf)
    return o, s_new


def _tri_and_causal(c):
    row = lax.broadcasted_iota(jnp.int32, (c, c), 0)
    col = lax.broadcasted_iota(jnp.int32, (c, c), 1)
    causal = row >= col
    return causal.astype(BF16), causal


def _log_sigmoid(x):
    return jnp.minimum(x, 0.0) - jnp.log(1.0 + jnp.exp(-jnp.abs(x)))


def _gla_kernel(*refs, chunk, n_chunks, zero_init):
    if zero_init:
        q_ref, k_ref, v_ref, ga_ref, wa2_ref, ba_ref, nw_ref, o_ref, sout_ref, s_scr = refs
    else:
        q_ref, k_ref, v_ref, ga_ref, wa2_ref, ba_ref, nw_ref, s0_ref, o_ref, sout_ref, s_scr = refs
    t = pl.program_id(2)

    @pl.when(t == 0)
    def _():
        if zero_init:
            s_scr[...] = jnp.zeros_like(s_scr)
        else:
            s_scr[...] = s0_ref[0, 0]

    tri_bf, causal = _tri_and_causal(chunk)
    wa2 = wa2_ref[...]
    ba = ba_ref[...]
    nw = nw_ref[...]
    s = s_scr[...]
    for ci in range(n_chunks):
        rows = pl.ds(ci * chunk, chunk)
        q = q_ref[rows, :] * (GLA_DK ** -0.5)
        k = k_ref[rows, :]
        v_bf = v_ref[rows, :].astype(BF16)
        gate_in = _dot(ga_ref[rows, :].astype(BF16), wa2) + ba
        g = _log_sigmoid(gate_in) * (1.0 / GLA_GATE_NORMALIZER)
        o, s = _chunk_step(q, k, v_bf, g, s, tri_bf, causal)
        o_ref[rows, :] = _rms(o, nw)
    s_scr[...] = s

    @pl.when(t == pl.num_programs(2) - 1)
    def _():
        sout_ref[0, 0] = s


def _gla_mix(proj, gate_lr, wa2_bf, b_a, norm_w, s0, batch, seq):
    chunk = min(CHUNK, seq)
    tb = min(REC_TOKENS, seq)
    nt = seq // tb
    kb = GLA_DK
    vb = GLA_DV
    k_off = GLA_KD // kb
    v_off = (2 * GLA_KD) // vb
    zero_init = s0 is None
    in_specs = [
        pl.BlockSpec((tb, kb), lambda b, h, t: (b * nt + t, h)),
        pl.BlockSpec((tb, kb), lambda b, h, t: (b * nt + t, k_off + h)),
        pl.BlockSpec((tb, vb), lambda b, h, t: (b * nt + t, v_off + h)),
        pl.BlockSpec((tb, V7X_LANES), lambda b, h, t: (b * nt + t, 0)),
        pl.BlockSpec((V7X_LANES, kb), lambda b, h, t: (0, h)),
        pl.BlockSpec((1, kb), lambda b, h, t: (0, h)),
        pl.BlockSpec((1, vb), lambda b, h, t: (0, 0)),
    ]
    args = [proj, proj, proj, gate_lr, wa2_bf, b_a.reshape(1, GLA_KD), norm_w.reshape(1, vb)]
    if not zero_init:
        in_specs.append(pl.BlockSpec((1, 1, kb, vb), lambda b, h, t: (b, h, 0, 0)))
        args.append(s0)
    o, s_out = pl.pallas_call(
        functools.partial(_gla_kernel, chunk=chunk, n_chunks=tb // chunk, zero_init=zero_init),
        grid=(batch, GLA_HEADS, nt),
        in_specs=in_specs,
        out_specs=[
            pl.BlockSpec((tb, vb), lambda b, h, t: (b * nt + t, h)),
            pl.BlockSpec((1, 1, kb, vb), lambda b, h, t: (b, h, 0, 0)),
        ],
        out_shape=[
            jax.ShapeDtypeStruct((batch * seq, D_INNER), F32),
            jax.ShapeDtypeStruct((batch, GLA_HEADS, kb, vb), F32),
        ],
        scratch_shapes=[pltpu.VMEM((kb, vb), F32)],
        compiler_params=_params(("parallel", "parallel", "arbitrary")),
        name="gla_mix",
    )(*args)
    return o, s_out


def _hgrn_kernel(*refs, chunk, n_chunks, zero_init, layer, heads):
    if zero_init:
        q_ref, f_ref, i_ref, lbp_ref, o_ref, sout_ref, s_scr = refs
    else:
        q_ref, f_ref, i_ref, lbp_ref, s0_ref, o_ref, sout_ref, s_scr = refs
    t = pl.program_id(2)

    @pl.when(t == 0)
    def _():
        if zero_init:
            s_scr[...] = jnp.zeros_like(s_scr)
        else:
            s_scr[...] = s0_ref[0]

    lbp = lbp_ref[...]
    lbe = jnp.exp(lbp - jnp.max(lbp, axis=0, keepdims=True))
    lbs = lbe / jnp.sum(lbe, axis=0, keepdims=True)
    lb = jnp.zeros_like(lbs[0:1])
    for r in range(1, layer + 1):
        lb = lb + lbs[r:r + 1]

    tri_bf, causal = _tri_and_causal(chunk)
    for hi in range(heads):
        cols = pl.ds(hi * HGRN_DK, HGRN_DK)
        lb_h = lb[:, hi * HGRN_DK:(hi + 1) * HGRN_DK]
        s = s_scr[hi]
        for ci in range(n_chunks):
            rows = pl.ds(ci * chunk, chunk)
            q = _silu(q_ref[rows, cols]) * (HGRN_DK ** -0.5)
            fgate = lb_h + (1.0 - lb_h) * _sigmoid(f_ref[rows, cols])
            k = 1.0 - fgate
            g = jnp.log(fgate)
            v_bf = i_ref[rows, cols].astype(BF16)
            o, s = _chunk_step(q, k, v_bf, g, s, tri_bf, causal)
            o_ref[rows, cols] = o
        s_scr[hi] = s

    @pl.when(t == pl.num_programs(2) - 1)
    def _():
        sout_ref[0] = s_scr[...]


def _hgrn_mix(proj, lower_bounds, layer, s0, batch, seq):
    chunk = min(CHUNK, seq)
    tb = min(REC_TOKENS, seq)
    nt = seq // tb
    hb = HGRN_HEADS_PER_STEP
    wb = hb * HGRN_DK
    ngroups = HGRN_HEADS // hb
    col_groups = D_INNER // wb
    zero_init = s0 is None
    in_specs = [
        pl.BlockSpec((tb, wb), lambda b, h, t: (b * nt + t, h)),
        pl.BlockSpec((tb, wb), lambda b, h, t: (b * nt + t, col_groups + h)),
        pl.BlockSpec((tb, wb), lambda b, h, t: (b * nt + t, 2 * col_groups + h)),
        pl.BlockSpec((DEPTH, wb), lambda b, h, t: (0, h)),
    ]
    args = [proj, proj, proj, lower_bounds]
    if not zero_init:
        in_specs.append(pl.BlockSpec((1, hb, HGRN_DK, HGRN_DV), lambda b, h, t: (b, h, 0, 0)))
        args.append(s0)
    o, s_out = pl.pallas_call(
        functools.partial(_hgrn_kernel, chunk=chunk, n_chunks=tb // chunk,
                          zero_init=zero_init, layer=layer, heads=hb),
        grid=(batch, ngroups, nt),
        in_specs=in_specs,
        out_specs=[
            pl.BlockSpec((tb, wb), lambda b, h, t: (b * nt + t, h)),
            pl.BlockSpec((1, hb, HGRN_DK, HGRN_DV), lambda b, h, t: (b, h, 0, 0)),
        ],
        out_shape=[
            jax.ShapeDtypeStruct((batch * seq, D_INNER), F32),
            jax.ShapeDtypeStruct((batch, HGRN_HEADS, HGRN_DK, HGRN_DV), F32),
        ],
        scratch_shapes=[pltpu.VMEM((hb, HGRN_DK, HGRN_DV), F32)],
        compiler_params=_params(("parallel", "parallel", "arbitrary")),
        name="hgrn_mix",
    )(*args)
    return o, s_out


def _diff_lambda(lam_ref, lam_init):
    lp = lam_ref[...]
    a = jnp.sum(lp[0:1] * lp[1:2], axis=-1, keepdims=True)
    b = jnp.sum(lp[2:3] * lp[3:4], axis=-1, keepdims=True)
    return jnp.exp(a) - jnp.exp(b) + lam_init


def _diff_finish(o1, o2, lam, sw, lam_init):
    o = o1 - lam * o2
    return _rms(o, sw) * (1.0 - lam_init)


def _diff_prompt_kernel(qi_tab, ki_tab, mask_tab, last_tab, q_ref, k_ref, v_ref, lam_ref, sw_ref,
                        o_ref, m_scr, l_scr, acc_scr, *, bq, bk, lam_init):
    p = pl.program_id(2)
    qi = qi_tab[p]
    ki = ki_tab[p]

    @pl.when(ki == 0)
    def _():
        m_scr[...] = jnp.full_like(m_scr, -jnp.inf)
        l_scr[...] = jnp.zeros_like(l_scr)
        acc_scr[...] = jnp.zeros_like(acc_scr)

    def step(masked):
        v_bf = v_ref[...].astype(BF16)
        if masked:
            q_chunk = (qi * bq + lax.broadcasted_iota(jnp.int32, (bq, bk), 0)) // CHUNK
            k_chunk = (ki * bk + lax.broadcasted_iota(jnp.int32, (bq, bk), 1)) // CHUNK
            visible = k_chunk <= q_chunk
        for h in range(2):
            cols = pl.ds(h * DIFF_HD, DIFF_HD)
            s = _dot_nt(q_ref[:, cols].astype(BF16), k_ref[:, cols].astype(BF16)) * (DIFF_HD ** -0.5)
            if masked:
                s = jnp.where(visible, s, -jnp.inf)
            m_prev = m_scr[h]
            m_new = jnp.maximum(m_prev, jnp.max(s, axis=-1, keepdims=True))
            alpha = jnp.exp(m_prev - m_new)
            pr = jnp.exp(s - m_new)
            l_scr[h] = alpha * l_scr[h] + jnp.sum(pr, axis=-1, keepdims=True)
            acc_scr[h] = alpha * acc_scr[h] + _dot(pr.astype(BF16), v_bf)
            m_scr[h] = m_new

    @pl.when(mask_tab[p] == 1)
    def _():
        step(True)

    @pl.when(mask_tab[p] == 0)
    def _():
        step(False)

    @pl.when(last_tab[p] == 1)
    def _():
        lam = _diff_lambda(lam_ref, lam_init)
        o1 = acc_scr[0] * (1.0 / l_scr[0])
        o2 = acc_scr[1] * (1.0 / l_scr[1])
        o_ref[...] = _diff_finish(o1, o2, lam, sw_ref[...], lam_init)


def _diff_prompt_tables(seq, bq, bk):
    qi_l, ki_l, mask_l, last_l = [], [], [], []
    for qi in range(seq // bq):
        q_lo, q_hi = qi * bq, (qi + 1) * bq
        nk = -(-q_hi // bk)
        for ki in range(nk):
            k_hi = (ki + 1) * bk
            fully_visible = k_hi <= (q_lo // CHUNK + 1) * CHUNK
            qi_l.append(qi)
            ki_l.append(ki)
            mask_l.append(0 if fully_visible else 1)
            last_l.append(1 if ki == nk - 1 else 0)
    return [jnp.asarray(np.asarray(a, np.int32)) for a in (qi_l, ki_l, mask_l, last_l)]


def _diff_prompt_attn(proj, lam_p, subln_w, lam_init, batch, seq):
    bq = min(ATT_BQ, seq)
    bk = min(ATT_BK, seq)
    nq = seq // bq
    nkb = seq // bk
    pw = 2 * DIFF_HD
    k_off = D_INNER // pw
    v_off = 2 * D_INNER // pw
    tabs = _diff_prompt_tables(seq, bq, bk)
    n_steps = int(tabs[0].shape[0])
    grid_spec = pltpu.PrefetchScalarGridSpec(
        num_scalar_prefetch=4,
        grid=(batch, DIFF_HEADS, n_steps),
        in_specs=[
            pl.BlockSpec((bq, pw), lambda b, h, p, qt, kt, mt, lt: (b * nq + qt[p], h)),
            pl.BlockSpec((bk, pw), lambda b, h, p, qt, kt, mt, lt: (b * nkb + kt[p], k_off + h)),
            pl.BlockSpec((bk, pw), lambda b, h, p, qt, kt, mt, lt: (b * nkb + kt[p], v_off + h)),
            pl.BlockSpec((4, DIFF_HD), lambda b, h, p, qt, kt, mt, lt: (0, 0)),
            pl.BlockSpec((1, pw), lambda b, h, p, qt, kt, mt, lt: (0, 0)),
        ],
        out_specs=pl.BlockSpec((bq, pw), lambda b, h, p, qt, kt, mt, lt: (b * nq + qt[p], h)),
        scratch_shapes=[
            pltpu.VMEM((2, bq, 1), F32),
            pltpu.VMEM((2, bq, 1), F32),
            pltpu.VMEM((2, bq, pw), F32),
        ],
    )
    return pl.pallas_call(
        functools.partial(_diff_prompt_kernel, bq=bq, bk=bk, lam_init=lam_init),
        grid_spec=grid_spec,
        out_shape=jax.ShapeDtypeStruct((batch * seq, D_INNER), F32),
        compiler_params=_params(("parallel", "parallel", "arbitrary")),
        name="diff_prompt_attn",
    )(*tabs, proj, proj, proj, lam_p, subln_w.reshape(1, pw))


def _diff_sample_kernel(q_ref, kn_ref, vn_ref, kc_ref, vc_ref, lam_ref, sw_ref, o_ref, *, lam_init):
    vc_bf = vc_ref[...].astype(BF16)
    vn_bf = vn_ref[...].astype(BF16)
    outs = []
    for h in range(2):
        cols = pl.ds(h * DIFF_HD, DIFF_HD)
        q_bf = q_ref[:, cols].astype(BF16)
        s_c = _dot_nt(q_bf, kc_ref[:, cols].astype(BF16)) * (DIFF_HD ** -0.5)
        s_n = _dot_nt(q_bf, kn_ref[:, cols].astype(BF16)) * (DIFF_HD ** -0.5)
        m = jnp.maximum(jnp.max(s_c, axis=-1, keepdims=True), jnp.max(s_n, axis=-1, keepdims=True))
        p_c = jnp.exp(s_c - m)
        p_n = jnp.exp(s_n - m)
        l = jnp.sum(p_c, axis=-1, keepdims=True) + jnp.sum(p_n, axis=-1, keepdims=True)
        acc = _dot(p_c.astype(BF16), vc_bf) + _dot(p_n.astype(BF16), vn_bf)
        outs.append(acc * (1.0 / l))
    lam = _diff_lambda(lam_ref, lam_init)
    o_ref[...] = _diff_finish(outs[0], outs[1], lam, sw_ref[...], lam_init)


def _diff_sample_attn(proj, cache_k2, cache_v2, lam_p, subln_w, lam_init, batch, seq, past):
    pw = 2 * DIFF_HD
    k_off = D_INNER // pw
    v_off = 2 * D_INNER // pw
    return pl.pallas_call(
        functools.partial(_diff_sample_kernel, lam_init=lam_init),
        grid=(batch, DIFF_HEADS),
        in_specs=[
            pl.BlockSpec((seq, pw), lambda b, h: (b, h)),
            pl.BlockSpec((seq, pw), lambda b, h: (b, k_off + h)),
            pl.BlockSpec((seq, pw), lambda b, h: (b, v_off + h)),
            pl.BlockSpec((past, pw), lambda b, h: (b, h)),
            pl.BlockSpec((past, pw), lambda b, h: (b, h)),
            pl.BlockSpec((4, DIFF_HD), lambda b, h: (0, 0)),
            pl.BlockSpec((1, pw), lambda b, h: (0, 0)),
        ],
        out_specs=pl.BlockSpec((seq, pw), lambda b, h: (b, h)),
        out_shape=jax.ShapeDtypeStruct((batch * seq, D_INNER), F32),
        compiler_params=_params(("parallel", "parallel")),
        name="diff_sample_attn",
    )(proj, proj, proj, cache_k2, cache_v2, lam_p, subln_w.reshape(1, pw))


def _final_norm_kernel(x_ref, w_ref, o_ref):
    o_ref[...] = _rms(x_ref[...], w_ref[...])


def _final_norm(x, w):
    m, d = x.shape
    tm = min(NORM_TM, m)
    return pl.pallas_call(
        _final_norm_kernel,
        grid=(m // tm,),
        in_specs=[pl.BlockSpec((tm, d), lambda i: (i, 0)), pl.BlockSpec((1, d), lambda i: (0, 0))],
        out_specs=pl.BlockSpec((tm, d), lambda i: (i, 0)),
        out_shape=jax.ShapeDtypeStruct((m, d), F32),
        compiler_params=_params(("parallel",)),
        name="final_norm",
    )(x, w.reshape(1, d))


def kernel(x_prompt, x_sample, state_gla, state_hgrn, cache_k, cache_v, norm_w, final_norm_w,
           gla_w_in, gla_w_a1, gla_w_a2, gla_b_a, gla_norm_w, gla_w_out,
           hgrn_w_in, hgrn_lower_bounds, hgrn_norm_w, hgrn_w_out,
           diff_w_in, diff_lambda, diff_subln_w, diff_w_out):
    bp, tp, d = x_prompt.shape
    bs, ts, _ = x_sample.shape
    past = cache_k.shape[2]
    streams = [(x_prompt.reshape(bp * tp, d), bp, tp), (x_sample.reshape(bs * ts, d), bs, ts)]
    xs = [s[0] for s in streams]

    gla_states = [[], []]
    hgrn_states = [[], []]
    k_rows = [[], []]
    v_rows = [[], []]
    ia = ib = ic = 0
    for i in range(DEPTH):
        kind = i % N_MIXERS
        if kind == 0:
            j = ia
            ia += 1
            w_in = gla_w_in[j].astype(BF16)
            w_a1 = jnp.pad(gla_w_a1[j], ((0, 0), (0, V7X_LANES - GLA_GATE_RANK))).astype(BF16)
            w_a2 = jnp.pad(gla_w_a2[j], ((0, V7X_LANES - GLA_GATE_RANK), (0, 0))).astype(BF16)
            w_out = gla_w_out[j].astype(BF16)
            for si, (_, batch, seq) in enumerate(streams):
                proj, gate_lr = _norm_proj(xs[si], norm_w[i], w_in, w_a1)
                s0 = None if si == 0 else state_gla[j]
                o, s_new = _gla_mix(proj, gate_lr, w_a2, gla_b_a[j], gla_norm_w[j], s0, batch, seq)
                z_block = (2 * GLA_KD + D_INNER) // D_INNER
                xs[si] = _gate_proj(o, proj, z_block, xs[si], w_out)
                gla_states[si].append(s_new)
        elif kind == 1:
            j = ib
            ib += 1
            w_in = hgrn_w_in[j].astype(BF16)
            w_out = hgrn_w_out[j].astype(BF16)
            for si, (_, batch, seq) in enumerate(streams):
                proj = _norm_proj(xs[si], norm_w[i], w_in)
                s0 = None if si == 0 else state_hgrn[j]
                o, s_new = _hgrn_mix(proj, hgrn_lower_bounds, i, s0, batch, seq)
                xs[si] = _gate_proj(o, proj, 3, xs[si], w_out, norm_w=hgrn_norm_w[j])
                hgrn_states[si].append(s_new)
        else:
            j = ic
            ic += 1
            lam_init = 0.8 - 0.6 * math.exp(-0.3 * i)
            w_in = diff_w_in[j].astype(BF16)
            w_out = diff_w_out[j].astype(BF16)
            for si, (_, batch, seq) in enumerate(streams):
                proj = _norm_proj(xs[si], norm_w[i], w_in)
                if si == 0:
                    o = _diff_prompt_attn(proj, diff_lambda[j], diff_subln_w[j], lam_init, batch, seq)
                else:
                    ck = cache_k[j].reshape(batch * past, 2 * DIFF_HEADS * DIFF_HD)
                    cv = cache_v[j].reshape(batch * past, 2 * DIFF_HEADS * DIFF_HD)
                    o = _diff_sample_attn(proj, ck, cv, diff_lambda[j], diff_subln_w[j], lam_init,
                                          batch, seq, past)
                xs[si] = _gate_proj(o, proj, 3, xs[si], w_out)
                k_rows[si].append(proj[:, D_INNER:2 * D_INNER].reshape(batch, seq, 2 * DIFF_HEADS, DIFF_HD))
                v_rows[si].append(proj[:, 2 * D_INNER:3 * D_INNER].reshape(batch, seq, DIFF_HEADS, 2 * DIFF_HD))

    y_prompt = _final_norm(xs[0], final_norm_w).reshape(bp, tp, d)
    y_sample = _final_norm(xs[1], final_norm_w).reshape(bs, ts, d)
    return (y_prompt, y_sample,
            jnp.stack(gla_states[0]), jnp.stack(gla_states[1]),
            jnp.stack(hgrn_states[0]), jnp.stack(hgrn_states[1]),
            jnp.stack(k_rows[0]), jnp.stack(v_rows[0]),
            jnp.stack(k_rows[1]), jnp.stack(v_rows[1]))
```

```python
import functools
import math

import numpy as np
import jax
import jax.numpy as jnp
from jax import lax
from jax.experimental import pallas as pl
from jax.experimental.pallas import tpu as pltpu

D_MODEL = 2048
DEPTH = 4
CHUNK = 64
N_MIXERS = 3
D_INNER = D_MODEL
NORM_EPS = 1e-6
GLA_HEADS = 4
GLA_KD = D_INNER // 2
GLA_DK = GLA_KD // GLA_HEADS
GLA_DV = D_INNER // GLA_HEADS
GLA_GATE_RANK = 16
GLA_GATE_NORMALIZER = 16.0
HGRN_HEADS = 16
HGRN_DK = 128
HGRN_DV = 128
DIFF_HD = 128
DIFF_HEADS = 8

F32 = jnp.float32
BF16 = jnp.bfloat16

V7X_LANES = 128
V7X_VMEM_BYTES = 64 * 1024 * 1024
VMEM_LIMIT_BYTES = (V7X_VMEM_BYTES * 3) // 4

PROJ_TM = 1024
PROJ_TN = 512
GATE_TM = 512
REC_TOKENS = 256
GLA_HEADS_PER_STEP = 4
HGRN_HEADS_PER_STEP = 4
ATT_BQ = 512
ATT_BK = 512
ATT_ROWS = 128
NORM_TM = 512


def _params(sem):
    return pltpu.CompilerParams(dimension_semantics=sem, vmem_limit_bytes=VMEM_LIMIT_BYTES)


def _dot(a, b):
    return jnp.dot(a, b, preferred_element_type=F32)


def _dot_nt(a, b):
    return lax.dot_general(a, b, (((1,), (1,)), ((), ())), preferred_element_type=F32)


def _dot_tn(a, b):
    return lax.dot_general(a, b, (((0,), (0,)), ((), ())), preferred_element_type=F32)


def _sigmoid(x):
    return 1.0 / (1.0 + jnp.exp(-x))


def _silu(x):
    return x * _sigmoid(x)


def _rms(x, w):
    ms = jnp.mean(x * x, axis=-1, keepdims=True)
    return x * lax.rsqrt(ms + NORM_EPS) * w


def _norm_proj_kernel(*refs, has_aux):
    if has_aux:
        x_ref, nw_ref, w_ref, aw_ref, o_ref, ao_ref, h_scr = refs
    else:
        x_ref, nw_ref, w_ref, o_ref, h_scr = refs

    @pl.when(pl.program_id(1) == 0)
    def _():
        h_scr[...] = _rms(x_ref[...], nw_ref[...]).astype(BF16)
        if has_aux:
            ao_ref[...] = _dot(h_scr[...], aw_ref[...])

    o_ref[...] = _dot(h_scr[...], w_ref[...])


def _norm_proj(x, nw, w_bf, aux_w_bf=None):
    m, d = x.shape
    n = w_bf.shape[1]
    tm = min(PROJ_TM, m)
    tn = PROJ_TN
    in_specs = [
        pl.BlockSpec((tm, d), lambda i, j: (i, 0)),
        pl.BlockSpec((1, d), lambda i, j: (0, 0)),
        pl.BlockSpec((d, tn), lambda i, j: (0, j)),
    ]
    out_shape = [jax.ShapeDtypeStruct((m, n), F32)]
    out_specs = [pl.BlockSpec((tm, tn), lambda i, j: (i, j))]
    args = [x, nw.reshape(1, d), w_bf]
    if aux_w_bf is not None:
        na = aux_w_bf.shape[1]
        in_specs.append(pl.BlockSpec((d, na), lambda i, j: (0, 0)))
        out_shape.append(jax.ShapeDtypeStruct((m, na), F32))
        out_specs.append(pl.BlockSpec((tm, na), lambda i, j: (i, 0)))
        args.append(aux_w_bf)
    outs = pl.pallas_call(
        functools.partial(_norm_proj_kernel, has_aux=aux_w_bf is not None),
        grid=(m // tm, n // tn),
        in_specs=in_specs,
        out_specs=out_specs,
        out_shape=out_shape,
        scratch_shapes=[pltpu.VMEM((tm, d), BF16)],
        compiler_params=_params(("parallel", "arbitrary")),
        name="norm_proj",
    )(*args)
    return outs if aux_w_bf is not None else outs[0]


def _res_proj_kernel(h_ref, res_ref, w_ref, y_ref):
    y_ref[...] = res_ref[...] + _dot(h_ref[...], w_ref[...])


def _res_proj(h_bf, res, w_bf):
    m, d = h_bf.shape
    n = w_bf.shape[1]
    tm = min(PROJ_TM, m)
    tn = PROJ_TN
    return pl.pallas_call(
        _res_proj_kernel,
        grid=(m // tm, n // tn),
        in_specs=[
            pl.BlockSpec((tm, d), lambda i, j: (i, 0)),
            pl.BlockSpec((tm, tn), lambda i, j: (i, j)),
            pl.BlockSpec((d, tn), lambda i, j: (0, j)),
        ],
        out_specs=pl.BlockSpec((tm, tn), lambda i, j: (i, j)),
        out_shape=jax.ShapeDtypeStruct((m, n), F32),
        compiler_params=_params(("parallel", "arbitrary")),
        name="res_proj",
    )(h_bf, res, w_bf)


def _gate_proj_kernel(o_ref, z_ref, nw_ref, res_ref, w_ref, y_ref, h_scr):
    @pl.when(pl.program_id(1) == 0)
    def _():
        h_scr[...] = (_rms(o_ref[...], nw_ref[...]) * _silu(z_ref[...])).astype(BF16)

    y_ref[...] = res_ref[...] + _dot(h_scr[...], w_ref[...])


def _gate_proj(o, proj, z_col_block, norm_w, res, w_bf):
    m, d = o.shape
    n = w_bf.shape[1]
    tm = min(GATE_TM, m)
    tn = PROJ_TN
    return pl.pallas_call(
        _gate_proj_kernel,
        grid=(m // tm, n // tn),
        in_specs=[
            pl.BlockSpec((tm, d), lambda i, j: (i, 0)),
            pl.BlockSpec((tm, d), lambda i, j: (i, z_col_block)),
            pl.BlockSpec((1, d), lambda i, j: (0, 0)),
            pl.BlockSpec((tm, tn), lambda i, j: (i, j)),
            pl.BlockSpec((d, tn), lambda i, j: (0, j)),
        ],
        out_specs=pl.BlockSpec((tm, tn), lambda i, j: (i, j)),
        out_shape=jax.ShapeDtypeStruct((m, n), F32),
        scratch_shapes=[pltpu.VMEM((tm, d), BF16)],
        compiler_params=_params(("parallel", "arbitrary")),
        name="gate_proj",
    )(o, proj, norm_w.reshape(1, d), res, w_bf)


def _block_constants(tb, chunk):
    row = np.arange(tb)[:, None]
    col = np.arange(tb)[None, :]
    rc, cc = row // chunk, col // chunk
    same = (rc == cc) & (row >= col)
    dist = np.where(same, 0, np.where(rc > cc, rc - cc, -1)).astype(np.int32)
    return jnp.asarray(same.astype(np.float32), BF16), jnp.asarray(dist)


def _chunk_cumsum(g, tri_bf):
    g_hi = g.astype(BF16)
    g_lo = (g - g_hi.astype(F32)).astype(BF16)
    return _dot(tri_bf, g_hi) + _dot(tri_bf, g_lo)


def _per_chunk(rows, chunk):
    parts = [jnp.broadcast_to(r, (chunk, r.shape[1])) for r in rows]
    return parts[0] if len(parts) == 1 else jnp.concatenate(parts, axis=0)


def _sum_rows(rows, width):
    out = jnp.zeros((1, width), F32)
    for r in rows:
        out = out + r
    return out


def _block_decay_column(b, chunk):
    tb, w = b.shape
    total = _sum_rows([b[(c + 1) * chunk - 1:(c + 1) * chunk, :] for c in range(tb // chunk)], w)
    return jnp.transpose(jnp.broadcast_to(jnp.exp(total), (V7X_LANES, w)))[:, 0:1]


def _block_recurrence(q, k, b, v_bf, s0, dist, dec_col, chunk):
    w = q.shape[1]
    n = q.shape[0] // chunk
    mids = [b[c * chunk + chunk // 2 - 1:c * chunk + chunk // 2, :] for c in range(n)]
    lasts = [b[(c + 1) * chunk - 1:(c + 1) * chunk, :] for c in range(n)]
    b_mid = _per_chunk(mids, chunk)
    e_fwd = jnp.exp(b - b_mid)
    e_bwd = jnp.exp(b_mid - b)
    q_in = q * (e_fwd * _per_chunk([jnp.exp(m) for m in mids], chunk))
    k_out = k * (e_bwd * _per_chunk([jnp.exp(l - m) for l, m in zip(lasts, mids)], chunk))
    attn = jnp.where(dist == 0, _dot_nt((q * e_fwd).astype(BF16), (k * e_bwd).astype(BF16)), 0.0)
    q_in_bf = q_in.astype(BF16)
    for d in range(1, n):
        gaps = [jnp.exp(_sum_rows(lasts[j + 1:j + d], w)) if j + d < n else jnp.zeros((1, w), F32)
                for j in range(n)]
        k_d = k_out if d == 1 else k_out * _per_chunk(gaps, chunk)
        attn = jnp.where(dist == d, _dot_nt(q_in_bf, k_d.astype(BF16)), attn)
    head = _per_chunk([jnp.exp(_sum_rows(lasts[:c], w)) for c in range(n)], chunk)
    tail = _per_chunk([jnp.exp(_sum_rows(lasts[c + 1:], w)) for c in range(n)], chunk)
    o = _dot((q_in * head).astype(BF16), s0.astype(BF16)) + _dot(attn.astype(BF16), v_bf)
    s_end = dec_col * s0 + _dot_tn((k_out * tail).astype(BF16), v_bf)
    return o, s_end


def _log_sigmoid(x):
    return jnp.minimum(x, 0.0) - jnp.log(1.0 + jnp.exp(-jnp.abs(x)))


def _gla_kernel(*refs, chunk, zero_init, heads):
    if zero_init:
        (q_ref, k_ref, v_ref, z_ref, ga_ref, wa2_ref, ba_ref, nw_ref, tri_ref, dist_ref,
         h_ref, sout_ref, s_scr) = refs
    else:
        (q_ref, k_ref, v_ref, z_ref, ga_ref, wa2_ref, ba_ref, nw_ref, tri_ref, dist_ref, s0_ref,
         h_ref, sout_ref, s_scr) = refs
    t = pl.program_id(2)

    @pl.when(t == 0)
    def _():
        if zero_init:
            s_scr[...] = jnp.zeros_like(s_scr)
        else:
            s_scr[...] = s0_ref[0]

    gate_in = _dot(ga_ref[...].astype(BF16), wa2_ref[...]) + ba_ref[...]
    g = _log_sigmoid(gate_in) * (1.0 / GLA_GATE_NORMALIZER)
    b = _chunk_cumsum(g, tri_ref[...])
    dec_col = _block_decay_column(b, chunk)
    dist = dist_ref[...]
    nw = nw_ref[...]
    for hi in range(heads):
        kc = slice(hi * GLA_DK, (hi + 1) * GLA_DK)
        vc = slice(hi * GLA_DV, (hi + 1) * GLA_DV)
        o, s = _block_recurrence(q_ref[:, kc] * (GLA_DK ** -0.5), k_ref[:, kc], b[:, kc],
                                 v_ref[:, vc].astype(BF16), s_scr[hi], dist, dec_col[kc, :], chunk)
        s_scr[hi] = s
        h_ref[:, vc] = (_rms(o, nw) * _silu(z_ref[:, vc])).astype(BF16)

    @pl.when(t == pl.num_programs(2) - 1)
    def _():
        sout_ref[0] = s_scr[...]


def _gla_mix(proj, gate_lr, wa2_bf, b_a, norm_w, s0, batch, seq):
    chunk = min(CHUNK, seq)
    tb = min(REC_TOKENS, seq)
    nt = seq // tb
    hb = GLA_HEADS_PER_STEP
    kb = hb * GLA_DK
    vb = hb * GLA_DV
    k_off = GLA_KD // kb
    v_off = (2 * GLA_KD) // vb
    z_off = (2 * GLA_KD + D_INNER) // vb
    zero_init = s0 is None
    in_specs = [
        pl.BlockSpec((tb, kb), lambda b, h, t: (b * nt + t, h)),
        pl.BlockSpec((tb, kb), lambda b, h, t: (b * nt + t, k_off + h)),
        pl.BlockSpec((tb, vb), lambda b, h, t: (b * nt + t, v_off + h)),
        pl.BlockSpec((tb, vb), lambda b, h, t: (b * nt + t, z_off + h)),
        pl.BlockSpec((tb, V7X_LANES), lambda b, h, t: (b * nt + t, 0)),
        pl.BlockSpec((V7X_LANES, kb), lambda b, h, t: (0, h)),
        pl.BlockSpec((1, kb), lambda b, h, t: (0, h)),
        pl.BlockSpec((1, GLA_DV), lambda b, h, t: (0, 0)),
        pl.BlockSpec((tb, tb), lambda b, h, t: (0, 0)),
        pl.BlockSpec((tb, tb), lambda b, h, t: (0, 0)),
    ]
    args = [proj, proj, proj, proj, gate_lr, wa2_bf, b_a.reshape(1, GLA_KD), norm_w.reshape(1, GLA_DV),
            *_block_constants(tb, chunk)]
    if not zero_init:
        in_specs.append(pl.BlockSpec((1, hb, GLA_DK, GLA_DV), lambda b, h, t: (b, h, 0, 0)))
        args.append(s0)
    h, s_out = pl.pallas_call(
        functools.partial(_gla_kernel, chunk=chunk, zero_init=zero_init, heads=hb),
        grid=(batch, GLA_HEADS // hb, nt),
        in_specs=in_specs,
        out_specs=[
            pl.BlockSpec((tb, vb), lambda b, h, t: (b * nt + t, h)),
            pl.BlockSpec((1, hb, GLA_DK, GLA_DV), lambda b, h, t: (b, h, 0, 0)),
        ],
        out_shape=[
            jax.ShapeDtypeStruct((batch * seq, D_INNER), BF16),
            jax.ShapeDtypeStruct((batch, GLA_HEADS, GLA_DK, GLA_DV), F32),
        ],
        scratch_shapes=[pltpu.VMEM((hb, GLA_DK, GLA_DV), F32)],
        compiler_params=_params(("parallel", "parallel", "arbitrary")),
        name="gla_mix",
    )(*args)
    return h, s_out


def _hgrn_kernel(*refs, chunk, zero_init, layer, heads):
    if zero_init:
        q_ref, f_ref, i_ref, lbp_ref, tri_ref, dist_ref, o_ref, sout_ref, s_scr = refs
    else:
        q_ref, f_ref, i_ref, lbp_ref, tri_ref, dist_ref, s0_ref, o_ref, sout_ref, s_scr = refs
    t = pl.program_id(2)

    @pl.when(t == 0)
    def _():
        if zero_init:
            s_scr[...] = jnp.zeros_like(s_scr)
        else:
            s_scr[...] = s0_ref[0]

    lbp = lbp_ref[...]
    lbe = jnp.exp(lbp - jnp.max(lbp, axis=0, keepdims=True))
    lbs = lbe / jnp.sum(lbe, axis=0, keepdims=True)
    lb = jnp.zeros_like(lbs[0:1])
    for r in range(1, layer + 1):
        lb = lb + lbs[r:r + 1]

    q = _silu(q_ref[...]) * (HGRN_DK ** -0.5)
    fgate = lb + (1.0 - lb) * _sigmoid(f_ref[...])
    k = 1.0 - fgate
    b = _chunk_cumsum(jnp.log(fgate), tri_ref[...])
    dec_col = _block_decay_column(b, chunk)
    v_bf = i_ref[...].astype(BF16)
    dist = dist_ref[...]
    for hi in range(heads):
        cols = slice(hi * HGRN_DK, (hi + 1) * HGRN_DK)
        o, s = _block_recurrence(q[:, cols], k[:, cols], b[:, cols], v_bf[:, cols], s_scr[hi],
                                 dist, dec_col[cols, :], chunk)
        o_ref[:, cols] = o
        s_scr[hi] = s

    @pl.when(t == pl.num_programs(2) - 1)
    def _():
        sout_ref[0] = s_scr[...]


def _hgrn_mix(proj, lower_bounds, layer, s0, batch, seq):
    chunk = min(CHUNK, seq)
    tb = min(REC_TOKENS, seq)
    nt = seq // tb
    hb = HGRN_HEADS_PER_STEP
    wb = hb * HGRN_DK
    ngroups = HGRN_HEADS // hb
    col_groups = D_INNER // wb
    zero_init = s0 is None
    in_specs = [
        pl.BlockSpec((tb, wb), lambda b, h, t: (b * nt + t, h)),
        pl.BlockSpec((tb, wb), lambda b, h, t: (b * nt + t, col_groups + h)),
        pl.BlockSpec((tb, wb), lambda b, h, t: (b * nt + t, 2 * col_groups + h)),
        pl.BlockSpec((DEPTH, wb), lambda b, h, t: (0, h)),
        pl.BlockSpec((tb, tb), lambda b, h, t: (0, 0)),
        pl.BlockSpec((tb, tb), lambda b, h, t: (0, 0)),
    ]
    args = [proj, proj, proj, lower_bounds, *_block_constants(tb, chunk)]
    if not zero_init:
        in_specs.append(pl.BlockSpec((1, hb, HGRN_DK, HGRN_DV), lambda b, h, t: (b, h, 0, 0)))
        args.append(s0)
    o, s_out = pl.pallas_call(
        functools.partial(_hgrn_kernel, chunk=chunk, zero_init=zero_init, layer=layer, heads=hb),
        grid=(batch, ngroups, nt),
        in_specs=in_specs,
        out_specs=[
            pl.BlockSpec((tb, wb), lambda b, h, t: (b * nt + t, h)),
            pl.BlockSpec((1, hb, HGRN_DK, HGRN_DV), lambda b, h, t: (b, h, 0, 0)),
        ],
        out_shape=[
            jax.ShapeDtypeStruct((batch * seq, D_INNER), F32),
            jax.ShapeDtypeStruct((batch, HGRN_HEADS, HGRN_DK, HGRN_DV), F32),
        ],
        scratch_shapes=[pltpu.VMEM((hb, HGRN_DK, HGRN_DV), F32)],
        compiler_params=_params(("parallel", "parallel", "arbitrary")),
        name="hgrn_mix",
    )(*args)
    return o, s_out


DIFF_Q_SCALE = (DIFF_HD ** -0.5) * math.log2(math.e)


def _diff_lambda(lam_ref, lam_init):
    lp = lam_ref[...]
    a = jnp.sum(lp[0:1] * lp[1:2], axis=-1, keepdims=True)
    b = jnp.sum(lp[2:3] * lp[3:4], axis=-1, keepdims=True)
    return jnp.exp(a) - jnp.exp(b) + lam_init


def _diff_finish(o1, o2, lam_ref, sw_ref, z_ref, lam_init):
    o = o1 - _diff_lambda(lam_ref, lam_init) * o2
    o = _rms(o, sw_ref[...]) * (1.0 - lam_init)
    return (o * _silu(z_ref[...])).astype(BF16)


def _diff_prompt_kernel(qi_tab, ki_tab, mask_tab, last_tab, q_ref, k_ref, v_ref, z_ref, lam_ref, sw_ref,
                        h_ref, qs_scr, s_scr, p_scr, m_scr, l_scr, a_scr, acc_scr,
                        *, bq, bk, rows_per_tile, lam_init):
    p = pl.program_id(2)
    qi = qi_tab[p]
    ki = ki_tab[p]

    @pl.when(ki == 0)
    def _():
        for h in range(2):
            qs_scr[h] = (q_ref[:, h * DIFF_HD:(h + 1) * DIFF_HD] * DIFF_Q_SCALE).astype(BF16)
        m_scr[...] = jnp.full_like(m_scr, -jnp.inf)
        l_scr[...] = jnp.zeros_like(l_scr)
        acc_scr[...] = jnp.zeros_like(acc_scr)

    def step(masked):
        for h in range(2):
            k_bf = k_ref[:, h * DIFF_HD:(h + 1) * DIFF_HD].astype(BF16)
            s_scr[h] = _dot_nt(qs_scr[h], k_bf)
        for h in range(2):
            for r in range(bq // rows_per_tile):
                rows = pl.ds(r * rows_per_tile, rows_per_tile)
                s = s_scr[h, rows, :]
                if masked:
                    shape = (rows_per_tile, bk)
                    q_chunk = (qi * bq + r * rows_per_tile + lax.broadcasted_iota(jnp.int32, shape, 0)) // CHUNK
                    k_chunk = (ki * bk + lax.broadcasted_iota(jnp.int32, shape, 1)) // CHUNK
                    s = jnp.where(k_chunk <= q_chunk, s, -jnp.inf)
                m_prev = m_scr[h, rows, :]
                m_new = jnp.maximum(m_prev, jnp.max(s, axis=-1, keepdims=True))
                alpha = jnp.exp2(m_prev - m_new)
                pr = jnp.exp2(s - jnp.tile(m_new, (1, bk // V7X_LANES)))
                l_scr[h, rows, :] = alpha * l_scr[h, rows, :] + jnp.sum(pr, axis=-1, keepdims=True)
                m_scr[h, rows, :] = m_new
                a_scr[h, rows, :] = alpha
                p_scr[h, rows, :] = pr.astype(BF16)
        v_bf = v_ref[...].astype(BF16)
        for h in range(2):
            alpha = jnp.tile(a_scr[h], (1, 2 * DIFF_HD // V7X_LANES))
            acc_scr[h] = alpha * acc_scr[h] + _dot(p_scr[h], v_bf)

    @pl.when(mask_tab[p] == 1)
    def _():
        step(True)

    @pl.when(mask_tab[p] == 0)
    def _():
        step(False)

    @pl.when(last_tab[p] == 1)
    def _():
        reps = (1, 2 * DIFF_HD // V7X_LANES)
        o1 = acc_scr[0] * jnp.tile(1.0 / l_scr[0], reps)
        o2 = acc_scr[1] * jnp.tile(1.0 / l_scr[1], reps)
        h_ref[...] = _diff_finish(o1, o2, lam_ref, sw_ref, z_ref, lam_init)


def _diff_prompt_tables(seq, bq, bk):
    qi_l, ki_l, mask_l, last_l = [], [], [], []
    for qi in range(seq // bq):
        q_lo, q_hi = qi * bq, (qi + 1) * bq
        nk = -(-q_hi // bk)
        for ki in range(nk):
            k_hi = (ki + 1) * bk
            fully_visible = k_hi <= (q_lo // CHUNK + 1) * CHUNK
            qi_l.append(qi)
            ki_l.append(ki)
            mask_l.append(0 if fully_visible else 1)
            last_l.append(1 if ki == nk - 1 else 0)
    return [jnp.asarray(np.asarray(a, np.int32)) for a in (qi_l, ki_l, mask_l, last_l)]


def _diff_prompt_attn(proj, lam_p, subln_w, lam_init, batch, seq):
    bq = min(ATT_BQ, seq)
    bk = min(ATT_BK, seq)
    nq = seq // bq
    nkb = seq // bk
    pw = 2 * DIFF_HD
    k_off = D_INNER // pw
    v_off = 2 * D_INNER // pw
    z_off = 3 * D_INNER // pw
    tabs = _diff_prompt_tables(seq, bq, bk)
    n_steps = int(tabs[0].shape[0])
    grid_spec = pltpu.PrefetchScalarGridSpec(
        num_scalar_prefetch=4,
        grid=(batch, DIFF_HEADS, n_steps),
        in_specs=[
            pl.BlockSpec((bq, pw), lambda b, h, p, qt, kt, mt, lt: (b * nq + qt[p], h)),
            pl.BlockSpec((bk, pw), lambda b, h, p, qt, kt, mt, lt: (b * nkb + kt[p], k_off + h)),
            pl.BlockSpec((bk, pw), lambda b, h, p, qt, kt, mt, lt: (b * nkb + kt[p], v_off + h)),
            pl.BlockSpec((bq, pw), lambda b, h, p, qt, kt, mt, lt: (b * nq + qt[p], z_off + h)),
            pl.BlockSpec((4, DIFF_HD), lambda b, h, p, qt, kt, mt, lt: (0, 0)),
            pl.BlockSpec((1, pw), lambda b, h, p, qt, kt, mt, lt: (0, 0)),
        ],
        out_specs=pl.BlockSpec((bq, pw), lambda b, h, p, qt, kt, mt, lt: (b * nq + qt[p], h)),
        scratch_shapes=[
            pltpu.VMEM((2, bq, DIFF_HD), BF16),
            pltpu.VMEM((2, bq, bk), F32),
            pltpu.VMEM((2, bq, bk), BF16),
            pltpu.VMEM((2, bq, V7X_LANES), F32),
            pltpu.VMEM((2, bq, V7X_LANES), F32),
            pltpu.VMEM((2, bq, V7X_LANES), F32),
            pltpu.VMEM((2, bq, pw), F32),
        ],
    )
    return pl.pallas_call(
        functools.partial(_diff_prompt_kernel, bq=bq, bk=bk, rows_per_tile=min(ATT_ROWS, bq),
                          lam_init=lam_init),
        grid_spec=grid_spec,
        out_shape=jax.ShapeDtypeStruct((batch * seq, D_INNER), BF16),
        compiler_params=_params(("parallel", "parallel", "arbitrary")),
        name="diff_prompt_attn",
    )(*tabs, proj, proj, proj, proj, lam_p, subln_w.reshape(1, pw))


def _diff_sample_kernel(q_ref, kn_ref, vn_ref, z_ref, kc_ref, vc_ref, lam_ref, sw_ref, h_ref, *, lam_init):
    vc_bf = vc_ref[...].astype(BF16)
    vn_bf = vn_ref[...].astype(BF16)
    outs = []
    for h in range(2):
        cols = pl.ds(h * DIFF_HD, DIFF_HD)
        q_bf = (q_ref[:, cols] * DIFF_Q_SCALE).astype(BF16)
        s_c = _dot_nt(q_bf, kc_ref[:, cols].astype(BF16))
        s_n = _dot_nt(q_bf, kn_ref[:, cols].astype(BF16))
        m = jnp.maximum(jnp.max(s_c, axis=-1, keepdims=True), jnp.max(s_n, axis=-1, keepdims=True))
        p_c = jnp.exp2(s_c - m)
        p_n = jnp.exp2(s_n - m)
        l = jnp.sum(p_c, axis=-1, keepdims=True) + jnp.sum(p_n, axis=-1, keepdims=True)
        acc = _dot(p_c.astype(BF16), vc_bf) + _dot(p_n.astype(BF16), vn_bf)
        outs.append(acc * (1.0 / l))
    h_ref[...] = _diff_finish(outs[0], outs[1], lam_ref, sw_ref, z_ref, lam_init)


def _diff_sample_attn(proj, cache_k2, cache_v2, lam_p, subln_w, lam_init, batch, seq, past):
    pw = 2 * DIFF_HD
    k_off = D_INNER // pw
    v_off = 2 * D_INNER // pw
    z_off = 3 * D_INNER // pw
    return pl.pallas_call(
        functools.partial(_diff_sample_kernel, lam_init=lam_init),
        grid=(batch, DIFF_HEADS),
        in_specs=[
            pl.BlockSpec((seq, pw), lambda b, h: (b, h)),
            pl.BlockSpec((seq, pw), lambda b, h: (b, k_off + h)),
            pl.BlockSpec((seq, pw), lambda b, h: (b, v_off + h)),
            pl.BlockSpec((seq, pw), lambda b, h: (b, z_off + h)),
            pl.BlockSpec((past, pw), lambda b, h: (b, h)),
            pl.BlockSpec((past, pw), lambda b, h: (b, h)),
            pl.BlockSpec((4, DIFF_HD), lambda b, h: (0, 0)),
            pl.BlockSpec((1, pw), lambda b, h: (0, 0)),
        ],
        out_specs=pl.BlockSpec((seq, pw), lambda b, h: (b, h)),
        out_shape=jax.ShapeDtypeStruct((batch * seq, D_INNER), BF16),
        compiler_params=_params(("parallel", "parallel")),
        name="diff_sample_attn",
    )(proj, proj, proj, proj, cache_k2, cache_v2, lam_p, subln_w.reshape(1, pw))


def _final_norm_kernel(x_ref, w_ref, o_ref):
    o_ref[...] = _rms(x_ref[...], w_ref[...])


def _final_norm(x, w):
    m, d = x.shape
    tm = min(NORM_TM, m)
    return pl.pallas_call(
        _final_norm_kernel,
        grid=(m // tm,),
        in_specs=[pl.BlockSpec((tm, d), lambda i: (i, 0)), pl.BlockSpec((1, d), lambda i: (0, 0))],
        out_specs=pl.BlockSpec((tm, d), lambda i: (i, 0)),
        out_shape=jax.ShapeDtypeStruct((m, d), F32),
        compiler_params=_params(("parallel",)),
        name="final_norm",
    )(x, w.reshape(1, d))


def kernel(x_prompt, x_sample, state_gla, state_hgrn, cache_k, cache_v, norm_w, final_norm_w,
           gla_w_in, gla_w_a1, gla_w_a2, gla_b_a, gla_norm_w, gla_w_out,
           hgrn_w_in, hgrn_lower_bounds, hgrn_norm_w, hgrn_w_out,
           diff_w_in, diff_lambda, diff_subln_w, diff_w_out):
    bp, tp, d = x_prompt.shape
    bs, ts, _ = x_sample.shape
    past = cache_k.shape[2]
    streams = [(bp, tp), (bs, ts)]
    xs = [x_prompt.reshape(bp * tp, d), x_sample.reshape(bs * ts, d)]

    gla_states = [[], []]
    hgrn_states = [[], []]
    k_rows = [[], []]
    v_rows = [[], []]
    ia = ib = ic = 0
    for i in range(DEPTH):
        kind = i % N_MIXERS
        if kind == 0:
            j = ia
            ia += 1
            w_in = gla_w_in[j].astype(BF16)
            w_a1 = jnp.pad(gla_w_a1[j], ((0, 0), (0, V7X_LANES - GLA_GATE_RANK))).astype(BF16)
            w_a2 = jnp.pad(gla_w_a2[j], ((0, V7X_LANES - GLA_GATE_RANK), (0, 0))).astype(BF16)
            w_out = gla_w_out[j].astype(BF16)
            for si, (batch, seq) in enumerate(streams):
                proj, gate_lr = _norm_proj(xs[si], norm_w[i], w_in, w_a1)
                s0 = None if si == 0 else state_gla[j]
                h, s_new = _gla_mix(proj, gate_lr, w_a2, gla_b_a[j], gla_norm_w[j], s0, batch, seq)
                xs[si] = _res_proj(h, xs[si], w_out)
                gla_states[si].append(s_new)
        elif kind == 1:
            j = ib
            ib += 1
            w_in = hgrn_w_in[j].astype(BF16)
            w_out = hgrn_w_out[j].astype(BF16)
            for si, (batch, seq) in enumerate(streams):
                proj = _norm_proj(xs[si], norm_w[i], w_in)
                s0 = None if si == 0 else state_hgrn[j]
                o, s_new = _hgrn_mix(proj, hgrn_lower_bounds, i, s0, batch, seq)
                xs[si] = _gate_proj(o, proj, 3, hgrn_norm_w[j], xs[si], w_out)
                hgrn_states[si].append(s_new)
        else:
            j = ic
            ic += 1
            lam_init = 0.8 - 0.6 * math.exp(-0.3 * i)
            w_in = diff_w_in[j].astype(BF16)
            w_out = diff_w_out[j].astype(BF16)
            for si, (batch, seq) in enumerate(streams):
                proj = _norm_proj(xs[si], norm_w[i], w_in)
                if si == 0:
                    h = _diff_prompt_attn(proj, diff_lambda[j], diff_subln_w[j], lam_init, batch, seq)
                else:
                    ck = cache_k[j].reshape(batch * past, 2 * DIFF_HEADS * DIFF_HD)
                    cv = cache_v[j].reshape(batch * past, 2 * DIFF_HEADS * DIFF_HD)
                    h = _diff_sample_attn(proj, ck, cv, diff_lambda[j], diff_subln_w[j], lam_init,
                                          batch, seq, past)
                xs[si] = _res_proj(h, xs[si], w_out)
                k_rows[si].append(proj[:, D_INNER:2 * D_INNER].reshape(batch, seq, 2 * DIFF_HEADS, DIFF_HD))
                v_rows[si].append(proj[:, 2 * D_INNER:3 * D_INNER].reshape(batch, seq, DIFF_HEADS, 2 * DIFF_HD))

    y_prompt = _final_norm(xs[0], final_norm_w).reshape(bp, tp, d)
    y_sample = _final_norm(xs[1], final_norm_w).reshape(bs, ts, d)
    return (y_prompt, y_sample,
            jnp.stack(gla_states[0]), jnp.stack(gla_states[1]),
            jnp.stack(hgrn_states[0]), jnp.stack(hgrn_states[1]),
            jnp.stack(k_rows[0]), jnp.stack(v_rows[0]),
            jnp.stack(k_rows[1]), jnp.stack(v_rows[1]))
```

```python
import functools
import math

import numpy as np
import jax
import jax.numpy as jnp
from jax import lax
from jax.experimental import pallas as pl
from jax.experimental.pallas import tpu as pltpu

D_MODEL = 2048
DEPTH = 4
CHUNK = 64
N_MIXERS = 3
D_INNER = D_MODEL
NORM_EPS = 1e-6
GLA_HEADS = 4
GLA_KD = D_INNER // 2
GLA_DK = GLA_KD // GLA_HEADS
GLA_DV = D_INNER // GLA_HEADS
GLA_GATE_RANK = 16
GLA_GATE_NORMALIZER = 16.0
HGRN_HEADS = 16
HGRN_DK = 128
HGRN_DV = 128
DIFF_HD = 128
DIFF_HEADS = 8

DIFF_Q_SCALE = (DIFF_HD ** -0.5) * math.log2(math.e)

F32 = jnp.float32
BF16 = jnp.bfloat16

V7X_LANES = 128
V7X_VMEM_BYTES = 64 * 1024 * 1024
VMEM_LIMIT_BYTES = (V7X_VMEM_BYTES * 3) // 4

PROJ_TM = 1024
PROJ_TN = 1024
DIFF_TN = 512
OUT_TN = 512
FINAL_TM = 512
REC_TOKENS = 256
GLA_HEADS_PER_STEP = 4
HGRN_HEADS_PER_STEP = 4
ATT_BQ = 512
ATT_BK = 512
ATT_ROWS = 128


def _params(sem):
    return pltpu.CompilerParams(dimension_semantics=sem, vmem_limit_bytes=VMEM_LIMIT_BYTES)


def _dot(a, b):
    return jnp.dot(a, b, preferred_element_type=F32)


def _dot_nt(a, b):
    return lax.dot_general(a, b, (((1,), (1,)), ((), ())), preferred_element_type=F32)


def _dot_tn(a, b):
    return lax.dot_general(a, b, (((0,), (0,)), ((), ())), preferred_element_type=F32)


def _sigmoid(x):
    return 1.0 / (1.0 + jnp.exp(-x))


def _silu(x):
    return x * _sigmoid(x)


def _rms(x, w):
    ms = jnp.mean(x * x, axis=-1, keepdims=True)
    return x * lax.rsqrt(ms + NORM_EPS) * w


def _norm_proj_kernel(*refs, has_aux):
    if has_aux:
        x_ref, nw_ref, w_ref, aw_ref, o_ref, ao_ref, h_scr = refs
    else:
        x_ref, nw_ref, w_ref, o_ref, h_scr = refs

    @pl.when(pl.program_id(1) == 0)
    def _():
        h_scr[...] = _rms(x_ref[...], nw_ref[...]).astype(BF16)
        if has_aux:
            ao_ref[...] = _dot(h_scr[...], aw_ref[...])

    o_ref[...] = _dot(h_scr[...], w_ref[...])


def _norm_proj(x, nw, w_bf, aux_w_bf=None):
    m, d = x.shape
    n = w_bf.shape[1]
    tm = min(PROJ_TM, m)
    tn = PROJ_TN
    in_specs = [
        pl.BlockSpec((tm, d), lambda i, j: (i, 0)),
        pl.BlockSpec((1, d), lambda i, j: (0, 0)),
        pl.BlockSpec((d, tn), lambda i, j: (0, j)),
    ]
    out_shape = [jax.ShapeDtypeStruct((m, n), F32)]
    out_specs = [pl.BlockSpec((tm, tn), lambda i, j: (i, j))]
    args = [x, nw.reshape(1, d), w_bf]
    if aux_w_bf is not None:
        na = aux_w_bf.shape[1]
        in_specs.append(pl.BlockSpec((d, na), lambda i, j: (0, 0)))
        out_shape.append(jax.ShapeDtypeStruct((m, na), F32))
        out_specs.append(pl.BlockSpec((tm, na), lambda i, j: (i, 0)))
        args.append(aux_w_bf)
    outs = pl.pallas_call(
        functools.partial(_norm_proj_kernel, has_aux=aux_w_bf is not None),
        grid=(m // tm, n // tn),
        in_specs=in_specs,
        out_specs=out_specs,
        out_shape=out_shape,
        scratch_shapes=[pltpu.VMEM((tm, d), BF16)],
        compiler_params=_params(("parallel", "arbitrary")),
        name="norm_proj",
    )(*args)
    return outs if aux_w_bf is not None else outs[0]


def _diff_proj_kernel(x_ref, nw_ref, w_ref, pm_ref, k_ref, v_ref, z_ref, h_scr, *, group_tiles):
    j = pl.program_id(1)

    @pl.when(j == 0)
    def _():
        h_scr[...] = _rms(x_ref[...], nw_ref[...]).astype(BF16)

    acc = _dot(h_scr[...], w_ref[...])
    pw = pm_ref.shape[2]

    def write_pairs(val):
        for p in range(pm_ref.shape[0]):
            pm_ref[p] = val[:, p * pw:(p + 1) * pw].astype(BF16)

    @pl.when(j < group_tiles)
    def _():
        write_pairs(acc * DIFF_Q_SCALE)

    @pl.when((j >= group_tiles) & (j < 2 * group_tiles))
    def _():
        write_pairs(acc)
        k_ref[...] = acc

    @pl.when((j >= 2 * group_tiles) & (j < 3 * group_tiles))
    def _():
        write_pairs(acc)
        v_ref[...] = acc

    @pl.when(j >= 3 * group_tiles)
    def _():
        z_ref[...] = acc


def _diff_proj(x, nw, w_bf):
    m, d = x.shape
    tm = min(PROJ_TM, m)
    tn = DIFF_TN
    pw = 2 * DIFF_HD
    g = D_INNER // tn
    ppt = tn // pw

    def rows_spec(group):
        return pl.BlockSpec((tm, tn), lambda i, j: (i, jnp.clip(j - group * g, 0, g - 1)))

    return pl.pallas_call(
        functools.partial(_diff_proj_kernel, group_tiles=g),
        grid=(m // tm, 4 * g),
        in_specs=[
            pl.BlockSpec((tm, d), lambda i, j: (i, 0)),
            pl.BlockSpec((1, d), lambda i, j: (0, 0)),
            pl.BlockSpec((d, tn), lambda i, j: (0, j)),
        ],
        out_specs=[
            pl.BlockSpec((ppt, tm, pw), lambda i, j: (jnp.minimum(j, 3 * g - 1), i, 0)),
            rows_spec(1), rows_spec(2), rows_spec(3),
        ],
        out_shape=[
            jax.ShapeDtypeStruct((3 * DIFF_HEADS, m, pw), BF16),
            jax.ShapeDtypeStruct((m, D_INNER), F32),
            jax.ShapeDtypeStruct((m, D_INNER), F32),
            jax.ShapeDtypeStruct((m, D_INNER), F32),
        ],
        scratch_shapes=[pltpu.VMEM((tm, d), BF16)],
        compiler_params=_params(("parallel", "arbitrary")),
        name="diff_proj",
    )(x, nw.reshape(1, d), w_bf)


def _row_rms_scale(ssq_ref):
    ssq = ssq_ref[0]
    for gi in range(1, ssq_ref.shape[0]):
        ssq = ssq + ssq_ref[gi]
    return lax.rsqrt(ssq * (1.0 / D_INNER) + NORM_EPS)


def _res_proj_kernel(*refs, has_ssq):
    if has_ssq:
        h_ref, ssq_ref, res_ref, w_ref, y_ref = refs
    else:
        h_ref, res_ref, w_ref, y_ref = refs
    acc = _dot(h_ref[...], w_ref[...])
    if has_ssq:
        acc = acc * _row_rms_scale(ssq_ref)
    y_ref[...] = res_ref[...] + acc


def _res_proj(h_bf, res, w_bf, ssq=None):
    m, d = h_bf.shape
    n = w_bf.shape[1]
    tm = min(PROJ_TM, m)
    tn = OUT_TN
    in_specs = [pl.BlockSpec((tm, d), lambda i, j: (i, 0))]
    args = [h_bf]
    if ssq is not None:
        in_specs.append(pl.BlockSpec((ssq.shape[0], tm, 1), lambda i, j: (0, i, 0)))
        args.append(ssq)
    in_specs += [
        pl.BlockSpec((tm, tn), lambda i, j: (i, j)),
        pl.BlockSpec((d, tn), lambda i, j: (0, j)),
    ]
    args += [res, w_bf]
    return pl.pallas_call(
        functools.partial(_res_proj_kernel, has_ssq=ssq is not None),
        grid=(m // tm, n // tn),
        in_specs=in_specs,
        out_specs=pl.BlockSpec((tm, tn), lambda i, j: (i, j)),
        out_shape=jax.ShapeDtypeStruct((m, n), F32),
        compiler_params=_params(("parallel", "arbitrary")),
        name="res_proj",
    )(*args)


def _final_proj_kernel(h_ref, res_ref, w_ref, fw_ref, y_ref):
    y_ref[...] = _rms(res_ref[...] + _dot(h_ref[...], w_ref[...]), fw_ref[...])


def _final_proj(h_bf, res, w_bf, final_w):
    m, d = h_bf.shape
    n = w_bf.shape[1]
    tm = min(FINAL_TM, m)
    return pl.pallas_call(
        _final_proj_kernel,
        grid=(m // tm,),
        in_specs=[
            pl.BlockSpec((tm, d), lambda i: (i, 0)),
            pl.BlockSpec((tm, n), lambda i: (i, 0)),
            pl.BlockSpec((d, n), lambda i: (0, 0)),
            pl.BlockSpec((1, n), lambda i: (0, 0)),
        ],
        out_specs=pl.BlockSpec((tm, n), lambda i: (i, 0)),
        out_shape=jax.ShapeDtypeStruct((m, n), F32),
        compiler_params=_params(("parallel",)),
        name="final_proj",
    )(h_bf, res, w_bf, final_w.reshape(1, n))


def _block_constants(tb, chunk):
    row = np.arange(tb)[:, None]
    col = np.arange(tb)[None, :]
    rc, cc = row // chunk, col // chunk
    same = (rc == cc) & (row >= col)
    dist = np.where(same, 0, np.where(rc > cc, rc - cc, -1)).astype(np.int32)
    return jnp.asarray(same.astype(np.float32), BF16), jnp.asarray(dist)


def _chunk_cumsum(g, tri_bf):
    g_hi = g.astype(BF16)
    g_lo = (g - g_hi.astype(F32)).astype(BF16)
    return _dot(tri_bf, g_hi) + _dot(tri_bf, g_lo)


def _per_chunk(rows, chunk):
    parts = [jnp.broadcast_to(r, (chunk, r.shape[1])) for r in rows]
    return parts[0] if len(parts) == 1 else jnp.concatenate(parts, axis=0)


def _sum_rows(rows, width):
    out = jnp.zeros((1, width), F32)
    for r in rows:
        out = out + r
    return out


def _block_decay_column(b, chunk):
    tb, w = b.shape
    total = _sum_rows([b[(c + 1) * chunk - 1:(c + 1) * chunk, :] for c in range(tb // chunk)], w)
    return jnp.transpose(jnp.broadcast_to(jnp.exp(total), (V7X_LANES, w)))[:, 0:1]


def _block_recurrence(q, k, b, v_bf, s0, dist, dec_col, chunk):
    w = q.shape[1]
    n = q.shape[0] // chunk
    mids = [b[c * chunk + chunk // 2 - 1:c * chunk + chunk // 2, :] for c in range(n)]
    lasts = [b[(c + 1) * chunk - 1:(c + 1) * chunk, :] for c in range(n)]
    b_mid = _per_chunk(mids, chunk)
    e_fwd = jnp.exp(b - b_mid)
    e_bwd = jnp.exp(b_mid - b)
    q_in = q * (e_fwd * _per_chunk([jnp.exp(m) for m in mids], chunk))
    k_out = k * (e_bwd * _per_chunk([jnp.exp(l - m) for l, m in zip(lasts, mids)], chunk))
    attn = jnp.where(dist == 0, _dot_nt((q * e_fwd).astype(BF16), (k * e_bwd).astype(BF16)), 0.0)
    q_in_bf = q_in.astype(BF16)
    for d in range(1, n):
        gaps = [jnp.exp(_sum_rows(lasts[j + 1:j + d], w)) if j + d < n else jnp.zeros((1, w), F32)
                for j in range(n)]
        k_d = k_out if d == 1 else k_out * _per_chunk(gaps, chunk)
        attn = jnp.where(dist == d, _dot_nt(q_in_bf, k_d.astype(BF16)), attn)
    head = _per_chunk([jnp.exp(_sum_rows(lasts[:c], w)) for c in range(n)], chunk)
    tail = _per_chunk([jnp.exp(_sum_rows(lasts[c + 1:], w)) for c in range(n)], chunk)
    o = _dot((q_in * head).astype(BF16), s0.astype(BF16)) + _dot(attn.astype(BF16), v_bf)
    s_end = dec_col * s0 + _dot_tn((k_out * tail).astype(BF16), v_bf)
    return o, s_end


def _log_sigmoid(x):
    return jnp.minimum(x, 0.0) - jnp.log(1.0 + jnp.exp(-jnp.abs(x)))


def _gla_kernel(*refs, chunk, zero_init, heads):
    if zero_init:
        (q_ref, k_ref, v_ref, z_ref, ga_ref, wa2_ref, ba_ref, nw_ref, tri_ref, dist_ref,
         h_ref, sout_ref, s_scr) = refs
    else:
        (q_ref, k_ref, v_ref, z_ref, ga_ref, wa2_ref, ba_ref, nw_ref, tri_ref, dist_ref, s0_ref,
         h_ref, sout_ref, s_scr) = refs
    t = pl.program_id(2)

    @pl.when(t == 0)
    def _():
        if zero_init:
            s_scr[...] = jnp.zeros_like(s_scr)
        else:
            s_scr[...] = s0_ref[0]

    gate_in = _dot(ga_ref[...].astype(BF16), wa2_ref[...]) + ba_ref[...]
    g = _log_sigmoid(gate_in) * (1.0 / GLA_GATE_NORMALIZER)
    b = _chunk_cumsum(g, tri_ref[...])
    dec_col = _block_decay_column(b, chunk)
    dist = dist_ref[...]
    nw = nw_ref[...]
    for hi in range(heads):
        kc = slice(hi * GLA_DK, (hi + 1) * GLA_DK)
        vc = slice(hi * GLA_DV, (hi + 1) * GLA_DV)
        o, s = _block_recurrence(q_ref[:, kc] * (GLA_DK ** -0.5), k_ref[:, kc], b[:, kc],
                                 v_ref[:, vc].astype(BF16), s_scr[hi], dist, dec_col[kc, :], chunk)
        s_scr[hi] = s
        h_ref[:, vc] = (_rms(o, nw) * _silu(z_ref[:, vc])).astype(BF16)

    @pl.when(t == pl.num_programs(2) - 1)
    def _():
        sout_ref[0] = s_scr[...]


def _gla_mix(proj, gate_lr, wa2_bf, b_a, norm_w, s0, batch, seq):
    chunk = min(CHUNK, seq)
    tb = min(REC_TOKENS, seq)
    nt = seq // tb
    hb = GLA_HEADS_PER_STEP
    kb = hb * GLA_DK
    vb = hb * GLA_DV
    k_off = GLA_KD // kb
    v_off = (2 * GLA_KD) // vb
    z_off = (2 * GLA_KD + D_INNER) // vb
    zero_init = s0 is None
    in_specs = [
        pl.BlockSpec((tb, kb), lambda b, h, t: (b * nt + t, h)),
        pl.BlockSpec((tb, kb), lambda b, h, t: (b * nt + t, k_off + h)),
        pl.BlockSpec((tb, vb), lambda b, h, t: (b * nt + t, v_off + h)),
        pl.BlockSpec((tb, vb), lambda b, h, t: (b * nt + t, z_off + h)),
        pl.BlockSpec((tb, V7X_LANES), lambda b, h, t: (b * nt + t, 0)),
        pl.BlockSpec((V7X_LANES, kb), lambda b, h, t: (0, h)),
        pl.BlockSpec((1, kb), lambda b, h, t: (0, h)),
        pl.BlockSpec((1, GLA_DV), lambda b, h, t: (0, 0)),
        pl.BlockSpec((tb, tb), lambda b, h, t: (0, 0)),
        pl.BlockSpec((tb, tb), lambda b, h, t: (0, 0)),
    ]
    args = [proj, proj, proj, proj, gate_lr, wa2_bf, b_a.reshape(1, GLA_KD), norm_w.reshape(1, GLA_DV),
            *_block_constants(tb, chunk)]
    if not zero_init:
        in_specs.append(pl.BlockSpec((1, hb, GLA_DK, GLA_DV), lambda b, h, t: (b, h, 0, 0)))
        args.append(s0)
    h, s_out = pl.pallas_call(
        functools.partial(_gla_kernel, chunk=chunk, zero_init=zero_init, heads=hb),
        grid=(batch, GLA_HEADS // hb, nt),
        in_specs=in_specs,
        out_specs=[
            pl.BlockSpec((tb, vb), lambda b, h, t: (b * nt + t, h)),
            pl.BlockSpec((1, hb, GLA_DK, GLA_DV), lambda b, h, t: (b, h, 0, 0)),
        ],
        out_shape=[
            jax.ShapeDtypeStruct((batch * seq, D_INNER), BF16),
            jax.ShapeDtypeStruct((batch, GLA_HEADS, GLA_DK, GLA_DV), F32),
        ],
        scratch_shapes=[pltpu.VMEM((hb, GLA_DK, GLA_DV), F32)],
        compiler_params=_params(("parallel", "parallel", "arbitrary")),
        name="gla_mix",
    )(*args)
    return h, s_out


def _hgrn_kernel(*refs, chunk, zero_init, layer, heads):
    if zero_init:
        (q_ref, f_ref, i_ref, z_ref, lbp_ref, nw_ref, tri_ref, dist_ref,
         u_ref, ssq_ref, sout_ref, s_scr) = refs
    else:
        (q_ref, f_ref, i_ref, z_ref, lbp_ref, nw_ref, tri_ref, dist_ref, s0_ref,
         u_ref, ssq_ref, sout_ref, s_scr) = refs
    t = pl.program_id(2)

    @pl.when(t == 0)
    def _():
        if zero_init:
            s_scr[...] = jnp.zeros_like(s_scr)
        else:
            s_scr[...] = s0_ref[0]

    lbp = lbp_ref[...]
    lbe = jnp.exp(lbp - jnp.max(lbp, axis=0, keepdims=True))
    lbs = lbe / jnp.sum(lbe, axis=0, keepdims=True)
    lb = jnp.zeros_like(lbs[0:1])
    for r in range(1, layer + 1):
        lb = lb + lbs[r:r + 1]

    q = _silu(q_ref[...]) * (HGRN_DK ** -0.5)
    fgate = lb + (1.0 - lb) * _sigmoid(f_ref[...])
    k = 1.0 - fgate
    b = _chunk_cumsum(jnp.log(fgate), tri_ref[...])
    dec_col = _block_decay_column(b, chunk)
    v_bf = i_ref[...].astype(BF16)
    dist = dist_ref[...]
    ssq = jnp.zeros((q.shape[0], 1), F32)
    for hi in range(heads):
        cols = slice(hi * HGRN_DK, (hi + 1) * HGRN_DK)
        o, s = _block_recurrence(q[:, cols], k[:, cols], b[:, cols], v_bf[:, cols], s_scr[hi],
                                 dist, dec_col[cols, :], chunk)
        s_scr[hi] = s
        ssq = ssq + jnp.sum(o * o, axis=-1, keepdims=True)
        u_ref[:, cols] = (o * nw_ref[:, cols] * _silu(z_ref[:, cols])).astype(BF16)
    ssq_ref[0] = ssq

    @pl.when(t == pl.num_programs(2) - 1)
    def _():
        sout_ref[0] = s_scr[...]


def _hgrn_mix(proj, lower_bounds, layer, norm_w, s0, batch, seq):
    chunk = min(CHUNK, seq)
    tb = min(REC_TOKENS, seq)
    nt = seq // tb
    hb = HGRN_HEADS_PER_STEP
    wb = hb * HGRN_DK
    ngroups = HGRN_HEADS // hb
    col_groups = D_INNER // wb
    zero_init = s0 is None
    in_specs = [
        pl.BlockSpec((tb, wb), lambda b, h, t: (b * nt + t, h)),
        pl.BlockSpec((tb, wb), lambda b, h, t: (b * nt + t, col_groups + h)),
        pl.BlockSpec((tb, wb), lambda b, h, t: (b * nt + t, 2 * col_groups + h)),
        pl.BlockSpec((tb, wb), lambda b, h, t: (b * nt + t, 3 * col_groups + h)),
        pl.BlockSpec((DEPTH, wb), lambda b, h, t: (0, h)),
        pl.BlockSpec((1, wb), lambda b, h, t: (0, h)),
        pl.BlockSpec((tb, tb), lambda b, h, t: (0, 0)),
        pl.BlockSpec((tb, tb), lambda b, h, t: (0, 0)),
    ]
    args = [proj, proj, proj, proj, lower_bounds, norm_w.reshape(1, D_INNER), *_block_constants(tb, chunk)]
    if not zero_init:
        in_specs.append(pl.BlockSpec((1, hb, HGRN_DK, HGRN_DV), lambda b, h, t: (b, h, 0, 0)))
        args.append(s0)
    u, ssq, s_out = pl.pallas_call(
        functools.partial(_hgrn_kernel, chunk=chunk, zero_init=zero_init, layer=layer, heads=hb),
        grid=(batch, ngroups, nt),
        in_specs=in_specs,
        out_specs=[
            pl.BlockSpec((tb, wb), lambda b, h, t: (b * nt + t, h)),
            pl.BlockSpec((1, tb, 1), lambda b, h, t: (h, b * nt + t, 0)),
            pl.BlockSpec((1, hb, HGRN_DK, HGRN_DV), lambda b, h, t: (b, h, 0, 0)),
        ],
        out_shape=[
            jax.ShapeDtypeStruct((batch * seq, D_INNER), BF16),
            jax.ShapeDtypeStruct((ngroups, batch * seq, 1), F32),
            jax.ShapeDtypeStruct((batch, HGRN_HEADS, HGRN_DK, HGRN_DV), F32),
        ],
        scratch_shapes=[pltpu.VMEM((hb, HGRN_DK, HGRN_DV), F32)],
        compiler_params=_params(("parallel", "parallel", "arbitrary")),
        name="hgrn_mix",
    )(*args)
    return u, ssq, s_out


def _diff_lambda(lam_ref, lam_init):
    lp = lam_ref[...]
    a = jnp.sum(lp[0:1] * lp[1:2], axis=-1, keepdims=True)
    b = jnp.sum(lp[2:3] * lp[3:4], axis=-1, keepdims=True)
    return jnp.exp(a) - jnp.exp(b) + lam_init


def _diff_finish(o1, o2, lam_ref, sw_ref, z_ref, lam_init):
    o = o1 - _diff_lambda(lam_ref, lam_init) * o2
    o = _rms(o, sw_ref[...]) * (1.0 - lam_init)
    return (o * _silu(z_ref[...])).astype(BF16)


def _diff_prompt_kernel(qi_tab, ki_tab, mask_tab, last_tab, q_ref, k_ref, v_ref, z_ref, lam_ref, sw_ref,
                        h_ref, s_scr, p_scr, m_scr, l_scr, a_scr, acc_scr,
                        *, bq, bk, rows_per_tile, lam_init):
    p = pl.program_id(2)
    qi = qi_tab[p]
    ki = ki_tab[p]

    @pl.when(ki == 0)
    def _():
        m_scr[...] = jnp.full_like(m_scr, -jnp.inf)
        l_scr[...] = jnp.zeros_like(l_scr)
        acc_scr[...] = jnp.zeros_like(acc_scr)

    def step(masked):
        for h in range(2):
            cols = slice(h * DIFF_HD, (h + 1) * DIFF_HD)
            s_scr[h] = _dot_nt(q_ref[0, :, cols], k_ref[0, :, cols])
        for h in range(2):
            for r in range(bq // rows_per_tile):
                rows = pl.ds(r * rows_per_tile, rows_per_tile)
                s = s_scr[h, rows, :]
                if masked:
                    shape = (rows_per_tile, bk)
                    q_chunk = (qi * bq + r * rows_per_tile + lax.broadcasted_iota(jnp.int32, shape, 0)) // CHUNK
                    k_chunk = (ki * bk + lax.broadcasted_iota(jnp.int32, shape, 1)) // CHUNK
                    s = jnp.where(k_chunk <= q_chunk, s, -jnp.inf)
                m_prev = m_scr[h, rows, :]
                m_new = jnp.maximum(m_prev, jnp.max(s, axis=-1, keepdims=True))
                alpha = jnp.exp2(m_prev - m_new)
                pr = jnp.exp2(s - jnp.tile(m_new, (1, bk // V7X_LANES)))
                l_scr[h, rows, :] = alpha * l_scr[h, rows, :] + jnp.sum(pr, axis=-1, keepdims=True)
                m_scr[h, rows, :] = m_new
                a_scr[h, rows, :] = alpha
                p_scr[h, rows, :] = pr.astype(BF16)
        for h in range(2):
            alpha = jnp.tile(a_scr[h], (1, 2 * DIFF_HD // V7X_LANES))
            acc_scr[h] = alpha * acc_scr[h] + _dot(p_scr[h], v_ref[0])

    @pl.when(mask_tab[p] == 1)
    def _():
        step(True)

    @pl.when(mask_tab[p] == 0)
    def _():
        step(False)

    @pl.when(last_tab[p] == 1)
    def _():
        reps = (1, 2 * DIFF_HD // V7X_LANES)
        o1 = acc_scr[0] * jnp.tile(1.0 / l_scr[0], reps)
        o2 = acc_scr[1] * jnp.tile(1.0 / l_scr[1], reps)
        h_ref[...] = _diff_finish(o1, o2, lam_ref, sw_ref, z_ref, lam_init)


def _diff_prompt_tables(seq, bq, bk):
    qi_l, ki_l, mask_l, last_l = [], [], [], []
    for qi in range(seq // bq):
        q_lo, q_hi = qi * bq, (qi + 1) * bq
        nk = -(-q_hi // bk)
        for ki in range(nk):
            k_hi = (ki + 1) * bk
            fully_visible = k_hi <= (q_lo // CHUNK + 1) * CHUNK
            qi_l.append(qi)
            ki_l.append(ki)
            mask_l.append(0 if fully_visible else 1)
            last_l.append(1 if ki == nk - 1 else 0)
    return [jnp.asarray(np.asarray(a, np.int32)) for a in (qi_l, ki_l, mask_l, last_l)]


def _diff_prompt_attn(qkv, z, lam_p, subln_w, lam_init, batch, seq):
    bq = min(ATT_BQ, seq)
    bk = min(ATT_BK, seq)
    nq = seq // bq
    nkb = seq // bk
    pw = 2 * DIFF_HD
    tabs = _diff_prompt_tables(seq, bq, bk)
    n_steps = int(tabs[0].shape[0])
    grid_spec = pltpu.PrefetchScalarGridSpec(
        num_scalar_prefetch=4,
        grid=(batch, DIFF_HEADS, n_steps),
        in_specs=[
            pl.BlockSpec((1, bq, pw), lambda b, h, p, qt, kt, mt, lt: (h, b * nq + qt[p], 0)),
            pl.BlockSpec((1, bk, pw), lambda b, h, p, qt, kt, mt, lt: (DIFF_HEADS + h, b * nkb + kt[p], 0)),
            pl.BlockSpec((1, bk, pw), lambda b, h, p, qt, kt, mt, lt: (2 * DIFF_HEADS + h, b * nkb + kt[p], 0)),
            pl.BlockSpec((bq, pw), lambda b, h, p, qt, kt, mt, lt: (b * nq + qt[p], h)),
            pl.BlockSpec((4, DIFF_HD), lambda b, h, p, qt, kt, mt, lt: (0, 0)),
            pl.BlockSpec((1, pw), lambda b, h, p, qt, kt, mt, lt: (0, 0)),
        ],
        out_specs=pl.BlockSpec((bq, pw), lambda b, h, p, qt, kt, mt, lt: (b * nq + qt[p], h)),
        scratch_shapes=[
            pltpu.VMEM((2, bq, bk), F32),
            pltpu.VMEM((2, bq, bk), BF16),
            pltpu.VMEM((2, bq, V7X_LANES), F32),
            pltpu.VMEM((2, bq, V7X_LANES), F32),
            pltpu.VMEM((2, bq, V7X_LANES), F32),
            pltpu.VMEM((2, bq, pw), F32),
        ],
    )
    return pl.pallas_call(
        functools.partial(_diff_prompt_kernel, bq=bq, bk=bk, rows_per_tile=min(ATT_ROWS, bq),
                          lam_init=lam_init),
        grid_spec=grid_spec,
        out_shape=jax.ShapeDtypeStruct((batch * seq, D_INNER), BF16),
        compiler_params=_params(("parallel", "parallel", "arbitrary")),
        name="diff_prompt_attn",
    )(*tabs, qkv, qkv, qkv, z, lam_p, subln_w.reshape(1, pw))


def _diff_sample_kernel(q_ref, kn_ref, vn_ref, z_ref, kc_ref, vc_ref, lam_ref, sw_ref, h_ref, *, lam_init):
    vc_bf = vc_ref[...].astype(BF16)
    outs = []
    for h in range(2):
        cols = slice(h * DIFF_HD, (h + 1) * DIFF_HD)
        q_bf = q_ref[0, :, cols]
        s_c = _dot_nt(q_bf, kc_ref[:, cols].astype(BF16))
        s_n = _dot_nt(q_bf, kn_ref[0, :, cols])
        m = jnp.maximum(jnp.max(s_c, axis=-1, keepdims=True), jnp.max(s_n, axis=-1, keepdims=True))
        p_c = jnp.exp2(s_c - m)
        p_n = jnp.exp2(s_n - m)
        l = jnp.sum(p_c, axis=-1, keepdims=True) + jnp.sum(p_n, axis=-1, keepdims=True)
        acc = _dot(p_c.astype(BF16), vc_bf) + _dot(p_n.astype(BF16), vn_ref[0])
        outs.append(acc * (1.0 / l))
    h_ref[...] = _diff_finish(outs[0], outs[1], lam_ref, sw_ref, z_ref, lam_init)


def _diff_sample_attn(qkv, z, cache_k2, cache_v2, lam_p, subln_w, lam_init, batch, seq, past):
    pw = 2 * DIFF_HD
    return pl.pallas_call(
        functools.partial(_diff_sample_kernel, lam_init=lam_init),
        grid=(batch, DIFF_HEADS),
        in_specs=[
            pl.BlockSpec((1, seq, pw), lambda b, h: (h, b, 0)),
            pl.BlockSpec((1, seq, pw), lambda b, h: (DIFF_HEADS + h, b, 0)),
            pl.BlockSpec((1, seq, pw), lambda b, h: (2 * DIFF_HEADS + h, b, 0)),
            pl.BlockSpec((seq, pw), lambda b, h: (b, h)),
            pl.BlockSpec((past, pw), lambda b, h: (b, h)),
            pl.BlockSpec((past, pw), lambda b, h: (b, h)),
            pl.BlockSpec((4, DIFF_HD), lambda b, h: (0, 0)),
            pl.BlockSpec((1, pw), lambda b, h: (0, 0)),
        ],
        out_specs=pl.BlockSpec((seq, pw), lambda b, h: (b, h)),
        out_shape=jax.ShapeDtypeStruct((batch * seq, D_INNER), BF16),
        compiler_params=_params(("parallel", "parallel")),
        name="diff_sample_attn",
    )(qkv, qkv, qkv, z, cache_k2, cache_v2, lam_p, subln_w.reshape(1, pw))


def kernel(x_prompt, x_sample, state_gla, state_hgrn, cache_k, cache_v, norm_w, final_norm_w,
           gla_w_in, gla_w_a1, gla_w_a2, gla_b_a, gla_norm_w, gla_w_out,
           hgrn_w_in, hgrn_lower_bounds, hgrn_norm_w, hgrn_w_out,
           diff_w_in, diff_lambda, diff_subln_w, diff_w_out):
    bp, tp, d = x_prompt.shape
    bs, ts, _ = x_sample.shape
    past = cache_k.shape[2]
    streams = [(bp, tp), (bs, ts)]
    xs = [x_prompt.reshape(bp * tp, d), x_sample.reshape(bs * ts, d)]

    def out_proj(h, res, w_out, layer, ssq=None):
        if layer == DEPTH - 1:
            return _final_proj(h, res, w_out, final_norm_w)
        return _res_proj(h, res, w_out, ssq)

    gla_states = [[], []]
    hgrn_states = [[], []]
    k_rows = [[], []]
    v_rows = [[], []]
    ia = ib = ic = 0
    for i in range(DEPTH):
        kind = i % N_MIXERS
        if kind == 0:
            j = ia
            ia += 1
            w_in = gla_w_in[j].astype(BF16)
            w_a1 = jnp.pad(gla_w_a1[j], ((0, 0), (0, V7X_LANES - GLA_GATE_RANK))).astype(BF16)
            w_a2 = jnp.pad(gla_w_a2[j], ((0, V7X_LANES - GLA_GATE_RANK), (0, 0))).astype(BF16)
            w_out = gla_w_out[j].astype(BF16)
            for si, (batch, seq) in enumerate(streams):
                proj, gate_lr = _norm_proj(xs[si], norm_w[i], w_in, w_a1)
                s0 = None if si == 0 else state_gla[j]
                h, s_new = _gla_mix(proj, gate_lr, w_a2, gla_b_a[j], gla_norm_w[j], s0, batch, seq)
                xs[si] = out_proj(h, xs[si], w_out, i)
                gla_states[si].append(s_new)
        elif kind == 1:
            j = ib
            ib += 1
            w_in = hgrn_w_in[j].astype(BF16)
            w_out = hgrn_w_out[j].astype(BF16)
            for si, (batch, seq) in enumerate(streams):
                proj = _norm_proj(xs[si], norm_w[i], w_in)
                s0 = None if si == 0 else state_hgrn[j]
                u, ssq, s_new = _hgrn_mix(proj, hgrn_lower_bounds, i, hgrn_norm_w[j], s0, batch, seq)
                xs[si] = out_proj(u, xs[si], w_out, i, ssq)
                hgrn_states[si].append(s_new)
        else:
            j = ic
            ic += 1
            lam_init = 0.8 - 0.6 * math.exp(-0.3 * i)
            w_in = diff_w_in[j].astype(BF16)
            w_out = diff_w_out[j].astype(BF16)
            for si, (batch, seq) in enumerate(streams):
                qkv, k_new, v_new, z = _diff_proj(xs[si], norm_w[i], w_in)
                if si == 0:
                    h = _diff_prompt_attn(qkv, z, diff_lambda[j], diff_subln_w[j], lam_init, batch, seq)
                else:
                    ck = cache_k[j].reshape(batch * past, 2 * DIFF_HEADS * DIFF_HD)
                    cv = cache_v[j].reshape(batch * past, 2 * DIFF_HEADS * DIFF_HD)
                    h = _diff_sample_attn(qkv, z, ck, cv, diff_lambda[j], diff_subln_w[j], lam_init,
                                          batch, seq, past)
                xs[si] = out_proj(h, xs[si], w_out, i)
                k_rows[si].append(k_new.reshape(batch, seq, 2 * DIFF_HEADS, DIFF_HD))
                v_rows[si].append(v_new.reshape(batch, seq, DIFF_HEADS, 2 * DIFF_HD))

    return (xs[0].reshape(bp, tp, d), xs[1].reshape(bs, ts, d),
            jnp.stack(gla_states[0]), jnp.stack(gla_states[1]),
            jnp.stack(hgrn_states[0]), jnp.stack(hgrn_states[1]),
            jnp.stack(k_rows[0]), jnp.stack(v_rows[0]),
            jnp.stack(k_rows[1]), jnp.stack(v_rows[1]))
```

```python
import functools
import math

import numpy as np
import jax
import jax.numpy as jnp
from jax import lax
from jax.experimental import pallas as pl
from jax.experimental.pallas import tpu as pltpu

D_MODEL = 2048
DEPTH = 4
CHUNK = 64
N_MIXERS = 3
D_INNER = D_MODEL
NORM_EPS = 1e-6
GLA_HEADS = 4
GLA_KD = D_INNER // 2
GLA_DK = GLA_KD // GLA_HEADS
GLA_DV = D_INNER // GLA_HEADS
GLA_GATE_RANK = 16
GLA_GATE_NORMALIZER = 16.0
HGRN_HEADS = 16
HGRN_DK = 128
HGRN_DV = 128
DIFF_HD = 128
DIFF_HEADS = 8

LOG2_E = math.log2(math.e)
DIFF_Q_SCALE = (DIFF_HD ** -0.5) * LOG2_E

F32 = jnp.float32
BF16 = jnp.bfloat16

V7X_LANES = 128
V7X_VMEM_BYTES = 64 * 1024 * 1024
VMEM_LIMIT_BYTES = (V7X_VMEM_BYTES * 3) // 4

PROJ_TM = 1024
PROJ_TN = 1024
DIFF_TN = 512
OUT_TN = 1024
FINAL_TM = 512
REC_TOKENS = 256
GLA_HEADS_PER_STEP = 4
HGRN_HEADS_PER_STEP = 4
ATT_BQ = 512
ATT_BK = 512
ATT_ROWS = 128


def _params(sem):
    return pltpu.CompilerParams(dimension_semantics=sem, vmem_limit_bytes=VMEM_LIMIT_BYTES)


def _tiled_weight(w, tn):
    d, n = w.shape
    return w.astype(BF16).reshape(d, n // tn, tn).transpose(1, 0, 2)


def _dot(a, b):
    return jnp.dot(a, b, preferred_element_type=F32)


def _dot_nt(a, b):
    return lax.dot_general(a, b, (((1,), (1,)), ((), ())), preferred_element_type=F32)


def _dot_tn(a, b):
    return lax.dot_general(a, b, (((0,), (0,)), ((), ())), preferred_element_type=F32)


def _sigmoid(x):
    return 1.0 / (1.0 + jnp.exp(-x))


def _silu(x):
    return x * _sigmoid(x)


def _rms(x, w):
    ms = jnp.mean(x * x, axis=-1, keepdims=True)
    return x * lax.rsqrt(ms + NORM_EPS) * w


def _norm_proj_kernel(*refs, has_aux):
    if has_aux:
        x_ref, nw_ref, w_ref, aw_ref, o_ref, ao_ref, h_scr = refs
    else:
        x_ref, nw_ref, w_ref, o_ref, h_scr = refs

    @pl.when(pl.program_id(1) == 0)
    def _():
        h_scr[...] = _rms(x_ref[...], nw_ref[...]).astype(BF16)
        if has_aux:
            ao_ref[...] = _dot(h_scr[...], aw_ref[...])

    o_ref[...] = _dot(h_scr[...], w_ref[0])


def _norm_proj(x, nw, w_tiles, aux_w_bf=None):
    m, d = x.shape
    tn = w_tiles.shape[2]
    n = w_tiles.shape[0] * tn
    tm = min(PROJ_TM, m)
    in_specs = [
        pl.BlockSpec((tm, d), lambda i, j: (i, 0)),
        pl.BlockSpec((1, d), lambda i, j: (0, 0)),
        pl.BlockSpec((1, d, tn), lambda i, j: (j, 0, 0)),
    ]
    out_shape = [jax.ShapeDtypeStruct((m, n), F32)]
    out_specs = [pl.BlockSpec((tm, tn), lambda i, j: (i, j))]
    args = [x, nw.reshape(1, d), w_tiles]
    if aux_w_bf is not None:
        na = aux_w_bf.shape[1]
        in_specs.append(pl.BlockSpec((d, na), lambda i, j: (0, 0)))
        out_shape.append(jax.ShapeDtypeStruct((m, na), F32))
        out_specs.append(pl.BlockSpec((tm, na), lambda i, j: (i, 0)))
        args.append(aux_w_bf)
    outs = pl.pallas_call(
        functools.partial(_norm_proj_kernel, has_aux=aux_w_bf is not None),
        grid=(m // tm, n // tn),
        in_specs=in_specs,
        out_specs=out_specs,
        out_shape=out_shape,
        scratch_shapes=[pltpu.VMEM((tm, d), BF16)],
        compiler_params=_params(("parallel", "arbitrary")),
        name="norm_proj",
    )(*args)
    return outs if aux_w_bf is not None else outs[0]


def _diff_proj_kernel(x_ref, nw_ref, w_ref, pm_ref, k_ref, v_ref, z_ref, h_scr, *, group_tiles):
    j = pl.program_id(1)

    @pl.when(j == 0)
    def _():
        h_scr[...] = _rms(x_ref[...], nw_ref[...]).astype(BF16)

    acc = _dot(h_scr[...], w_ref[0])
    pw = pm_ref.shape[2]

    def write_pairs(val):
        for p in range(pm_ref.shape[0]):
            pm_ref[p] = val[:, p * pw:(p + 1) * pw].astype(BF16)

    @pl.when(j < group_tiles)
    def _():
        write_pairs(acc * DIFF_Q_SCALE)

    @pl.when((j >= group_tiles) & (j < 2 * group_tiles))
    def _():
        write_pairs(acc)
        k_ref[...] = acc

    @pl.when((j >= 2 * group_tiles) & (j < 3 * group_tiles))
    def _():
        write_pairs(acc)
        v_ref[...] = acc

    @pl.when(j >= 3 * group_tiles)
    def _():
        z_ref[...] = acc


def _diff_proj(x, nw, w_tiles):
    m, d = x.shape
    tm = min(PROJ_TM, m)
    tn = w_tiles.shape[2]
    pw = 2 * DIFF_HD
    g = D_INNER // tn
    ppt = tn // pw

    def rows_spec(group):
        return pl.BlockSpec((tm, tn), lambda i, j: (i, jnp.clip(j - group * g, 0, g - 1)))

    return pl.pallas_call(
        functools.partial(_diff_proj_kernel, group_tiles=g),
        grid=(m // tm, 4 * g),
        in_specs=[
            pl.BlockSpec((tm, d), lambda i, j: (i, 0)),
            pl.BlockSpec((1, d), lambda i, j: (0, 0)),
            pl.BlockSpec((1, d, tn), lambda i, j: (j, 0, 0)),
        ],
        out_specs=[
            pl.BlockSpec((ppt, tm, pw), lambda i, j: (jnp.minimum(j, 3 * g - 1), i, 0)),
            rows_spec(1), rows_spec(2), rows_spec(3),
        ],
        out_shape=[
            jax.ShapeDtypeStruct((3 * DIFF_HEADS, m, pw), BF16),
            jax.ShapeDtypeStruct((m, D_INNER), F32),
            jax.ShapeDtypeStruct((m, D_INNER), F32),
            jax.ShapeDtypeStruct((m, D_INNER), F32),
        ],
        scratch_shapes=[pltpu.VMEM((tm, d), BF16)],
        compiler_params=_params(("parallel", "arbitrary")),
        name="diff_proj",
    )(x, nw.reshape(1, d), w_tiles)


def _row_rms_scale(ssq_ref):
    ssq = ssq_ref[0]
    for gi in range(1, ssq_ref.shape[0]):
        ssq = ssq + ssq_ref[gi]
    return lax.rsqrt(ssq * (1.0 / D_INNER) + NORM_EPS)


def _res_proj_kernel(*refs, has_ssq):
    if has_ssq:
        h_ref, ssq_ref, res_ref, w_ref, y_ref = refs
    else:
        h_ref, res_ref, w_ref, y_ref = refs
    acc = _dot(h_ref[...], w_ref[0])
    if has_ssq:
        acc = acc * _row_rms_scale(ssq_ref)
    y_ref[...] = res_ref[...] + acc


def _res_proj(h_bf, res, w_tiles, ssq=None):
    m, d = h_bf.shape
    tn = w_tiles.shape[2]
    n = w_tiles.shape[0] * tn
    tm = min(PROJ_TM, m)
    in_specs = [pl.BlockSpec((tm, d), lambda i, j: (i, 0))]
    args = [h_bf]
    if ssq is not None:
        in_specs.append(pl.BlockSpec((ssq.shape[0], tm, 1), lambda i, j: (0, i, 0)))
        args.append(ssq)
    in_specs += [
        pl.BlockSpec((tm, tn), lambda i, j: (i, j)),
        pl.BlockSpec((1, d, tn), lambda i, j: (j, 0, 0)),
    ]
    args += [res, w_tiles]
    return pl.pallas_call(
        functools.partial(_res_proj_kernel, has_ssq=ssq is not None),
        grid=(m // tm, n // tn),
        in_specs=in_specs,
        out_specs=pl.BlockSpec((tm, tn), lambda i, j: (i, j)),
        out_shape=jax.ShapeDtypeStruct((m, n), F32),
        compiler_params=_params(("parallel", "arbitrary")),
        name="res_proj",
    )(*args)


def _final_proj_kernel(h_ref, res_ref, w_ref, fw_ref, y_ref):
    y_ref[...] = _rms(res_ref[...] + _dot(h_ref[...], w_ref[...]), fw_ref[...])


def _final_proj(h_bf, res, w_bf, final_w):
    m, d = h_bf.shape
    n = w_bf.shape[1]
    tm = min(FINAL_TM, m)
    return pl.pallas_call(
        _final_proj_kernel,
        grid=(m // tm,),
        in_specs=[
            pl.BlockSpec((tm, d), lambda i: (i, 0)),
            pl.BlockSpec((tm, n), lambda i: (i, 0)),
            pl.BlockSpec((d, n), lambda i: (0, 0)),
            pl.BlockSpec((1, n), lambda i: (0, 0)),
        ],
        out_specs=pl.BlockSpec((tm, n), lambda i: (i, 0)),
        out_shape=jax.ShapeDtypeStruct((m, n), F32),
        compiler_params=_params(("parallel",)),
        name="final_proj",
    )(h_bf, res, w_bf, final_w.reshape(1, n))


def _block_constants(tb, chunk):
    row = np.arange(tb)[:, None]
    col = np.arange(tb)[None, :]
    rc, cc = row // chunk, col // chunk
    same = (rc == cc) & (row >= col)
    dist = np.where(same, 0, np.where(rc > cc, rc - cc, -1)).astype(np.int32)
    return jnp.asarray(same.astype(np.float32), BF16), jnp.asarray(dist)


def _chunk_cumsum(g, tri_bf):
    g_hi = g.astype(BF16)
    g_lo = (g - g_hi.astype(F32)).astype(BF16)
    return _dot(tri_bf, g_hi) + _dot(tri_bf, g_lo)


def _per_chunk(rows, chunk):
    parts = [jnp.broadcast_to(r, (chunk, r.shape[1])) for r in rows]
    return parts[0] if len(parts) == 1 else jnp.concatenate(parts, axis=0)


def _sum_rows(rows, width):
    out = jnp.zeros((1, width), F32)
    for r in rows:
        out = out + r
    return out


def _block_decay_column(b, chunk):
    tb, w = b.shape
    total = _sum_rows([b[(c + 1) * chunk - 1:(c + 1) * chunk, :] for c in range(tb // chunk)], w)
    return jnp.transpose(jnp.broadcast_to(jnp.exp2(total), (V7X_LANES, w)))[:, 0:1]


def _block_recurrence(q, k, b, v_bf, s0, dist, dec_col, chunk, q_log2_scale):
    w = q.shape[1]
    n = q.shape[0] // chunk
    mids = [b[c * chunk + chunk // 2 - 1:c * chunk + chunk // 2, :] for c in range(n)]
    lasts = [b[(c + 1) * chunk - 1:(c + 1) * chunk, :] for c in range(n)]
    q_mid = q * jnp.exp2(b - _per_chunk([m - q_log2_scale for m in mids], chunk))
    k_mid = k * jnp.exp2(_per_chunk(mids, chunk) - b)
    q_in = q_mid * _per_chunk([jnp.exp2(m) for m in mids], chunk)
    k_out = k_mid * _per_chunk([jnp.exp2(l - m) for l, m in zip(lasts, mids)], chunk)
    attn = jnp.where(dist == 0, _dot_nt(q_mid.astype(BF16), k_mid.astype(BF16)), 0.0)
    q_in_bf = q_in.astype(BF16)
    for d in range(1, n):
        gaps = [jnp.exp2(_sum_rows(lasts[j + 1:j + d], w)) if j + d < n else jnp.zeros((1, w), F32)
                for j in range(n)]
        k_d = k_out if d == 1 else k_out * _per_chunk(gaps, chunk)
        attn = jnp.where(dist == d, _dot_nt(q_in_bf, k_d.astype(BF16)), attn)
    head = _per_chunk([jnp.exp2(_sum_rows(lasts[:c], w)) for c in range(n)], chunk)
    tail = _per_chunk([jnp.exp2(_sum_rows(lasts[c + 1:], w)) for c in range(n)], chunk)
    o = _dot((q_in * head).astype(BF16), s0.astype(BF16)) + _dot(attn.astype(BF16), v_bf)
    s_end = dec_col * s0 + _dot_tn((k_out * tail).astype(BF16), v_bf)
    return o, s_end


def _log_sigmoid(x):
    return jnp.minimum(x, 0.0) - jnp.log(1.0 + jnp.exp(-jnp.abs(x)))


def _gla_kernel(*refs, chunk, zero_init, heads):
    if zero_init:
        (q_ref, k_ref, v_ref, z_ref, ga_ref, wa2_ref, ba_ref, nw_ref, tri_ref, dist_ref,
         h_ref, sout_ref, s_scr) = refs
    else:
        (q_ref, k_ref, v_ref, z_ref, ga_ref, wa2_ref, ba_ref, nw_ref, tri_ref, dist_ref, s0_ref,
         h_ref, sout_ref, s_scr) = refs
    t = pl.program_id(2)

    @pl.when(t == 0)
    def _():
        if zero_init:
            s_scr[...] = jnp.zeros_like(s_scr)
        else:
            s_scr[...] = s0_ref[0]

    gate_in = _dot(ga_ref[...].astype(BF16), wa2_ref[...]) + ba_ref[...]
    g = _log_sigmoid(gate_in) * (LOG2_E / GLA_GATE_NORMALIZER)
    b = _chunk_cumsum(g, tri_ref[...])
    dec_col = _block_decay_column(b, chunk)
    dist = dist_ref[...]
    nw = nw_ref[...]
    for hi in range(heads):
        kc = slice(hi * GLA_DK, (hi + 1) * GLA_DK)
        vc = slice(hi * GLA_DV, (hi + 1) * GLA_DV)
        o, s = _block_recurrence(q_ref[:, kc], k_ref[:, kc], b[:, kc], v_ref[:, vc].astype(BF16),
                                 s_scr[hi], dist, dec_col[kc, :], chunk, math.log2(GLA_DK ** -0.5))
        s_scr[hi] = s
        h_ref[:, vc] = (_rms(o, nw) * _silu(z_ref[:, vc])).astype(BF16)

    @pl.when(t == pl.num_programs(2) - 1)
    def _():
        sout_ref[0] = s_scr[...]


def _gla_mix(proj, gate_lr, wa2_bf, b_a, norm_w, s0, batch, seq):
    chunk = min(CHUNK, seq)
    tb = min(REC_TOKENS, seq)
    nt = seq // tb
    hb = GLA_HEADS_PER_STEP
    kb = hb * GLA_DK
    vb = hb * GLA_DV
    k_off = GLA_KD // kb
    v_off = (2 * GLA_KD) // vb
    z_off = (2 * GLA_KD + D_INNER) // vb
    zero_init = s0 is None
    in_specs = [
        pl.BlockSpec((tb, kb), lambda b, h, t: (b * nt + t, h)),
        pl.BlockSpec((tb, kb), lambda b, h, t: (b * nt + t, k_off + h)),
        pl.BlockSpec((tb, vb), lambda b, h, t: (b * nt + t, v_off + h)),
        pl.BlockSpec((tb, vb), lambda b, h, t: (b * nt + t, z_off + h)),
        pl.BlockSpec((tb, V7X_LANES), lambda b, h, t: (b * nt + t, 0)),
        pl.BlockSpec((V7X_LANES, kb), lambda b, h, t: (0, h)),
        pl.BlockSpec((1, kb), lambda b, h, t: (0, h)),
        pl.BlockSpec((1, GLA_DV), lambda b, h, t: (0, 0)),
        pl.BlockSpec((tb, tb), lambda b, h, t: (0, 0)),
        pl.BlockSpec((tb, tb), lambda b, h, t: (0, 0)),
    ]
    args = [proj, proj, proj, proj, gate_lr, wa2_bf, b_a.reshape(1, GLA_KD), norm_w.reshape(1, GLA_DV),
            *_block_constants(tb, chunk)]
    if not zero_init:
        in_specs.append(pl.BlockSpec((1, hb, GLA_DK, GLA_DV), lambda b, h, t: (b, h, 0, 0)))
        args.append(s0)
    h, s_out = pl.pallas_call(
        functools.partial(_gla_kernel, chunk=chunk, zero_init=zero_init, heads=hb),
        grid=(batch, GLA_HEADS // hb, nt),
        in_specs=in_specs,
        out_specs=[
            pl.BlockSpec((tb, vb), lambda b, h, t: (b * nt + t, h)),
            pl.BlockSpec((1, hb, GLA_DK, GLA_DV), lambda b, h, t: (b, h, 0, 0)),
        ],
        out_shape=[
            jax.ShapeDtypeStruct((batch * seq, D_INNER), BF16),
            jax.ShapeDtypeStruct((batch, GLA_HEADS, GLA_DK, GLA_DV), F32),
        ],
        scratch_shapes=[pltpu.VMEM((hb, GLA_DK, GLA_DV), F32)],
        compiler_params=_params(("parallel", "parallel", "arbitrary")),
        name="gla_mix",
    )(*args)
    return h, s_out


def _hgrn_kernel(*refs, chunk, zero_init, layer, heads):
    if zero_init:
        (q_ref, f_ref, i_ref, z_ref, lbp_ref, nw_ref, tri_ref, dist_ref,
         u_ref, ssq_ref, sout_ref, s_scr) = refs
    else:
        (q_ref, f_ref, i_ref, z_ref, lbp_ref, nw_ref, tri_ref, dist_ref, s0_ref,
         u_ref, ssq_ref, sout_ref, s_scr) = refs
    t = pl.program_id(2)

    @pl.when(t == 0)
    def _():
        if zero_init:
            s_scr[...] = jnp.zeros_like(s_scr)
        else:
            s_scr[...] = s0_ref[0]

    lbp = lbp_ref[...]
    lbe = jnp.exp(lbp - jnp.max(lbp, axis=0, keepdims=True))
    lbs = lbe / jnp.sum(lbe, axis=0, keepdims=True)
    lb = jnp.zeros_like(lbs[0:1])
    for r in range(1, layer + 1):
        lb = lb + lbs[r:r + 1]

    q = _silu(q_ref[...])
    fgate = lb + (1.0 - lb) * _sigmoid(f_ref[...])
    k = 1.0 - fgate
    b = _chunk_cumsum(jnp.log(fgate) * LOG2_E, tri_ref[...])
    dec_col = _block_decay_column(b, chunk)
    v_bf = i_ref[...].astype(BF16)
    dist = dist_ref[...]
    ssq = jnp.zeros((q.shape[0], 1), F32)
    for hi in range(heads):
        cols = slice(hi * HGRN_DK, (hi + 1) * HGRN_DK)
        o, s = _block_recurrence(q[:, cols], k[:, cols], b[:, cols], v_bf[:, cols], s_scr[hi],
                                 dist, dec_col[cols, :], chunk, math.log2(HGRN_DK ** -0.5))
        s_scr[hi] = s
        ssq = ssq + jnp.sum(o * o, axis=-1, keepdims=True)
        u_ref[:, cols] = (o * nw_ref[:, cols] * _silu(z_ref[:, cols])).astype(BF16)
    ssq_ref[0] = ssq

    @pl.when(t == pl.num_programs(2) - 1)
    def _():
        sout_ref[0] = s_scr[...]


def _hgrn_mix(proj, lower_bounds, layer, norm_w, s0, batch, seq):
    chunk = min(CHUNK, seq)
    tb = min(REC_TOKENS, seq)
    nt = seq // tb
    hb = HGRN_HEADS_PER_STEP
    wb = hb * HGRN_DK
    ngroups = HGRN_HEADS // hb
    col_groups = D_INNER // wb
    zero_init = s0 is None
    in_specs = [
        pl.BlockSpec((tb, wb), lambda b, h, t: (b * nt + t, h)),
        pl.BlockSpec((tb, wb), lambda b, h, t: (b * nt + t, col_groups + h)),
        pl.BlockSpec((tb, wb), lambda b, h, t: (b * nt + t, 2 * col_groups + h)),
        pl.BlockSpec((tb, wb), lambda b, h, t: (b * nt + t, 3 * col_groups + h)),
        pl.BlockSpec((DEPTH, wb), lambda b, h, t: (0, h)),
        pl.BlockSpec((1, wb), lambda b, h, t: (0, h)),
        pl.BlockSpec((tb, tb), lambda b, h, t: (0, 0)),
        pl.BlockSpec((tb, tb), lambda b, h, t: (0, 0)),
    ]
    args = [proj, proj, proj, proj, lower_bounds, norm_w.reshape(1, D_INNER), *_block_constants(tb, chunk)]
    if not zero_init:
        in_specs.append(pl.BlockSpec((1, hb, HGRN_DK, HGRN_DV), lambda b, h, t: (b, h, 0, 0)))
        args.append(s0)
    u, ssq, s_out = pl.pallas_call(
        functools.partial(_hgrn_kernel, chunk=chunk, zero_init=zero_init, layer=layer, heads=hb),
        grid=(batch, ngroups, nt),
        in_specs=in_specs,
        out_specs=[
            pl.BlockSpec((tb, wb), lambda b, h, t: (b * nt + t, h)),
            pl.BlockSpec((1, tb, 1), lambda b, h, t: (h, b * nt + t, 0)),
            pl.BlockSpec((1, hb, HGRN_DK, HGRN_DV), lambda b, h, t: (b, h, 0, 0)),
        ],
        out_shape=[
            jax.ShapeDtypeStruct((batch * seq, D_INNER), BF16),
            jax.ShapeDtypeStruct((ngroups, batch * seq, 1), F32),
            jax.ShapeDtypeStruct((batch, HGRN_HEADS, HGRN_DK, HGRN_DV), F32),
        ],
        scratch_shapes=[pltpu.VMEM((hb, HGRN_DK, HGRN_DV), F32)],
        compiler_params=_params(("parallel", "parallel", "arbitrary")),
        name="hgrn_mix",
    )(*args)
    return u, ssq, s_out


def _diff_lambda(lam_ref, lam_init):
    lp = lam_ref[...]
    a = jnp.sum(lp[0:1] * lp[1:2], axis=-1, keepdims=True)
    b = jnp.sum(lp[2:3] * lp[3:4], axis=-1, keepdims=True)
    return jnp.exp(a) - jnp.exp(b) + lam_init


def _diff_finish(o1, o2, lam_ref, sw_ref, z, lam_init):
    o = o1 - _diff_lambda(lam_ref, lam_init) * o2
    o = _rms(o, sw_ref[...]) * (1.0 - lam_init)
    return (o * _silu(z)).astype(BF16)


def _diff_prompt_kernel(qi_tab, ki_tab, mask_tab, last_tab, q_ref, k_ref, v_ref, z_ref, lam_ref, sw_ref,
                        h_ref, *scratch, bq, bk, rows_per_tile, lam_init):
    heads = [scratch[6 * h:6 * (h + 1)] for h in range(2)]
    p = pl.program_id(2)
    qi = qi_tab[p]
    ki = ki_tab[p]

    @pl.when(ki == 0)
    def _():
        for s_scr, p_scr, m_scr, l_scr, a_scr, acc_scr in heads:
            m_scr[...] = jnp.full_like(m_scr, -jnp.inf)
            l_scr[...] = jnp.zeros_like(l_scr)
            acc_scr[...] = jnp.zeros_like(acc_scr)

    def step(masked):
        for h, (s_scr, p_scr, m_scr, l_scr, a_scr, acc_scr) in enumerate(heads):
            cols = slice(h * DIFF_HD, (h + 1) * DIFF_HD)
            s_scr[...] = _dot_nt(q_ref[0, :, cols], k_ref[0, :, cols])
        for s_scr, p_scr, m_scr, l_scr, a_scr, acc_scr in heads:
            for r in range(bq // rows_per_tile):
                rows = pl.ds(r * rows_per_tile, rows_per_tile)
                s = s_scr[rows, :]
                if masked:
                    shape = (rows_per_tile, bk)
                    q_chunk = (qi * bq + r * rows_per_tile + lax.broadcasted_iota(jnp.int32, shape, 0)) // CHUNK
                    k_chunk = (ki * bk + lax.broadcasted_iota(jnp.int32, shape, 1)) // CHUNK
                    s = jnp.where(k_chunk <= q_chunk, s, -jnp.inf)
                m_prev = m_scr[rows, :]
                m_new = jnp.maximum(m_prev, jnp.max(s, axis=-1, keepdims=True))
                alpha = jnp.exp2(m_prev - m_new)
                pr = jnp.exp2(s - jnp.tile(m_new, (1, bk // V7X_LANES)))
                l_scr[rows, :] = alpha * l_scr[rows, :] + jnp.sum(pr, axis=-1, keepdims=True)
                m_scr[rows, :] = m_new
                a_scr[rows, :] = alpha
                p_scr[rows, :] = pr.astype(BF16)
            alpha = jnp.tile(a_scr[...], (1, 2 * DIFF_HD // V7X_LANES))
            acc_scr[...] = alpha * acc_scr[...] + _dot(p_scr[...], v_ref[0])

    @pl.when(mask_tab[p] == 1)
    def _():
        step(True)

    @pl.when(mask_tab[p] == 0)
    def _():
        step(False)

    @pl.when(last_tab[p] == 1)
    def _():
        reps = (1, 2 * DIFF_HD // V7X_LANES)
        o1, o2 = [acc_scr[...] * jnp.tile(1.0 / l_scr[...], reps)
                  for _, _, _, l_scr, _, acc_scr in heads]
        h_ref[...] = _diff_finish(o1, o2, lam_ref, sw_ref, z_ref[...], lam_init)


def _diff_prompt_tables(seq, bq, bk):
    qi_l, ki_l, mask_l, last_l = [], [], [], []
    for qi in range(seq // bq):
        q_lo, q_hi = qi * bq, (qi + 1) * bq
        nk = -(-q_hi // bk)
        for ki in range(nk):
            k_hi = (ki + 1) * bk
            fully_visible = k_hi <= (q_lo // CHUNK + 1) * CHUNK
            qi_l.append(qi)
            ki_l.append(ki)
            mask_l.append(0 if fully_visible else 1)
            last_l.append(1 if ki == nk - 1 else 0)
    return [jnp.asarray(np.asarray(a, np.int32)) for a in (qi_l, ki_l, mask_l, last_l)]


def _diff_prompt_attn(qkv, z, lam_p, subln_w, lam_init, batch, seq):
    bq = min(ATT_BQ, seq)
    bk = min(ATT_BK, seq)
    nq = seq // bq
    nkb = seq // bk
    pw = 2 * DIFF_HD
    tabs = _diff_prompt_tables(seq, bq, bk)
    n_steps = int(tabs[0].shape[0])
    grid_spec = pltpu.PrefetchScalarGridSpec(
        num_scalar_prefetch=4,
        grid=(batch, DIFF_HEADS, n_steps),
        in_specs=[
            pl.BlockSpec((1, bq, pw), lambda b, h, p, qt, kt, mt, lt: (h, b * nq + qt[p], 0)),
            pl.BlockSpec((1, bk, pw), lambda b, h, p, qt, kt, mt, lt: (DIFF_HEADS + h, b * nkb + kt[p], 0)),
            pl.BlockSpec((1, bk, pw), lambda b, h, p, qt, kt, mt, lt: (2 * DIFF_HEADS + h, b * nkb + kt[p], 0)),
            pl.BlockSpec((bq, pw), lambda b, h, p, qt, kt, mt, lt: (b * nq + qt[p], h)),
            pl.BlockSpec((4, DIFF_HD), lambda b, h, p, qt, kt, mt, lt: (0, 0)),
            pl.BlockSpec((1, pw), lambda b, h, p, qt, kt, mt, lt: (0, 0)),
        ],
        out_specs=pl.BlockSpec((bq, pw), lambda b, h, p, qt, kt, mt, lt: (b * nq + qt[p], h)),
        scratch_shapes=2 * [
            pltpu.VMEM((bq, bk), F32),
            pltpu.VMEM((bq, bk), BF16),
            pltpu.VMEM((bq, V7X_LANES), F32),
            pltpu.VMEM((bq, V7X_LANES), F32),
            pltpu.VMEM((bq, V7X_LANES), F32),
            pltpu.VMEM((bq, pw), F32),
        ],
    )
    return pl.pallas_call(
        functools.partial(_diff_prompt_kernel, bq=bq, bk=bk, rows_per_tile=min(ATT_ROWS, bq),
                          lam_init=lam_init),
        grid_spec=grid_spec,
        out_shape=jax.ShapeDtypeStruct((batch * seq, D_INNER), BF16),
        compiler_params=_params(("parallel", "parallel", "arbitrary")),
        name="diff_prompt_attn",
    )(*tabs, qkv, qkv, qkv, z, lam_p, subln_w.reshape(1, pw))


def _diff_sample_kernel(qkv_ref, z_ref, kc_ref, vc_ref, lam_ref, sw_ref, h_ref, *, lam_init):
    pw = 2 * DIFF_HD
    for pair in range(DIFF_HEADS):
        pcols = slice(pair * pw, (pair + 1) * pw)
        vc_bf = vc_ref[:, pcols].astype(BF16)
        vn_bf = qkv_ref[2 * DIFF_HEADS + pair]
        outs = []
        for h in range(2):
            cols = slice(h * DIFF_HD, (h + 1) * DIFF_HD)
            ccols = slice(pair * pw + h * DIFF_HD, pair * pw + (h + 1) * DIFF_HD)
            q_bf = qkv_ref[pair, :, cols]
            s_c = _dot_nt(q_bf, kc_ref[:, ccols].astype(BF16))
            s_n = _dot_nt(q_bf, qkv_ref[DIFF_HEADS + pair, :, cols])
            m = jnp.maximum(jnp.max(s_c, axis=-1, keepdims=True), jnp.max(s_n, axis=-1, keepdims=True))
            p_c = jnp.exp2(s_c - m)
            p_n = jnp.exp2(s_n - m)
            l = jnp.sum(p_c, axis=-1, keepdims=True) + jnp.sum(p_n, axis=-1, keepdims=True)
            acc = _dot(p_c.astype(BF16), vc_bf) + _dot(p_n.astype(BF16), vn_bf)
            outs.append(acc * (1.0 / l))
        h_ref[:, pcols] = _diff_finish(outs[0], outs[1], lam_ref, sw_ref, z_ref[:, pcols], lam_init)


def _diff_sample_attn(qkv, z, cache_k2, cache_v2, lam_p, subln_w, lam_init, batch, seq, past):
    pw = 2 * DIFF_HD
    return pl.pallas_call(
        functools.partial(_diff_sample_kernel, lam_init=lam_init),
        grid=(batch,),
        in_specs=[
            pl.BlockSpec((3 * DIFF_HEADS, seq, pw), lambda b: (0, b, 0)),
            pl.BlockSpec((seq, D_INNER), lambda b: (b, 0)),
            pl.BlockSpec((past, D_INNER), lambda b: (b, 0)),
            pl.BlockSpec((past, D_INNER), lambda b: (b, 0)),
            pl.BlockSpec((4, DIFF_HD), lambda b: (0, 0)),
            pl.BlockSpec((1, pw), lambda b: (0, 0)),
        ],
        out_specs=pl.BlockSpec((seq, D_INNER), lambda b: (b, 0)),
        out_shape=jax.ShapeDtypeStruct((batch * seq, D_INNER), BF16),
        compiler_params=_params(("parallel",)),
        name="diff_sample_attn",
    )(qkv, z, cache_k2, cache_v2, lam_p, subln_w.reshape(1, pw))


def kernel(x_prompt, x_sample, state_gla, state_hgrn, cache_k, cache_v, norm_w, final_norm_w,
           gla_w_in, gla_w_a1, gla_w_a2, gla_b_a, gla_norm_w, gla_w_out,
           hgrn_w_in, hgrn_lower_bounds, hgrn_norm_w, hgrn_w_out,
           diff_w_in, diff_lambda, diff_subln_w, diff_w_out):
    bp, tp, d = x_prompt.shape
    bs, ts, _ = x_sample.shape
    past = cache_k.shape[2]
    streams = [(bp, tp), (bs, ts)]
    xs = [x_prompt.reshape(bp * tp, d), x_sample.reshape(bs * ts, d)]

    def out_weight(w, layer):
        return w.astype(BF16) if layer == DEPTH - 1 else _tiled_weight(w, OUT_TN)

    def out_proj(h, res, w_out, layer, ssq=None):
        if layer == DEPTH - 1:
            return _final_proj(h, res, w_out, final_norm_w)
        return _res_proj(h, res, w_out, ssq)

    gla_states = [[], []]
    hgrn_states = [[], []]
    k_rows = [[], []]
    v_rows = [[], []]
    ia = ib = ic = 0
    for i in range(DEPTH):
        kind = i % N_MIXERS
        if kind == 0:
            j = ia
            ia += 1
            w_in = _tiled_weight(gla_w_in[j], PROJ_TN)
            w_a1 = jnp.pad(gla_w_a1[j], ((0, 0), (0, V7X_LANES - GLA_GATE_RANK))).astype(BF16)
            w_a2 = jnp.pad(gla_w_a2[j], ((0, V7X_LANES - GLA_GATE_RANK), (0, 0))).astype(BF16)
            w_out = out_weight(gla_w_out[j], i)
            for si, (batch, seq) in enumerate(streams):
                proj, gate_lr = _norm_proj(xs[si], norm_w[i], w_in, w_a1)
                s0 = None if si == 0 else state_gla[j]
                h, s_new = _gla_mix(proj, gate_lr, w_a2, gla_b_a[j], gla_norm_w[j], s0, batch, seq)
                xs[si] = out_proj(h, xs[si], w_out, i)
                gla_states[si].append(s_new)
        elif kind == 1:
            j = ib
            ib += 1
            w_in = _tiled_weight(hgrn_w_in[j], PROJ_TN)
            w_out = out_weight(hgrn_w_out[j], i)
            for si, (batch, seq) in enumerate(streams):
                proj = _norm_proj(xs[si], norm_w[i], w_in)
                s0 = None if si == 0 else state_hgrn[j]
                u, ssq, s_new = _hgrn_mix(proj, hgrn_lower_bounds, i, hgrn_norm_w[j], s0, batch, seq)
                xs[si] = out_proj(u, xs[si], w_out, i, ssq)
                hgrn_states[si].append(s_new)
        else:
            j = ic
            ic += 1
            lam_init = 0.8 - 0.6 * math.exp(-0.3 * i)
            w_in = _tiled_weight(diff_w_in[j], DIFF_TN)
            w_out = out_weight(diff_w_out[j], i)
            for si, (batch, seq) in enumerate(streams):
                qkv, k_new, v_new, z = _diff_proj(xs[si], norm_w[i], w_in)
                if si == 0:
                    h = _diff_prompt_attn(qkv, z, diff_lambda[j], diff_subln_w[j], lam_init, batch, seq)
                else:
                    ck = cache_k[j].reshape(batch * past, 2 * DIFF_HEADS * DIFF_HD)
                    cv = cache_v[j].reshape(batch * past, 2 * DIFF_HEADS * DIFF_HD)
                    h = _diff_sample_attn(qkv, z, ck, cv, diff_lambda[j], diff_subln_w[j], lam_init,
                                          batch, seq, past)
                xs[si] = out_proj(h, xs[si], w_out, i)
                k_rows[si].append(k_new.reshape(batch, seq, 2 * DIFF_HEADS, DIFF_HD))
                v_rows[si].append(v_new.reshape(batch, seq, DIFF_HEADS, 2 * DIFF_HD))

    return (xs[0].reshape(bp, tp, d), xs[1].reshape(bs, ts, d),
            jnp.stack(gla_states[0]), jnp.stack(gla_states[1]),
            jnp.stack(hgrn_states[0]), jnp.stack(hgrn_states[1]),
            jnp.stack(k_rows[0]), jnp.stack(v_rows[0]),
            jnp.stack(k_rows[1]), jnp.stack(v_rows[1]))
```

```python
import functools
import math

import numpy as np
import jax
import jax.numpy as jnp
from jax import lax
from jax.experimental import pallas as pl
from jax.experimental.pallas import tpu as pltpu

D_MODEL = 2048
DEPTH = 4
CHUNK = 64
N_MIXERS = 3
D_INNER = D_MODEL
NORM_EPS = 1e-6
GLA_HEADS = 4
GLA_KD = D_INNER // 2
GLA_DK = GLA_KD // GLA_HEADS
GLA_DV = D_INNER // GLA_HEADS
GLA_GATE_RANK = 16
GLA_GATE_NORMALIZER = 16.0
HGRN_HEADS = 16
HGRN_DK = 128
HGRN_DV = 128
DIFF_HD = 128
DIFF_HEADS = 8

LOG2_E = math.log2(math.e)
DIFF_Q_SCALE = (DIFF_HD ** -0.5) * LOG2_E

F32 = jnp.float32
BF16 = jnp.bfloat16

V7X_LANES = 128
V7X_VMEM_BYTES = 64 * 1024 * 1024
VMEM_LIMIT_BYTES = (V7X_VMEM_BYTES * 3) // 4

PROJ_TM = 1024
PROJ_TN = 1024
DIFF_TN = 512
OUT_TN = 1024
FINAL_TM = 512
SMALL_M_WEIGHT_SPLITS = 4
REC_TOKENS = 256
GLA_HEADS_PER_STEP = 4
HGRN_HEADS_PER_STEP = 4
ATT_BQ = 512
ATT_BK = 512
ATT_ROWS = 128


def _params(sem):
    return pltpu.CompilerParams(dimension_semantics=sem, vmem_limit_bytes=VMEM_LIMIT_BYTES)


def _weight_splits(m):
    return 1 if m >= PROJ_TM else SMALL_M_WEIGHT_SPLITS


def _weight_specs(d, tn, splits, index):
    return [pl.BlockSpec((d // splits, tn), functools.partial(lambda k, *g: (k, index(*g)), k))
            for k in range(splits)]


def _dot_split(h, w_refs):
    dk = h.shape[1] // len(w_refs)
    acc = _dot(h[:, 0:dk], w_refs[0][...])
    for k in range(1, len(w_refs)):
        acc = acc + _dot(h[:, k * dk:(k + 1) * dk], w_refs[k][...])
    return acc


def _dot(a, b):
    return jnp.dot(a, b, preferred_element_type=F32)


def _dot_nt(a, b):
    return lax.dot_general(a, b, (((1,), (1,)), ((), ())), preferred_element_type=F32)


def _dot_tn(a, b):
    return lax.dot_general(a, b, (((0,), (0,)), ((), ())), preferred_element_type=F32)


def _sigmoid(x):
    return 1.0 / (1.0 + jnp.exp(-x))


def _silu(x):
    return x * _sigmoid(x)


def _rms(x, w):
    ms = jnp.mean(x * x, axis=-1, keepdims=True)
    return x * lax.rsqrt(ms + NORM_EPS) * w


def _norm_proj_kernel(*refs, has_aux, splits):
    x_ref, nw_ref = refs[:2]
    w_refs = refs[2:2 + splits]
    if has_aux:
        aw_ref, o_ref, ao_ref, h_scr = refs[2 + splits:]
    else:
        o_ref, h_scr = refs[2 + splits:]

    @pl.when(pl.program_id(1) == 0)
    def _():
        h_scr[...] = _rms(x_ref[...], nw_ref[...]).astype(BF16)
        if has_aux:
            ao_ref[...] = _dot(h_scr[...], aw_ref[...])

    o_ref[...] = _dot_split(h_scr[...], w_refs)


def _norm_proj(x, nw, w_bf, aux_w_bf=None):
    m, d = x.shape
    n = w_bf.shape[1]
    tm = min(PROJ_TM, m)
    tn = PROJ_TN
    splits = _weight_splits(m)
    in_specs = [
        pl.BlockSpec((tm, d), lambda i, j: (i, 0)),
        pl.BlockSpec((1, d), lambda i, j: (0, 0)),
        *_weight_specs(d, tn, splits, lambda i, j: j),
    ]
    out_shape = [jax.ShapeDtypeStruct((m, n), F32)]
    out_specs = [pl.BlockSpec((tm, tn), lambda i, j: (i, j))]
    args = [x, nw.reshape(1, d)] + splits * [w_bf]
    if aux_w_bf is not None:
        na = aux_w_bf.shape[1]
        in_specs.append(pl.BlockSpec((d, na), lambda i, j: (0, 0)))
        out_shape.append(jax.ShapeDtypeStruct((m, na), F32))
        out_specs.append(pl.BlockSpec((tm, na), lambda i, j: (i, 0)))
        args.append(aux_w_bf)
    outs = pl.pallas_call(
        functools.partial(_norm_proj_kernel, has_aux=aux_w_bf is not None, splits=splits),
        grid=(m // tm, n // tn),
        in_specs=in_specs,
        out_specs=out_specs,
        out_shape=out_shape,
        scratch_shapes=[pltpu.VMEM((tm, d), BF16)],
        compiler_params=_params(("parallel", "arbitrary")),
        name="norm_proj",
    )(*args)
    return outs if aux_w_bf is not None else outs[0]


def _diff_proj_kernel(*refs, group_tiles, splits):
    x_ref, nw_ref = refs[:2]
    w_refs = refs[2:2 + splits]
    pm_ref, k_ref, v_ref, z_ref, h_scr = refs[2 + splits:]
    j = pl.program_id(1)

    @pl.when(j == 0)
    def _():
        h_scr[...] = _rms(x_ref[...], nw_ref[...]).astype(BF16)

    acc = _dot_split(h_scr[...], w_refs)
    pw = pm_ref.shape[2]

    def write_pairs(val):
        for p in range(pm_ref.shape[0]):
            pm_ref[p] = val[:, p * pw:(p + 1) * pw].astype(BF16)

    @pl.when(j < group_tiles)
    def _():
        write_pairs(acc * DIFF_Q_SCALE)

    @pl.when((j >= group_tiles) & (j < 2 * group_tiles))
    def _():
        write_pairs(acc)
        k_ref[...] = acc

    @pl.when((j >= 2 * group_tiles) & (j < 3 * group_tiles))
    def _():
        write_pairs(acc)
        v_ref[...] = acc

    @pl.when(j >= 3 * group_tiles)
    def _():
        z_ref[...] = acc


def _diff_proj(x, nw, w_bf):
    m, d = x.shape
    tm = min(PROJ_TM, m)
    tn = DIFF_TN
    splits = _weight_splits(m)
    pw = 2 * DIFF_HD
    g = D_INNER // tn
    ppt = tn // pw

    def rows_spec(group):
        return pl.BlockSpec((tm, tn), lambda i, j: (i, jnp.clip(j - group * g, 0, g - 1)))

    return pl.pallas_call(
        functools.partial(_diff_proj_kernel, group_tiles=g, splits=splits),
        grid=(m // tm, 4 * g),
        in_specs=[
            pl.BlockSpec((tm, d), lambda i, j: (i, 0)),
            pl.BlockSpec((1, d), lambda i, j: (0, 0)),
            *_weight_specs(d, tn, splits, lambda i, j: j),
        ],
        out_specs=[
            pl.BlockSpec((ppt, tm, pw), lambda i, j: (jnp.minimum(j, 3 * g - 1), i, 0)),
            rows_spec(1), rows_spec(2), rows_spec(3),
        ],
        out_shape=[
            jax.ShapeDtypeStruct((3 * DIFF_HEADS, m, pw), BF16),
            jax.ShapeDtypeStruct((m, D_INNER), F32),
            jax.ShapeDtypeStruct((m, D_INNER), F32),
            jax.ShapeDtypeStruct((m, D_INNER), F32),
        ],
        scratch_shapes=[pltpu.VMEM((tm, d), BF16)],
        compiler_params=_params(("parallel", "arbitrary")),
        name="diff_proj",
    )(x, nw.reshape(1, d), *(splits * [w_bf]))


def _row_rms_scale(ssq_ref):
    ssq = ssq_ref[0]
    for gi in range(1, ssq_ref.shape[0]):
        ssq = ssq + ssq_ref[gi]
    return lax.rsqrt(ssq * (1.0 / D_INNER) + NORM_EPS)


def _res_proj_kernel(*refs, has_ssq, splits):
    h_ref = refs[0]
    ssq_ref = refs[1] if has_ssq else None
    res_ref = refs[1 + has_ssq]
    w_refs = refs[2 + has_ssq:2 + has_ssq + splits]
    y_ref = refs[2 + has_ssq + splits]
    acc = _dot_split(h_ref[...], w_refs)
    if has_ssq:
        acc = acc * _row_rms_scale(ssq_ref)
    y_ref[...] = res_ref[...] + acc


def _res_proj(h_bf, res, w_bf, ssq=None):
    m, d = h_bf.shape
    n = w_bf.shape[1]
    tm = min(PROJ_TM, m)
    tn = OUT_TN
    splits = _weight_splits(m)
    in_specs = [pl.BlockSpec((tm, d), lambda i, j: (i, 0))]
    args = [h_bf]
    if ssq is not None:
        in_specs.append(pl.BlockSpec((ssq.shape[0], tm, 1), lambda i, j: (0, i, 0)))
        args.append(ssq)
    in_specs += [
        pl.BlockSpec((tm, tn), lambda i, j: (i, j)),
        *_weight_specs(d, tn, splits, lambda i, j: j),
    ]
    args += [res] + splits * [w_bf]
    return pl.pallas_call(
        functools.partial(_res_proj_kernel, has_ssq=ssq is not None, splits=splits),
        grid=(m // tm, n // tn),
        in_specs=in_specs,
        out_specs=pl.BlockSpec((tm, tn), lambda i, j: (i, j)),
        out_shape=jax.ShapeDtypeStruct((m, n), F32),
        compiler_params=_params(("parallel", "arbitrary")),
        name="res_proj",
    )(*args)


def _final_proj_kernel(*refs, splits):
    h_ref, res_ref = refs[:2]
    w_refs = refs[2:2 + splits]
    fw_ref, y_ref = refs[2 + splits:]
    y_ref[...] = _rms(res_ref[...] + _dot_split(h_ref[...], w_refs), fw_ref[...])


def _final_proj(h_bf, res, w_bf, final_w):
    m, d = h_bf.shape
    n = w_bf.shape[1]
    tm = min(FINAL_TM, m)
    splits = _weight_splits(m)
    return pl.pallas_call(
        functools.partial(_final_proj_kernel, splits=splits),
        grid=(m // tm,),
        in_specs=[
            pl.BlockSpec((tm, d), lambda i: (i, 0)),
            pl.BlockSpec((tm, n), lambda i: (i, 0)),
            *_weight_specs(d, n, splits, lambda i: 0),
            pl.BlockSpec((1, n), lambda i: (0, 0)),
        ],
        out_specs=pl.BlockSpec((tm, n), lambda i: (i, 0)),
        out_shape=jax.ShapeDtypeStruct((m, n), F32),
        compiler_params=_params(("parallel",)),
        name="final_proj",
    )(h_bf, res, *(splits * [w_bf]), final_w.reshape(1, n))


def _block_constants(tb, chunk):
    row = np.arange(tb)[:, None]
    col = np.arange(tb)[None, :]
    rc, cc = row // chunk, col // chunk
    same = (rc == cc) & (row >= col)
    dist = np.where(same, 0, np.where(rc > cc, rc - cc, -1)).astype(np.int32)
    return jnp.asarray(same.astype(np.float32), BF16), jnp.asarray(dist)


def _chunk_cumsum(g, tri_bf):
    g_hi = g.astype(BF16)
    g_lo = (g - g_hi.astype(F32)).astype(BF16)
    return _dot(tri_bf, g_hi) + _dot(tri_bf, g_lo)


def _per_chunk(rows, chunk):
    parts = [jnp.broadcast_to(r, (chunk, r.shape[1])) for r in rows]
    return parts[0] if len(parts) == 1 else jnp.concatenate(parts, axis=0)


def _sum_rows(rows, width):
    out = jnp.zeros((1, width), F32)
    for r in rows:
        out = out + r
    return out


def _block_decay_column(b, chunk):
    tb, w = b.shape
    total = _sum_rows([b[(c + 1) * chunk - 1:(c + 1) * chunk, :] for c in range(tb // chunk)], w)
    return jnp.transpose(jnp.broadcast_to(jnp.exp2(total), (V7X_LANES, w)))[:, 0:1]


def _block_recurrence(q, k, b, v_bf, s0, dist, dec_col, chunk, q_log2_scale):
    w = q.shape[1]
    n = q.shape[0] // chunk
    mids = [b[c * chunk + chunk // 2 - 1:c * chunk + chunk // 2, :] for c in range(n)]
    lasts = [b[(c + 1) * chunk - 1:(c + 1) * chunk, :] for c in range(n)]
    q_mid = q * jnp.exp2(b - _per_chunk([m - q_log2_scale for m in mids], chunk))
    k_mid = k * jnp.exp2(_per_chunk(mids, chunk) - b)
    q_in = q_mid * _per_chunk([jnp.exp2(m) for m in mids], chunk)
    k_out = k_mid * _per_chunk([jnp.exp2(l - m) for l, m in zip(lasts, mids)], chunk)
    attn = jnp.where(dist == 0, _dot_nt(q_mid.astype(BF16), k_mid.astype(BF16)), 0.0)
    q_in_bf = q_in.astype(BF16)
    for d in range(1, n):
        gaps = [jnp.exp2(_sum_rows(lasts[j + 1:j + d], w)) if j + d < n else jnp.zeros((1, w), F32)
                for j in range(n)]
        k_d = k_out if d == 1 else k_out * _per_chunk(gaps, chunk)
        attn = jnp.where(dist == d, _dot_nt(q_in_bf, k_d.astype(BF16)), attn)
    head = _per_chunk([jnp.exp2(_sum_rows(lasts[:c], w)) for c in range(n)], chunk)
    tail = _per_chunk([jnp.exp2(_sum_rows(lasts[c + 1:], w)) for c in range(n)], chunk)
    o = _dot((q_in * head).astype(BF16), s0.astype(BF16)) + _dot(attn.astype(BF16), v_bf)
    s_end = dec_col * s0 + _dot_tn((k_out * tail).astype(BF16), v_bf)
    return o, s_end


def _log_sigmoid(x):
    return jnp.minimum(x, 0.0) - jnp.log(1.0 + jnp.exp(-jnp.abs(x)))


def _gla_kernel(*refs, chunk, zero_init, heads):
    if zero_init:
        (q_ref, k_ref, v_ref, z_ref, ga_ref, wa2_ref, ba_ref, nw_ref, tri_ref, dist_ref,
         h_ref, sout_ref, s_scr) = refs
    else:
        (q_ref, k_ref, v_ref, z_ref, ga_ref, wa2_ref, ba_ref, nw_ref, tri_ref, dist_ref, s0_ref,
         h_ref, sout_ref, s_scr) = refs
    t = pl.program_id(2)

    @pl.when(t == 0)
    def _():
        if zero_init:
            s_scr[...] = jnp.zeros_like(s_scr)
        else:
            s_scr[...] = s0_ref[0]

    gate_in = _dot(ga_ref[...].astype(BF16), wa2_ref[...]) + ba_ref[...]
    g = _log_sigmoid(gate_in) * (LOG2_E / GLA_GATE_NORMALIZER)
    b = _chunk_cumsum(g, tri_ref[...])
    dec_col = _block_decay_column(b, chunk)
    dist = dist_ref[...]
    nw = nw_ref[...]
    for hi in range(heads):
        kc = slice(hi * GLA_DK, (hi + 1) * GLA_DK)
        vc = slice(hi * GLA_DV, (hi + 1) * GLA_DV)
        o, s = _block_recurrence(q_ref[:, kc], k_ref[:, kc], b[:, kc], v_ref[:, vc].astype(BF16),
                                 s_scr[hi], dist, dec_col[kc, :], chunk, math.log2(GLA_DK ** -0.5))
        s_scr[hi] = s
        h_ref[:, vc] = (_rms(o, nw) * _silu(z_ref[:, vc])).astype(BF16)

    @pl.when(t == pl.num_programs(2) - 1)
    def _():
        sout_ref[0] = s_scr[...]


def _gla_mix(proj, gate_lr, wa2_bf, b_a, norm_w, s0, batch, seq):
    chunk = min(CHUNK, seq)
    tb = min(REC_TOKENS, seq)
    nt = seq // tb
    hb = GLA_HEADS_PER_STEP
    kb = hb * GLA_DK
    vb = hb * GLA_DV
    k_off = GLA_KD // kb
    v_off = (2 * GLA_KD) // vb
    z_off = (2 * GLA_KD + D_INNER) // vb
    zero_init = s0 is None
    in_specs = [
        pl.BlockSpec((tb, kb), lambda b, h, t: (b * nt + t, h)),
        pl.BlockSpec((tb, kb), lambda b, h, t: (b * nt + t, k_off + h)),
        pl.BlockSpec((tb, vb), lambda b, h, t: (b * nt + t, v_off + h)),
        pl.BlockSpec((tb, vb), lambda b, h, t: (b * nt + t, z_off + h)),
        pl.BlockSpec((tb, V7X_LANES), lambda b, h, t: (b * nt + t, 0)),
        pl.BlockSpec((V7X_LANES, kb), lambda b, h, t: (0, h)),
        pl.BlockSpec((1, kb), lambda b, h, t: (0, h)),
        pl.BlockSpec((1, GLA_DV), lambda b, h, t: (0, 0)),
        pl.BlockSpec((tb, tb), lambda b, h, t: (0, 0)),
        pl.BlockSpec((tb, tb), lambda b, h, t: (0, 0)),
    ]
    args = [proj, proj, proj, proj, gate_lr, wa2_bf, b_a.reshape(1, GLA_KD), norm_w.reshape(1, GLA_DV),
            *_block_constants(tb, chunk)]
    if not zero_init:
        in_specs.append(pl.BlockSpec((1, hb, GLA_DK, GLA_DV), lambda b, h, t: (b, h, 0, 0)))
        args.append(s0)
    h, s_out = pl.pallas_call(
        functools.partial(_gla_kernel, chunk=chunk, zero_init=zero_init, heads=hb),
        grid=(batch, GLA_HEADS // hb, nt),
        in_specs=in_specs,
        out_specs=[
            pl.BlockSpec((tb, vb), lambda b, h, t: (b * nt + t, h)),
            pl.BlockSpec((1, hb, GLA_DK, GLA_DV), lambda b, h, t: (b, h, 0, 0)),
        ],
        out_shape=[
            jax.ShapeDtypeStruct((batch * seq, D_INNER), BF16),
            jax.ShapeDtypeStruct((batch, GLA_HEADS, GLA_DK, GLA_DV), F32),
        ],
        scratch_shapes=[pltpu.VMEM((hb, GLA_DK, GLA_DV), F32)],
        compiler_params=_params(("parallel", "parallel", "arbitrary")),
        name="gla_mix",
    )(*args)
    return h, s_out


def _hgrn_kernel(*refs, chunk, zero_init, layer, heads):
    if zero_init:
        (q_ref, f_ref, i_ref, z_ref, lbp_ref, nw_ref, tri_ref, dist_ref,
         u_ref, ssq_ref, sout_ref, s_scr) = refs
    else:
        (q_ref, f_ref, i_ref, z_ref, lbp_ref, nw_ref, tri_ref, dist_ref, s0_ref,
         u_ref, ssq_ref, sout_ref, s_scr) = refs
    t = pl.program_id(2)

    @pl.when(t == 0)
    def _():
        if zero_init:
            s_scr[...] = jnp.zeros_like(s_scr)
        else:
            s_scr[...] = s0_ref[0]

    lbp = lbp_ref[...]
    lbe = jnp.exp(lbp - jnp.max(lbp, axis=0, keepdims=True))
    lbs = lbe / jnp.sum(lbe, axis=0, keepdims=True)
    lb = jnp.zeros_like(lbs[0:1])
    for r in range(1, layer + 1):
        lb = lb + lbs[r:r + 1]

    q = _silu(q_ref[...])
    fgate = lb + (1.0 - lb) * _sigmoid(f_ref[...])
    k = 1.0 - fgate
    b = _chunk_cumsum(jnp.log(fgate) * LOG2_E, tri_ref[...])
    dec_col = _block_decay_column(b, chunk)
    v_bf = i_ref[...].astype(BF16)
    dist = dist_ref[...]
    ssq = jnp.zeros((q.shape[0], 1), F32)
    for hi in range(heads):
        cols = slice(hi * HGRN_DK, (hi + 1) * HGRN_DK)
        o, s = _block_recurrence(q[:, cols], k[:, cols], b[:, cols], v_bf[:, cols], s_scr[hi],
                                 dist, dec_col[cols, :], chunk, math.log2(HGRN_DK ** -0.5))
        s_scr[hi] = s
        ssq = ssq + jnp.sum(o * o, axis=-1, keepdims=True)
        u_ref[:, cols] = (o * nw_ref[:, cols] * _silu(z_ref[:, cols])).astype(BF16)
    ssq_ref[0] = ssq

    @pl.when(t == pl.num_programs(2) - 1)
    def _():
        sout_ref[0] = s_scr[...]


def _hgrn_mix(proj, lower_bounds, layer, norm_w, s0, batch, seq):
    chunk = min(CHUNK, seq)
    tb = min(REC_TOKENS, seq)
    nt = seq // tb
    hb = HGRN_HEADS_PER_STEP
    wb = hb * HGRN_DK
    ngroups = HGRN_HEADS // hb
    col_groups = D_INNER // wb
    zero_init = s0 is None
    in_specs = [
        pl.BlockSpec((tb, wb), lambda b, h, t: (b * nt + t, h)),
        pl.BlockSpec((tb, wb), lambda b, h, t: (b * nt + t, col_groups + h)),
        pl.BlockSpec((tb, wb), lambda b, h, t: (b * nt + t, 2 * col_groups + h)),
        pl.BlockSpec((tb, wb), lambda b, h, t: (b * nt + t, 3 * col_groups + h)),
        pl.BlockSpec((DEPTH, wb), lambda b, h, t: (0, h)),
        pl.BlockSpec((1, wb), lambda b, h, t: (0, h)),
        pl.BlockSpec((tb, tb), lambda b, h, t: (0, 0)),
        pl.BlockSpec((tb, tb), lambda b, h, t: (0, 0)),
    ]
    args = [proj, proj, proj, proj, lower_bounds, norm_w.reshape(1, D_INNER), *_block_constants(tb, chunk)]
    if not zero_init:
        in_specs.append(pl.BlockSpec((1, hb, HGRN_DK, HGRN_DV), lambda b, h, t: (b, h, 0, 0)))
        args.append(s0)
    u, ssq, s_out = pl.pallas_call(
        functools.partial(_hgrn_kernel, chunk=chunk, zero_init=zero_init, layer=layer, heads=hb),
        grid=(batch, ngroups, nt),
        in_specs=in_specs,
        out_specs=[
            pl.BlockSpec((tb, wb), lambda b, h, t: (b * nt + t, h)),
            pl.BlockSpec((1, tb, 1), lambda b, h, t: (h, b * nt + t, 0)),
            pl.BlockSpec((1, hb, HGRN_DK, HGRN_DV), lambda b, h, t: (b, h, 0, 0)),
        ],
        out_shape=[
            jax.ShapeDtypeStruct((batch * seq, D_INNER), BF16),
            jax.ShapeDtypeStruct((ngroups, batch * seq, 1), F32),
            jax.ShapeDtypeStruct((batch, HGRN_HEADS, HGRN_DK, HGRN_DV), F32),
        ],
        scratch_shapes=[pltpu.VMEM((hb, HGRN_DK, HGRN_DV), F32)],
        compiler_params=_params(("parallel", "parallel", "arbitrary")),
        name="hgrn_mix",
    )(*args)
    return u, ssq, s_out


def _diff_lambda(lam_ref, lam_init):
    lp = lam_ref[...]
    a = jnp.sum(lp[0:1] * lp[1:2], axis=-1, keepdims=True)
    b = jnp.sum(lp[2:3] * lp[3:4], axis=-1, keepdims=True)
    return jnp.exp(a) - jnp.exp(b) + lam_init


def _diff_finish(o1, o2, lam_ref, sw_ref, z, lam_init):
    o = o1 - _diff_lambda(lam_ref, lam_init) * o2
    o = _rms(o, sw_ref[...]) * (1.0 - lam_init)
    return (o * _silu(z)).astype(BF16)


KIND_FIRST, KIND_PLAIN, KIND_MASKED, KIND_DRAIN = 0, 1, 3, 5


def _diff_prompt_kernel(qa_tab, ka_tab, qb_tab, kb_tab, kind_tab, last_tab,
                        q_ref, k_ref, v_ref, z_ref, lam_ref, sw_ref, h_ref, *scratch,
                        bq, bk, rows_per_tile, lam_init):
    per_head = len(scratch) // 2
    heads = []
    for h in range(2):
        s_scr, p0, p1, m_scr, l_scr, a0, a1, acc_scr = scratch[per_head * h:per_head * (h + 1)]
        heads.append((s_scr, (p0, p1), m_scr, l_scr, (a0, a1), acc_scr))
    p = pl.program_id(2)
    qi = qa_tab[p]
    ki = ka_tab[p]
    par_a = qi % 2
    par_b = qb_tab[p] % 2
    first = ki == 0
    reps = (1, 2 * DIFF_HD // V7X_LANES)

    def stage_a(masked, slot):
        for h, (s_scr, p_scr, m_scr, l_scr, a_scr, acc_scr) in enumerate(heads):
            cols = slice(h * DIFF_HD, (h + 1) * DIFF_HD)
            s_scr[...] = _dot_nt(q_ref[0, :, cols], k_ref[0, :, cols])
        for s_scr, p_scr, m_scr, l_scr, a_scr, acc_scr in heads:
            for r in range(bq // rows_per_tile):
                rows = pl.ds(r * rows_per_tile, rows_per_tile)
                s = s_scr[rows, :]
                if masked:
                    shape = (rows_per_tile, bk)
                    q_chunk = (qi * bq + r * rows_per_tile + lax.broadcasted_iota(jnp.int32, shape, 0)) // CHUNK
                    k_chunk = (ki * bk + lax.broadcasted_iota(jnp.int32, shape, 1)) // CHUNK
                    s = jnp.where(k_chunk <= q_chunk, s, -jnp.inf)
                m_prev = jnp.where(first, -jnp.inf, m_scr[rows, :])
                m_new = jnp.maximum(m_prev, jnp.max(s, axis=-1, keepdims=True))
                alpha = jnp.exp2(m_prev - m_new)
                pr = jnp.exp2(s - jnp.tile(m_new, (1, bk // V7X_LANES)))
                l_scr[par_a, rows, :] = alpha * l_scr[par_a, rows, :] + jnp.sum(pr, axis=-1, keepdims=True)
                m_scr[rows, :] = m_new
                a_scr[slot][rows, :] = alpha
                p_scr[slot][rows, :] = pr.astype(BF16)

    def stage_b(slot):
        for s_scr, p_scr, m_scr, l_scr, a_scr, acc_scr in heads:
            alpha = jnp.tile(a_scr[slot][...], reps)
            acc_scr[...] = alpha * acc_scr[...] + _dot(p_scr[slot][...], v_ref[0])

    kind = kind_tab[p]

    @pl.when(kind == KIND_FIRST)
    def _():
        for s_scr, p_scr, m_scr, l_scr, a_scr, acc_scr in heads:
            m_scr[...] = jnp.full_like(m_scr, -jnp.inf)
            l_scr[...] = jnp.zeros_like(l_scr)
            acc_scr[...] = jnp.zeros_like(acc_scr)
        stage_a(True, 0)

    for slot in range(2):
        @pl.when(kind == KIND_PLAIN + slot)
        def _():
            stage_a(False, slot)
            stage_b(1 - slot)

        @pl.when(kind == KIND_MASKED + slot)
        def _():
            stage_a(True, slot)
            stage_b(1 - slot)

        @pl.when(kind == KIND_DRAIN + slot)
        def _():
            stage_b(1 - slot)

    @pl.when(last_tab[p] == 1)
    def _():
        o1, o2 = [acc_scr[...] * jnp.tile(1.0 / l_scr[par_b], reps)
                  for _, _, _, l_scr, _, acc_scr in heads]
        h_ref[...] = _diff_finish(o1, o2, lam_ref, sw_ref, z_ref[...], lam_init)


def _diff_prompt_tables(seq, bq, bk):
    pairs = []
    for qi in range(seq // bq):
        q_lo, q_hi = qi * bq, (qi + 1) * bq
        nk = -(-q_hi // bk)
        for ki in range(nk):
            fully_visible = (ki + 1) * bk <= (q_lo // CHUNK + 1) * CHUNK
            pairs.append((qi, ki, not fully_visible, ki == nk - 1))
    assert pairs[0][2], "the first pair sits on the diagonal"
    n = len(pairs)
    qa = [pairs[min(p, n - 1)][0] for p in range(n + 1)]
    ka = [pairs[min(p, n - 1)][1] for p in range(n + 1)]
    qb = [pairs[max(p - 1, 0)][0] for p in range(n + 1)]
    kb = [pairs[max(p - 1, 0)][1] for p in range(n + 1)]
    kind = ([KIND_FIRST] + [(KIND_MASKED if pairs[p][2] else KIND_PLAIN) + p % 2 for p in range(1, n)]
            + [KIND_DRAIN + n % 2])
    last = [0] + [1 if pairs[p - 1][3] else 0 for p in range(1, n + 1)]
    return [jnp.asarray(np.asarray(t, np.int32)) for t in (qa, ka, qb, kb, kind, last)]


def _diff_prompt_attn(qkv, z, lam_p, subln_w, lam_init, batch, seq):
    bq = min(ATT_BQ, seq)
    bk = min(ATT_BK, seq)
    nq = seq // bq
    nkb = seq // bk
    pw = 2 * DIFF_HD
    tabs = _diff_prompt_tables(seq, bq, bk)
    n_steps = int(tabs[0].shape[0])

    def spec(shape, index):
        return pl.BlockSpec(shape, lambda b, h, p, qa, ka, qb, kb, kind, last: index(b, h, p, qa, ka, qb, kb))

    grid_spec = pltpu.PrefetchScalarGridSpec(
        num_scalar_prefetch=6,
        grid=(batch, DIFF_HEADS, n_steps),
        in_specs=[
            spec((1, bq, pw), lambda b, h, p, qa, ka, qb, kb: (h, b * nq + qa[p], 0)),
            spec((1, bk, pw), lambda b, h, p, qa, ka, qb, kb: (DIFF_HEADS + h, b * nkb + ka[p], 0)),
            spec((1, bk, pw), lambda b, h, p, qa, ka, qb, kb: (2 * DIFF_HEADS + h, b * nkb + kb[p], 0)),
            spec((bq, pw), lambda b, h, p, qa, ka, qb, kb: (b * nq + qb[p], h)),
            spec((4, DIFF_HD), lambda b, h, p, qa, ka, qb, kb: (0, 0)),
            spec((1, pw), lambda b, h, p, qa, ka, qb, kb: (0, 0)),
        ],
        out_specs=spec((bq, pw), lambda b, h, p, qa, ka, qb, kb: (b * nq + qb[p], h)),
        scratch_shapes=2 * [
            pltpu.VMEM((bq, bk), F32),
            pltpu.VMEM((bq, bk), BF16),
            pltpu.VMEM((bq, bk), BF16),
            pltpu.VMEM((bq, V7X_LANES), F32),
            pltpu.VMEM((2, bq, V7X_LANES), F32),
            pltpu.VMEM((bq, V7X_LANES), F32),
            pltpu.VMEM((bq, V7X_LANES), F32),
            pltpu.VMEM((bq, pw), F32),
        ],
    )
    return pl.pallas_call(
        functools.partial(_diff_prompt_kernel, bq=bq, bk=bk, rows_per_tile=min(ATT_ROWS, bq),
                          lam_init=lam_init),
        grid_spec=grid_spec,
        out_shape=jax.ShapeDtypeStruct((batch * seq, D_INNER), BF16),
        compiler_params=_params(("parallel", "parallel", "arbitrary")),
        name="diff_prompt_attn",
    )(*tabs, qkv, qkv, qkv, z, lam_p, subln_w.reshape(1, pw))


def _diff_sample_kernel(qkv_ref, z_ref, kc_ref, vc_ref, lam_ref, sw_ref, h_ref, *, lam_init):
    pw = 2 * DIFF_HD
    for pair in range(DIFF_HEADS):
        pcols = slice(pair * pw, (pair + 1) * pw)
        vc_bf = vc_ref[:, pcols].astype(BF16)
        vn_bf = qkv_ref[2 * DIFF_HEADS + pair]
        outs = []
        for h in range(2):
            cols = slice(h * DIFF_HD, (h + 1) * DIFF_HD)
            ccols = slice(pair * pw + h * DIFF_HD, pair * pw + (h + 1) * DIFF_HD)
            q_bf = qkv_ref[pair, :, cols]
            s_c = _dot_nt(q_bf, kc_ref[:, ccols].astype(BF16))
            s_n = _dot_nt(q_bf, qkv_ref[DIFF_HEADS + pair, :, cols])
            m = jnp.maximum(jnp.max(s_c, axis=-1, keepdims=True), jnp.max(s_n, axis=-1, keepdims=True))
            p_c = jnp.exp2(s_c - m)
            p_n = jnp.exp2(s_n - m)
            l = jnp.sum(p_c, axis=-1, keepdims=True) + jnp.sum(p_n, axis=-1, keepdims=True)
            acc = _dot(p_c.astype(BF16), vc_bf) + _dot(p_n.astype(BF16), vn_bf)
            outs.append(acc * (1.0 / l))
        h_ref[:, pcols] = _diff_finish(outs[0], outs[1], lam_ref, sw_ref, z_ref[:, pcols], lam_init)


def _diff_sample_attn(qkv, z, cache_k2, cache_v2, lam_p, subln_w, lam_init, batch, seq, past):
    pw = 2 * DIFF_HD
    return pl.pallas_call(
        functools.partial(_diff_sample_kernel, lam_init=lam_init),
        grid=(batch,),
        in_specs=[
            pl.BlockSpec((3 * DIFF_HEADS, seq, pw), lambda b: (0, b, 0)),
            pl.BlockSpec((seq, D_INNER), lambda b: (b, 0)),
            pl.BlockSpec((past, D_INNER), lambda b: (b, 0)),
            pl.BlockSpec((past, D_INNER), lambda b: (b, 0)),
            pl.BlockSpec((4, DIFF_HD), lambda b: (0, 0)),
            pl.BlockSpec((1, pw), lambda b: (0, 0)),
        ],
        out_specs=pl.BlockSpec((seq, D_INNER), lambda b: (b, 0)),
        out_shape=jax.ShapeDtypeStruct((batch * seq, D_INNER), BF16),
        compiler_params=_params(("parallel",)),
        name="diff_sample_attn",
    )(qkv, z, cache_k2, cache_v2, lam_p, subln_w.reshape(1, pw))


def kernel(x_prompt, x_sample, state_gla, state_hgrn, cache_k, cache_v, norm_w, final_norm_w,
           gla_w_in, gla_w_a1, gla_w_a2, gla_b_a, gla_norm_w, gla_w_out,
           hgrn_w_in, hgrn_lower_bounds, hgrn_norm_w, hgrn_w_out,
           diff_w_in, diff_lambda, diff_subln_w, diff_w_out):
    bp, tp, d = x_prompt.shape
    bs, ts, _ = x_sample.shape
    past = cache_k.shape[2]
    streams = [(bp, tp), (bs, ts)]
    xs = [x_prompt.reshape(bp * tp, d), x_sample.reshape(bs * ts, d)]

    def out_proj(h, res, w_out, layer, ssq=None):
        if layer == DEPTH - 1:
            return _final_proj(h, res, w_out, final_norm_w)
        return _res_proj(h, res, w_out, ssq)

    gla_states = [[], []]
    hgrn_states = [[], []]
    k_rows = [[], []]
    v_rows = [[], []]
    ia = ib = ic = 0
    for i in range(DEPTH):
        kind = i % N_MIXERS
        if kind == 0:
            j = ia
            ia += 1
            w_in = gla_w_in[j].astype(BF16)
            w_a1 = jnp.pad(gla_w_a1[j], ((0, 0), (0, V7X_LANES - GLA_GATE_RANK))).astype(BF16)
            w_a2 = jnp.pad(gla_w_a2[j], ((0, V7X_LANES - GLA_GATE_RANK), (0, 0))).astype(BF16)
            w_out = gla_w_out[j].astype(BF16)
            for si, (batch, seq) in enumerate(streams):
                proj, gate_lr = _norm_proj(xs[si], norm_w[i], w_in, w_a1)
                s0 = None if si == 0 else state_gla[j]
                h, s_new = _gla_mix(proj, gate_lr, w_a2, gla_b_a[j], gla_norm_w[j], s0, batch, seq)
                xs[si] = out_proj(h, xs[si], w_out, i)
                gla_states[si].append(s_new)
        elif kind == 1:
            j = ib
            ib += 1
            w_in = hgrn_w_in[j].astype(BF16)
            w_out = hgrn_w_out[j].astype(BF16)
            for si, (batch, seq) in enumerate(streams):
                proj = _norm_proj(xs[si], norm_w[i], w_in)
                s0 = None if si == 0 else state_hgrn[j]
                u, ssq, s_new = _hgrn_mix(proj, hgrn_lower_bounds, i, hgrn_norm_w[j], s0, batch, seq)
                xs[si] = out_proj(u, xs[si], w_out, i, ssq)
                hgrn_states[si].append(s_new)
        else:
            j = ic
            ic += 1
            lam_init = 0.8 - 0.6 * math.exp(-0.3 * i)
            w_in = diff_w_in[j].astype(BF16)
            w_out = diff_w_out[j].astype(BF16)
            for si, (batch, seq) in enumerate(streams):
                qkv, k_new, v_new, z = _diff_proj(xs[si], norm_w[i], w_in)
                if si == 0:
                    h = _diff_prompt_attn(qkv, z, diff_lambda[j], diff_subln_w[j], lam_init, batch, seq)
                else:
                    ck = cache_k[j].reshape(batch * past, 2 * DIFF_HEADS * DIFF_HD)
                    cv = cache_v[j].reshape(batch * past, 2 * DIFF_HEADS * DIFF_HD)
                    h = _diff_sample_attn(qkv, z, ck, cv, diff_lambda[j], diff_subln_w[j], lam_init,
                                          batch, seq, past)
                xs[si] = out_proj(h, xs[si], w_out, i)
                k_rows[si].append(k_new.reshape(batch, seq, 2 * DIFF_HEADS, DIFF_HD))
                v_rows[si].append(v_new.reshape(batch, seq, DIFF_HEADS, 2 * DIFF_HD))

    return (xs[0].reshape(bp, tp, d), xs[1].reshape(bs, ts, d),
            jnp.stack(gla_states[0]), jnp.stack(gla_states[1]),
            jnp.stack(hgrn_states[0]), jnp.stack(hgrn_states[1]),
            jnp.stack(k_rows[0]), jnp.stack(v_rows[0]),
            jnp.stack(k_rows[1]), jnp.stack(v_rows[1]))
```

```python
import functools
import math

import numpy as np
import jax
import jax.numpy as jnp
from jax import lax
from jax.experimental import pallas as pl
from jax.experimental.pallas import tpu as pltpu

D_MODEL = 2048
DEPTH = 4
CHUNK = 64
N_MIXERS = 3
D_INNER = D_MODEL
NORM_EPS = 1e-6
GLA_HEADS = 4
GLA_KD = D_INNER // 2
GLA_DK = GLA_KD // GLA_HEADS
GLA_DV = D_INNER // GLA_HEADS
GLA_GATE_RANK = 16
GLA_GATE_NORMALIZER = 16.0
HGRN_HEADS = 16
HGRN_DK = 128
HGRN_DV = 128
DIFF_HD = 128
DIFF_HEADS = 8

LOG2_E = math.log2(math.e)
DIFF_Q_SCALE = (DIFF_HD ** -0.5) * LOG2_E

F32 = jnp.float32
BF16 = jnp.bfloat16

V7X_LANES = 128
V7X_VMEM_BYTES = 64 * 1024 * 1024
VMEM_LIMIT_BYTES = (V7X_VMEM_BYTES * 3) // 4
VMEM_LIMIT_WIDE_BYTES = (V7X_VMEM_BYTES * 7) // 8

PROJ_TM = 1024
PROJ_TN = 1024
IN_TN = 2048
DIFF_TN = 1024
OUT_TM = 512
REC_TOKENS = 256
GLA_HEADS_PER_STEP = 4
HGRN_HEADS_PER_STEP = 4
ATT_BQ = 512
ATT_BK = 512
ATT_ROWS = 128


def _params(sem, vmem_limit=VMEM_LIMIT_BYTES):
    return pltpu.CompilerParams(dimension_semantics=sem, vmem_limit_bytes=vmem_limit)


def _dot(a, b):
    return jnp.dot(a, b, preferred_element_type=F32)


def _dot_nt(a, b):
    return lax.dot_general(a, b, (((1,), (1,)), ((), ())), preferred_element_type=F32)


def _dot_tn(a, b):
    return lax.dot_general(a, b, (((0,), (0,)), ((), ())), preferred_element_type=F32)


def _sigmoid(x):
    return 1.0 / (1.0 + jnp.exp(-x))


def _silu(x):
    return x * _sigmoid(x)


def _rms(x, w):
    ms = jnp.mean(x * x, axis=-1, keepdims=True)
    return x * lax.rsqrt(ms + NORM_EPS) * w


def _norm_proj_kernel(*refs, has_aux):
    if has_aux:
        x_ref, nw_ref, w_ref, aw_ref, o_ref, ao_ref, h_scr = refs
    else:
        x_ref, nw_ref, w_ref, o_ref, h_scr = refs

    @pl.when(pl.program_id(1) == 0)
    def _():
        h_scr[...] = _rms(x_ref[...], nw_ref[...]).astype(BF16)
        if has_aux:
            ao_ref[...] = _dot(h_scr[...], aw_ref[...])

    o_ref[...] = _dot(h_scr[...], w_ref[0])


def _norm_proj(x, nw, w_stack, layer, aux_w_bf=None):
    m, d = x.shape
    n = w_stack.shape[2]
    tm = min(PROJ_TM, m)
    tn = PROJ_TN
    in_specs = [
        pl.BlockSpec((tm, d), lambda i, j: (i, 0)),
        pl.BlockSpec((1, d), lambda i, j: (0, 0)),
        pl.BlockSpec((1, d, tn), lambda i, j: (layer, 0, j)),
    ]
    out_shape = [jax.ShapeDtypeStruct((m, n), F32)]
    out_specs = [pl.BlockSpec((tm, tn), lambda i, j: (i, j))]
    args = [x, nw.reshape(1, d), w_stack]
    if aux_w_bf is not None:
        na = aux_w_bf.shape[1]
        in_specs.append(pl.BlockSpec((d, na), lambda i, j: (0, 0)))
        out_shape.append(jax.ShapeDtypeStruct((m, na), F32))
        out_specs.append(pl.BlockSpec((tm, na), lambda i, j: (i, 0)))
        args.append(aux_w_bf)
    outs = pl.pallas_call(
        functools.partial(_norm_proj_kernel, has_aux=aux_w_bf is not None),
        grid=(m // tm, n // tn),
        in_specs=in_specs,
        out_specs=out_specs,
        out_shape=out_shape,
        scratch_shapes=[pltpu.VMEM((tm, d), BF16)],
        compiler_params=_params(("parallel", "arbitrary")),
        name="norm_proj",
    )(*args)
    return outs if aux_w_bf is not None else outs[0]


def _in_proj_kernel(*refs, has_aux):
    if has_aux:
        h_ref, w_ref, aw_ref, o_ref, ao_ref = refs

        @pl.when(pl.program_id(1) == 0)
        def _():
            ao_ref[...] = _dot(h_ref[...], aw_ref[...])
    else:
        h_ref, w_ref, o_ref = refs
    o_ref[...] = _dot(h_ref[...], w_ref[0])


def _in_proj(hn_bf, w_stack, layer, aux_w_bf=None):
    m, d = hn_bf.shape
    n = w_stack.shape[2]
    tm = min(PROJ_TM, m)
    tn = IN_TN
    in_specs = [
        pl.BlockSpec((tm, d), lambda i, j: (i, 0)),
        pl.BlockSpec((1, d, tn), lambda i, j: (layer, 0, j)),
    ]
    out_shape = [jax.ShapeDtypeStruct((m, n), F32)]
    out_specs = [pl.BlockSpec((tm, tn), lambda i, j: (i, j))]
    args = [hn_bf, w_stack]
    if aux_w_bf is not None:
        na = aux_w_bf.shape[1]
        in_specs.append(pl.BlockSpec((d, na), lambda i, j: (0, 0)))
        out_shape.append(jax.ShapeDtypeStruct((m, na), F32))
        out_specs.append(pl.BlockSpec((tm, na), lambda i, j: (i, 0)))
        args.append(aux_w_bf)
    outs = pl.pallas_call(
        functools.partial(_in_proj_kernel, has_aux=aux_w_bf is not None),
        grid=(m // tm, n // tn),
        in_specs=in_specs,
        out_specs=out_specs,
        out_shape=out_shape,
        compiler_params=_params(("parallel", "arbitrary")),
        name="in_proj",
    )(*args)
    return outs if aux_w_bf is not None else outs[0]


def _diff_proj_kernel(h_ref, w_ref, pm_ref, k_ref, v_ref, z_ref, *, group_tiles):
    j = pl.program_id(1)
    acc = _dot(h_ref[...], w_ref[0])
    pw = pm_ref.shape[2]

    def write_pairs(val):
        for p in range(pm_ref.shape[0]):
            pm_ref[p] = val[:, p * pw:(p + 1) * pw].astype(BF16)

    @pl.when(j < group_tiles)
    def _():
        write_pairs(acc * DIFF_Q_SCALE)

    @pl.when((j >= group_tiles) & (j < 2 * group_tiles))
    def _():
        write_pairs(acc)
        k_ref[...] = acc

    @pl.when((j >= 2 * group_tiles) & (j < 3 * group_tiles))
    def _():
        write_pairs(acc)
        v_ref[...] = acc

    @pl.when(j >= 3 * group_tiles)
    def _():
        z_ref[...] = acc


def _diff_proj(hn_bf, w_stack, layer):
    m, d = hn_bf.shape
    tm = min(PROJ_TM, m)
    tn = DIFF_TN
    pw = 2 * DIFF_HD
    g = D_INNER // tn
    ppt = tn // pw

    def rows_spec(group):
        return pl.BlockSpec((tm, tn), lambda i, j: (i, jnp.clip(j - group * g, 0, g - 1)))

    return pl.pallas_call(
        functools.partial(_diff_proj_kernel, group_tiles=g),
        grid=(m // tm, 4 * g),
        in_specs=[
            pl.BlockSpec((tm, d), lambda i, j: (i, 0)),
            pl.BlockSpec((1, d, tn), lambda i, j: (layer, 0, j)),
        ],
        out_specs=[
            pl.BlockSpec((ppt, tm, pw), lambda i, j: (jnp.minimum(j, 3 * g - 1), i, 0)),
            rows_spec(1), rows_spec(2), rows_spec(3),
        ],
        out_shape=[
            jax.ShapeDtypeStruct((3 * DIFF_HEADS, m, pw), BF16),
            jax.ShapeDtypeStruct((m, D_INNER), F32),
            jax.ShapeDtypeStruct((m, D_INNER), F32),
            jax.ShapeDtypeStruct((m, D_INNER), F32),
        ],
        compiler_params=_params(("parallel", "arbitrary"), VMEM_LIMIT_WIDE_BYTES),
        name="diff_proj",
    )(hn_bf, w_stack)


def _row_rms_scale(ssq_ref):
    ssq = ssq_ref[0]
    for gi in range(1, ssq_ref.shape[0]):
        ssq = ssq + ssq_ref[gi]
    return lax.rsqrt(ssq * (1.0 / D_INNER) + NORM_EPS)


def _out_proj_kernel(*refs, has_ssq, emit_x):
    h_ref = refs[0]
    ssq_ref = refs[1] if has_ssq else None
    res_ref, w_ref, nw_ref = refs[1 + has_ssq:4 + has_ssq]
    outs = refs[4 + has_ssq:]
    acc = _dot(h_ref[...], w_ref[0])
    if has_ssq:
        acc = acc * _row_rms_scale(ssq_ref)
    x_new = res_ref[...] + acc
    if emit_x:
        outs[0][...] = x_new
    outs[-1][...] = _rms(x_new, nw_ref[...]).astype(outs[-1].dtype)


def _out_proj(h_bf, res, w_stack, layer, norm_w, last, ssq=None):
    m, d = h_bf.shape
    n = w_stack.shape[2]
    tm = min(OUT_TM, m)
    in_specs = [pl.BlockSpec((tm, d), lambda i: (i, 0))]
    args = [h_bf]
    if ssq is not None:
        in_specs.append(pl.BlockSpec((ssq.shape[0], tm, 1), lambda i: (0, i, 0)))
        args.append(ssq)
    in_specs += [
        pl.BlockSpec((tm, n), lambda i: (i, 0)),
        pl.BlockSpec((1, d, n), lambda i: (layer, 0, 0)),
        pl.BlockSpec((1, n), lambda i: (0, 0)),
    ]
    args += [res, w_stack, norm_w.reshape(1, n)]
    out_shape = [jax.ShapeDtypeStruct((m, n), F32 if last else BF16)]
    if not last:
        out_shape.insert(0, jax.ShapeDtypeStruct((m, n), F32))
    outs = pl.pallas_call(
        functools.partial(_out_proj_kernel, has_ssq=ssq is not None, emit_x=not last),
        grid=(m // tm,),
        in_specs=in_specs,
        out_specs=[pl.BlockSpec((tm, n), lambda i: (i, 0)) for _ in out_shape],
        out_shape=out_shape,
        compiler_params=_params(("parallel",)),
        name="out_proj",
    )(*args)
    return (None, outs[0]) if last else (outs[0], outs[1])


def _block_constants(tb, chunk):
    row = np.arange(tb)[:, None]
    col = np.arange(tb)[None, :]
    rc, cc = row // chunk, col // chunk
    same = (rc == cc) & (row >= col)
    dist = np.where(same, 0, np.where(rc > cc, rc - cc, -1)).astype(np.int32)
    return jnp.asarray(same.astype(np.float32), BF16), jnp.asarray(dist)


def _chunk_cumsum(g, tri_bf):
    g_hi = g.astype(BF16)
    g_lo = (g - g_hi.astype(F32)).astype(BF16)
    return _dot(tri_bf, g_hi) + _dot(tri_bf, g_lo)


def _per_chunk(rows, chunk):
    parts = [jnp.broadcast_to(r, (chunk, r.shape[1])) for r in rows]
    return parts[0] if len(parts) == 1 else jnp.concatenate(parts, axis=0)


def _sum_rows(rows, width):
    out = jnp.zeros((1, width), F32)
    for r in rows:
        out = out + r
    return out


def _block_decay_column(b, chunk):
    tb, w = b.shape
    total = _sum_rows([b[(c + 1) * chunk - 1:(c + 1) * chunk, :] for c in range(tb // chunk)], w)
    return jnp.transpose(jnp.broadcast_to(jnp.exp2(total), (V7X_LANES, w)))[:, 0:1]


def _block_recurrence(q, k, b, v_bf, s0, dist, dec_col, chunk, q_log2_scale):
    w = q.shape[1]
    n = q.shape[0] // chunk
    mids = [b[c * chunk + chunk // 2 - 1:c * chunk + chunk // 2, :] for c in range(n)]
    lasts = [b[(c + 1) * chunk - 1:(c + 1) * chunk, :] for c in range(n)]
    q_mid = q * jnp.exp2(b - _per_chunk([m - q_log2_scale for m in mids], chunk))
    k_mid = k * jnp.exp2(_per_chunk(mids, chunk) - b)
    q_in = q_mid * _per_chunk([jnp.exp2(m) for m in mids], chunk)
    k_out = k_mid * _per_chunk([jnp.exp2(l - m) for l, m in zip(lasts, mids)], chunk)
    attn = jnp.where(dist == 0, _dot_nt(q_mid.astype(BF16), k_mid.astype(BF16)), 0.0)
    q_in_bf = q_in.astype(BF16)
    for d in range(1, n):
        gaps = [jnp.exp2(_sum_rows(lasts[j + 1:j + d], w)) if j + d < n else jnp.zeros((1, w), F32)
                for j in range(n)]
        k_d = k_out if d == 1 else k_out * _per_chunk(gaps, chunk)
        attn = jnp.where(dist == d, _dot_nt(q_in_bf, k_d.astype(BF16)), attn)
    head = _per_chunk([jnp.exp2(_sum_rows(lasts[:c], w)) for c in range(n)], chunk)
    tail = _per_chunk([jnp.exp2(_sum_rows(lasts[c + 1:], w)) for c in range(n)], chunk)
    o = _dot((q_in * head).astype(BF16), s0.astype(BF16)) + _dot(attn.astype(BF16), v_bf)
    s_end = dec_col * s0 + _dot_tn((k_out * tail).astype(BF16), v_bf)
    return o, s_end


def _log_sigmoid(x):
    return jnp.minimum(x, 0.0) - jnp.log(1.0 + jnp.exp(-jnp.abs(x)))


def _gla_kernel(*refs, chunk, zero_init, heads):
    if zero_init:
        (q_ref, k_ref, v_ref, z_ref, ga_ref, wa2_ref, ba_ref, nw_ref, tri_ref, dist_ref,
         h_ref, sout_ref, s_scr) = refs
    else:
        (q_ref, k_ref, v_ref, z_ref, ga_ref, wa2_ref, ba_ref, nw_ref, tri_ref, dist_ref, s0_ref,
         h_ref, sout_ref, s_scr) = refs
    t = pl.program_id(2)

    @pl.when(t == 0)
    def _():
        if zero_init:
            s_scr[...] = jnp.zeros_like(s_scr)
        else:
            s_scr[...] = s0_ref[0, 0]

    gate_in = _dot(ga_ref[...].astype(BF16), wa2_ref[...]) + ba_ref[...]
    g = _log_sigmoid(gate_in) * (LOG2_E / GLA_GATE_NORMALIZER)
    b = _chunk_cumsum(g, tri_ref[...])
    dec_col = _block_decay_column(b, chunk)
    dist = dist_ref[...]
    nw = nw_ref[...]
    for hi in range(heads):
        kc = slice(hi * GLA_DK, (hi + 1) * GLA_DK)
        vc = slice(hi * GLA_DV, (hi + 1) * GLA_DV)
        o, s = _block_recurrence(q_ref[:, kc], k_ref[:, kc], b[:, kc], v_ref[:, vc].astype(BF16),
                                 s_scr[hi], dist, dec_col[kc, :], chunk, math.log2(GLA_DK ** -0.5))
        s_scr[hi] = s
        h_ref[:, vc] = (_rms(o, nw) * _silu(z_ref[:, vc])).astype(BF16)

    @pl.when(t == pl.num_programs(2) - 1)
    def _():
        sout_ref[0] = s_scr[...]


def _gla_mix(proj, gate_lr, wa2_bf, b_a, norm_w, states, layer, batch, seq):
    chunk = min(CHUNK, seq)
    tb = min(REC_TOKENS, seq)
    nt = seq // tb
    hb = GLA_HEADS_PER_STEP
    kb = hb * GLA_DK
    vb = hb * GLA_DV
    k_off = GLA_KD // kb
    v_off = (2 * GLA_KD) // vb
    z_off = (2 * GLA_KD + D_INNER) // vb
    zero_init = states is None
    in_specs = [
        pl.BlockSpec((tb, kb), lambda b, h, t: (b * nt + t, h)),
        pl.BlockSpec((tb, kb), lambda b, h, t: (b * nt + t, k_off + h)),
        pl.BlockSpec((tb, vb), lambda b, h, t: (b * nt + t, v_off + h)),
        pl.BlockSpec((tb, vb), lambda b, h, t: (b * nt + t, z_off + h)),
        pl.BlockSpec((tb, V7X_LANES), lambda b, h, t: (b * nt + t, 0)),
        pl.BlockSpec((V7X_LANES, kb), lambda b, h, t: (0, h)),
        pl.BlockSpec((1, kb), lambda b, h, t: (0, h)),
        pl.BlockSpec((1, GLA_DV), lambda b, h, t: (0, 0)),
        pl.BlockSpec((tb, tb), lambda b, h, t: (0, 0)),
        pl.BlockSpec((tb, tb), lambda b, h, t: (0, 0)),
    ]
    args = [proj, proj, proj, proj, gate_lr, wa2_bf, b_a.reshape(1, GLA_KD), norm_w.reshape(1, GLA_DV),
            *_block_constants(tb, chunk)]
    if not zero_init:
        in_specs.append(pl.BlockSpec((1, 1, hb, GLA_DK, GLA_DV), lambda b, h, t: (layer, b, h, 0, 0)))
        args.append(states)
    h, s_out = pl.pallas_call(
        functools.partial(_gla_kernel, chunk=chunk, zero_init=zero_init, heads=hb),
        grid=(batch, GLA_HEADS // hb, nt),
        in_specs=in_specs,
        out_specs=[
            pl.BlockSpec((tb, vb), lambda b, h, t: (b * nt + t, h)),
            pl.BlockSpec((1, hb, GLA_DK, GLA_DV), lambda b, h, t: (b, h, 0, 0)),
        ],
        out_shape=[
            jax.ShapeDtypeStruct((batch * seq, D_INNER), BF16),
            jax.ShapeDtypeStruct((batch, GLA_HEADS, GLA_DK, GLA_DV), F32),
        ],
        scratch_shapes=[pltpu.VMEM((hb, GLA_DK, GLA_DV), F32)],
        compiler_params=_params(("parallel", "parallel", "arbitrary")),
        name="gla_mix",
    )(*args)
    return h, s_out


def _hgrn_kernel(*refs, chunk, zero_init, layer, heads):
    if zero_init:
        (q_ref, f_ref, i_ref, z_ref, lbp_ref, nw_ref, tri_ref, dist_ref,
         u_ref, ssq_ref, sout_ref, s_scr) = refs
    else:
        (q_ref, f_ref, i_ref, z_ref, lbp_ref, nw_ref, tri_ref, dist_ref, s0_ref,
         u_ref, ssq_ref, sout_ref, s_scr) = refs
    t = pl.program_id(2)

    @pl.when(t == 0)
    def _():
        if zero_init:
            s_scr[...] = jnp.zeros_like(s_scr)
        else:
            s_scr[...] = s0_ref[0, 0]

    lbp = lbp_ref[...]
    lbe = jnp.exp(lbp - jnp.max(lbp, axis=0, keepdims=True))
    lbs = lbe / jnp.sum(lbe, axis=0, keepdims=True)
    lb = jnp.zeros_like(lbs[0:1])
    for r in range(1, layer + 1):
        lb = lb + lbs[r:r + 1]

    q = _silu(q_ref[...])
    fgate = lb + (1.0 - lb) * _sigmoid(f_ref[...])
    k = 1.0 - fgate
    b = _chunk_cumsum(jnp.log(fgate) * LOG2_E, tri_ref[...])
    dec_col = _block_decay_column(b, chunk)
    v_bf = i_ref[...].astype(BF16)
    dist = dist_ref[...]
    ssq = jnp.zeros((q.shape[0], 1), F32)
    for hi in range(heads):
        cols = slice(hi * HGRN_DK, (hi + 1) * HGRN_DK)
        o, s = _block_recurrence(q[:, cols], k[:, cols], b[:, cols], v_bf[:, cols], s_scr[hi],
                                 dist, dec_col[cols, :], chunk, math.log2(HGRN_DK ** -0.5))
        s_scr[hi] = s
        ssq = ssq + jnp.sum(o * o, axis=-1, keepdims=True)
        u_ref[:, cols] = (o * nw_ref[:, cols] * _silu(z_ref[:, cols])).astype(BF16)
    ssq_ref[0] = ssq

    @pl.when(t == pl.num_programs(2) - 1)
    def _():
        sout_ref[0] = s_scr[...]


def _hgrn_mix(proj, lower_bounds, layer, norm_w, states, state_layer, batch, seq):
    chunk = min(CHUNK, seq)
    tb = min(REC_TOKENS, seq)
    nt = seq // tb
    hb = HGRN_HEADS_PER_STEP
    wb = hb * HGRN_DK
    ngroups = HGRN_HEADS // hb
    col_groups = D_INNER // wb
    zero_init = states is None
    in_specs = [
        pl.BlockSpec((tb, wb), lambda b, h, t: (b * nt + t, h)),
        pl.BlockSpec((tb, wb), lambda b, h, t: (b * nt + t, col_groups + h)),
        pl.BlockSpec((tb, wb), lambda b, h, t: (b * nt + t, 2 * col_groups + h)),
        pl.BlockSpec((tb, wb), lambda b, h, t: (b * nt + t, 3 * col_groups + h)),
        pl.BlockSpec((DEPTH, wb), lambda b, h, t: (0, h)),
        pl.BlockSpec((1, wb), lambda b, h, t: (0, h)),
        pl.BlockSpec((tb, tb), lambda b, h, t: (0, 0)),
        pl.BlockSpec((tb, tb), lambda b, h, t: (0, 0)),
    ]
    args = [proj, proj, proj, proj, lower_bounds, norm_w.reshape(1, D_INNER), *_block_constants(tb, chunk)]
    if not zero_init:
        in_specs.append(pl.BlockSpec((1, 1, hb, HGRN_DK, HGRN_DV), lambda b, h, t: (state_layer, b, h, 0, 0)))
        args.append(states)
    u, ssq, s_out = pl.pallas_call(
        functools.partial(_hgrn_kernel, chunk=chunk, zero_init=zero_init, layer=layer, heads=hb),
        grid=(batch, ngroups, nt),
        in_specs=in_specs,
        out_specs=[
            pl.BlockSpec((tb, wb), lambda b, h, t: (b * nt + t, h)),
            pl.BlockSpec((1, tb, 1), lambda b, h, t: (h, b * nt + t, 0)),
            pl.BlockSpec((1, hb, HGRN_DK, HGRN_DV), lambda b, h, t: (b, h, 0, 0)),
        ],
        out_shape=[
            jax.ShapeDtypeStruct((batch * seq, D_INNER), BF16),
            jax.ShapeDtypeStruct((ngroups, batch * seq, 1), F32),
            jax.ShapeDtypeStruct((batch, HGRN_HEADS, HGRN_DK, HGRN_DV), F32),
        ],
        scratch_shapes=[pltpu.VMEM((hb, HGRN_DK, HGRN_DV), F32)],
        compiler_params=_params(("parallel", "parallel", "arbitrary")),
        name="hgrn_mix",
    )(*args)
    return u, ssq, s_out


def _diff_lambda(lam_ref, lam_init):
    lp = lam_ref[...]
    a = jnp.sum(lp[0:1] * lp[1:2], axis=-1, keepdims=True)
    b = jnp.sum(lp[2:3] * lp[3:4], axis=-1, keepdims=True)
    return jnp.exp(a) - jnp.exp(b) + lam_init


def _diff_finish(o1, o2, lam_ref, sw_ref, z, lam_init):
    o = o1 - _diff_lambda(lam_ref, lam_init) * o2
    o = _rms(o, sw_ref[...]) * (1.0 - lam_init)
    return (o * _silu(z)).astype(BF16)


KIND_FIRST, KIND_PLAIN, KIND_MASKED, KIND_DRAIN = 0, 1, 3, 5


def _diff_prompt_kernel(qa_tab, ka_tab, qb_tab, kb_tab, kind_tab, last_tab,
                        q_ref, k_ref, v_ref, z_ref, lam_ref, sw_ref, h_ref, *scratch,
                        bq, bk, rows_per_tile, lam_init):
    per_head = len(scratch) // 2
    heads = []
    for h in range(2):
        s_scr, p0, p1, m_scr, l_scr, a0, a1, acc_scr = scratch[per_head * h:per_head * (h + 1)]
        heads.append((s_scr, (p0, p1), m_scr, l_scr, (a0, a1), acc_scr))
    p = pl.program_id(2)
    qi = qa_tab[p]
    ki = ka_tab[p]
    par_a = qi % 2
    par_b = qb_tab[p] % 2
    first = ki == 0
    reps = (1, 2 * DIFF_HD // V7X_LANES)

    def stage_a(masked, slot):
        for h, (s_scr, p_scr, m_scr, l_scr, a_scr, acc_scr) in enumerate(heads):
            cols = slice(h * DIFF_HD, (h + 1) * DIFF_HD)
            s_scr[...] = _dot_nt(q_ref[0, :, cols], k_ref[0, :, cols])
        for s_scr, p_scr, m_scr, l_scr, a_scr, acc_scr in heads:
            for r in range(bq // rows_per_tile):
                rows = pl.ds(r * rows_per_tile, rows_per_tile)
                s = s_scr[rows, :]
                if masked:
                    shape = (rows_per_tile, bk)
                    q_chunk = (qi * bq + r * rows_per_tile + lax.broadcasted_iota(jnp.int32, shape, 0)) // CHUNK
                    k_chunk = (ki * bk + lax.broadcasted_iota(jnp.int32, shape, 1)) // CHUNK
                    s = jnp.where(k_chunk <= q_chunk, s, -jnp.inf)
                m_prev = jnp.where(first, -jnp.inf, m_scr[rows, :])
                m_new = jnp.maximum(m_prev, jnp.max(s, axis=-1, keepdims=True))
                alpha = jnp.exp2(m_prev - m_new)
                pr = jnp.exp2(s - jnp.tile(m_new, (1, bk // V7X_LANES)))
                l_scr[par_a, rows, :] = alpha * l_scr[par_a, rows, :] + jnp.sum(pr, axis=-1, keepdims=True)
                m_scr[rows, :] = m_new
                a_scr[slot][rows, :] = alpha
                p_scr[slot][rows, :] = pr.astype(BF16)

    def stage_b(slot):
        for s_scr, p_scr, m_scr, l_scr, a_scr, acc_scr in heads:
            alpha = jnp.tile(a_scr[slot][...], reps)
            acc_scr[...] = alpha * acc_scr[...] + _dot(p_scr[slot][...], v_ref[0])

    kind = kind_tab[p]

    @pl.when(kind == KIND_FIRST)
    def _():
        for s_scr, p_scr, m_scr, l_scr, a_scr, acc_scr in heads:
            m_scr[...] = jnp.full_like(m_scr, -jnp.inf)
            l_scr[...] = jnp.zeros_like(l_scr)
            acc_scr[...] = jnp.zeros_like(acc_scr)
        stage_a(True, 0)

    for slot in range(2):
        @pl.when(kind == KIND_PLAIN + slot)
        def _():
            stage_a(False, slot)
            stage_b(1 - slot)

        @pl.when(kind == KIND_MASKED + slot)
        def _():
            stage_a(True, slot)
            stage_b(1 - slot)

        @pl.when(kind == KIND_DRAIN + slot)
        def _():
            stage_b(1 - slot)

    @pl.when(last_tab[p] == 1)
    def _():
        o1, o2 = [acc_scr[...] * jnp.tile(1.0 / l_scr[par_b], reps)
                  for _, _, _, l_scr, _, acc_scr in heads]
        h_ref[...] = _diff_finish(o1, o2, lam_ref, sw_ref, z_ref[...], lam_init)


def _diff_prompt_tables(seq, bq, bk):
    pairs = []
    for qi in range(seq // bq):
        q_lo, q_hi = qi * bq, (qi + 1) * bq
        nk = -(-q_hi // bk)
        for ki in range(nk):
            fully_visible = (ki + 1) * bk <= (q_lo // CHUNK + 1) * CHUNK
            pairs.append((qi, ki, not fully_visible, ki == nk - 1))
    assert pairs[0][2], "the first pair sits on the diagonal"
    n = len(pairs)
    qa = [pairs[min(p, n - 1)][0] for p in range(n + 1)]
    ka = [pairs[min(p, n - 1)][1] for p in range(n + 1)]
    qb = [pairs[max(p - 1, 0)][0] for p in range(n + 1)]
    kb = [pairs[max(p - 1, 0)][1] for p in range(n + 1)]
    kind = ([KIND_FIRST] + [(KIND_MASKED if pairs[p][2] else KIND_PLAIN) + p % 2 for p in range(1, n)]
            + [KIND_DRAIN + n % 2])
    last = [0] + [1 if pairs[p - 1][3] else 0 for p in range(1, n + 1)]
    return [jnp.asarray(np.asarray(t, np.int32)) for t in (qa, ka, qb, kb, kind, last)]


def _diff_prompt_attn(qkv, z, lam_p, subln_w, lam_init, batch, seq):
    bq = min(ATT_BQ, seq)
    bk = min(ATT_BK, seq)
    nq = seq // bq
    nkb = seq // bk
    pw = 2 * DIFF_HD
    tabs = _diff_prompt_tables(seq, bq, bk)
    n_steps = int(tabs[0].shape[0])

    def spec(shape, index):
        return pl.BlockSpec(shape, lambda b, h, p, qa, ka, qb, kb, kind, last: index(b, h, p, qa, ka, qb, kb))

    grid_spec = pltpu.PrefetchScalarGridSpec(
        num_scalar_prefetch=6,
        grid=(batch, DIFF_HEADS, n_steps),
        in_specs=[
            spec((1, bq, pw), lambda b, h, p, qa, ka, qb, kb: (h, b * nq + qa[p], 0)),
            spec((1, bk, pw), lambda b, h, p, qa, ka, qb, kb: (DIFF_HEADS + h, b * nkb + ka[p], 0)),
            spec((1, bk, pw), lambda b, h, p, qa, ka, qb, kb: (2 * DIFF_HEADS + h, b * nkb + kb[p], 0)),
            spec((bq, pw), lambda b, h, p, qa, ka, qb, kb: (b * nq + qb[p], h)),
            spec((4, DIFF_HD), lambda b, h, p, qa, ka, qb, kb: (0, 0)),
            spec((1, pw), lambda b, h, p, qa, ka, qb, kb: (0, 0)),
        ],
        out_specs=spec((bq, pw), lambda b, h, p, qa, ka, qb, kb: (b * nq + qb[p], h)),
        scratch_shapes=2 * [
            pltpu.VMEM((bq, bk), F32),
            pltpu.VMEM((bq, bk), BF16),
            pltpu.VMEM((bq, bk), BF16),
            pltpu.VMEM((bq, V7X_LANES), F32),
            pltpu.VMEM((2, bq, V7X_LANES), F32),
            pltpu.VMEM((bq, V7X_LANES), F32),
            pltpu.VMEM((bq, V7X_LANES), F32),
            pltpu.VMEM((bq, pw), F32),
        ],
    )
    return pl.pallas_call(
        functools.partial(_diff_prompt_kernel, bq=bq, bk=bk, rows_per_tile=min(ATT_ROWS, bq),
                          lam_init=lam_init),
        grid_spec=grid_spec,
        out_shape=jax.ShapeDtypeStruct((batch * seq, D_INNER), BF16),
        compiler_params=_params(("parallel", "parallel", "arbitrary")),
        name="diff_prompt_attn",
    )(*tabs, qkv, qkv, qkv, z, lam_p, subln_w.reshape(1, pw))


def _diff_sample_kernel(qkv_ref, z_ref, kc_ref, vc_ref, lam_ref, sw_ref, h_ref, *, lam_init):
    pw = 2 * DIFF_HD
    for pair in range(DIFF_HEADS):
        pcols = slice(pair * pw, (pair + 1) * pw)
        vc_bf = vc_ref[:, pcols].astype(BF16)
        vn_bf = qkv_ref[2 * DIFF_HEADS + pair]
        outs = []
        for h in range(2):
            cols = slice(h * DIFF_HD, (h + 1) * DIFF_HD)
            ccols = slice(pair * pw + h * DIFF_HD, pair * pw + (h + 1) * DIFF_HD)
            q_bf = qkv_ref[pair, :, cols]
            s_c = _dot_nt(q_bf, kc_ref[:, ccols].astype(BF16))
            s_n = _dot_nt(q_bf, qkv_ref[DIFF_HEADS + pair, :, cols])
            m = jnp.maximum(jnp.max(s_c, axis=-1, keepdims=True), jnp.max(s_n, axis=-1, keepdims=True))
            p_c = jnp.exp2(s_c - m)
            p_n = jnp.exp2(s_n - m)
            l = jnp.sum(p_c, axis=-1, keepdims=True) + jnp.sum(p_n, axis=-1, keepdims=True)
            acc = _dot(p_c.astype(BF16), vc_bf) + _dot(p_n.astype(BF16), vn_bf)
            outs.append(acc * (1.0 / l))
        h_ref[:, pcols] = _diff_finish(outs[0], outs[1], lam_ref, sw_ref, z_ref[:, pcols], lam_init)


def _diff_sample_attn(qkv, z, cache_k2, cache_v2, lam_p, subln_w, lam_init, batch, seq, past):
    pw = 2 * DIFF_HD
    return pl.pallas_call(
        functools.partial(_diff_sample_kernel, lam_init=lam_init),
        grid=(batch,),
        in_specs=[
            pl.BlockSpec((3 * DIFF_HEADS, seq, pw), lambda b: (0, b, 0)),
            pl.BlockSpec((seq, D_INNER), lambda b: (b, 0)),
            pl.BlockSpec((past, D_INNER), lambda b: (b, 0)),
            pl.BlockSpec((past, D_INNER), lambda b: (b, 0)),
            pl.BlockSpec((4, DIFF_HD), lambda b: (0, 0)),
            pl.BlockSpec((1, pw), lambda b: (0, 0)),
        ],
        out_specs=pl.BlockSpec((seq, D_INNER), lambda b: (b, 0)),
        out_shape=jax.ShapeDtypeStruct((batch * seq, D_INNER), BF16),
        compiler_params=_params(("parallel",)),
        name="diff_sample_attn",
    )(qkv, z, cache_k2, cache_v2, lam_p, subln_w.reshape(1, pw))


def kernel(x_prompt, x_sample, state_gla, state_hgrn, cache_k, cache_v, norm_w, final_norm_w,
           gla_w_in, gla_w_a1, gla_w_a2, gla_b_a, gla_norm_w, gla_w_out,
           hgrn_w_in, hgrn_lower_bounds, hgrn_norm_w, hgrn_w_out,
           diff_w_in, diff_lambda, diff_subln_w, diff_w_out):
    bp, tp, d = x_prompt.shape
    bs, ts, _ = x_sample.shape
    past = cache_k.shape[2]
    streams = [(bp, tp), (bs, ts)]
    xs = [x_prompt.reshape(bp * tp, d), x_sample.reshape(bs * ts, d)]

    def out_proj(h, res, w_stack, j, i, ssq=None):
        last = i == DEPTH - 1
        return _out_proj(h, res, w_stack, j, final_norm_w if last else norm_w[i + 1], last, ssq)

    gla_w_in_bf, gla_w_out_bf = gla_w_in.astype(BF16), gla_w_out.astype(BF16)
    hgrn_w_in_bf, hgrn_w_out_bf = hgrn_w_in.astype(BF16), hgrn_w_out.astype(BF16)
    diff_w_in_bf, diff_w_out_bf = diff_w_in.astype(BF16), diff_w_out.astype(BF16)

    hn = [None, None]
    gla_states = [[], []]
    hgrn_states = [[], []]
    k_rows = [[], []]
    v_rows = [[], []]
    ia = ib = ic = 0
    for i in range(DEPTH):
        kind = i % N_MIXERS
        if kind == 0:
            j = ia
            ia += 1
            w_a1 = jnp.pad(gla_w_a1[j], ((0, 0), (0, V7X_LANES - GLA_GATE_RANK))).astype(BF16)
            w_a2 = jnp.pad(gla_w_a2[j], ((0, V7X_LANES - GLA_GATE_RANK), (0, 0))).astype(BF16)
            for si, (batch, seq) in enumerate(streams):
                if i == 0:
                    proj, gate_lr = _norm_proj(xs[si], norm_w[i], gla_w_in_bf, j, w_a1)
                else:
                    proj, gate_lr = _in_proj(hn[si], gla_w_in_bf, j, w_a1)
                states = None if si == 0 else state_gla
                h, s_new = _gla_mix(proj, gate_lr, w_a2, gla_b_a[j], gla_norm_w[j], states, j, batch, seq)
                xs[si], hn[si] = out_proj(h, xs[si], gla_w_out_bf, j, i)
                gla_states[si].append(s_new)
        elif kind == 1:
            j = ib
            ib += 1
            for si, (batch, seq) in enumerate(streams):
                proj = _in_proj(hn[si], hgrn_w_in_bf, j)
                states = None if si == 0 else state_hgrn
                u, ssq, s_new = _hgrn_mix(proj, hgrn_lower_bounds, i, hgrn_norm_w[j], states, j, batch, seq)
                xs[si], hn[si] = out_proj(u, xs[si], hgrn_w_out_bf, j, i, ssq)
                hgrn_states[si].append(s_new)
        else:
            j = ic
            ic += 1
            lam_init = 0.8 - 0.6 * math.exp(-0.3 * i)
            for si, (batch, seq) in enumerate(streams):
                qkv, k_new, v_new, z = _diff_proj(hn[si], diff_w_in_bf, j)
                if si == 0:
                    h = _diff_prompt_attn(qkv, z, diff_lambda[j], diff_subln_w[j], lam_init, batch, seq)
                else:
                    ck = cache_k[j].reshape(batch * past, 2 * DIFF_HEADS * DIFF_HD)
                    cv = cache_v[j].reshape(batch * past, 2 * DIFF_HEADS * DIFF_HD)
                    h = _diff_sample_attn(qkv, z, ck, cv, diff_lambda[j], diff_subln_w[j], lam_init,
                                          batch, seq, past)
                xs[si], hn[si] = out_proj(h, xs[si], diff_w_out_bf, j, i)
                k_rows[si].append(k_new.reshape(batch, seq, 2 * DIFF_HEADS, DIFF_HD))
                v_rows[si].append(v_new.reshape(batch, seq, DIFF_HEADS, 2 * DIFF_HD))

    return (hn[0].reshape(bp, tp, d), hn[1].reshape(bs, ts, d),
            jnp.stack(gla_states[0]), jnp.stack(gla_states[1]),
            jnp.stack(hgrn_states[0]), jnp.stack(hgrn_states[1]),
            jnp.stack(k_rows[0]), jnp.stack(v_rows[0]),
            jnp.stack(k_rows[1]), jnp.stack(v_rows[1]))
```

```python
import functools
import math

import numpy as np
import jax
import jax.numpy as jnp
from jax import lax
from jax.experimental import pallas as pl
from jax.experimental.pallas import tpu as pltpu

D_MODEL = 2048
DEPTH = 4
CHUNK = 64
N_MIXERS = 3
D_INNER = D_MODEL
NORM_EPS = 1e-6
GLA_HEADS = 4
GLA_KD = D_INNER // 2
GLA_DK = GLA_KD // GLA_HEADS
GLA_DV = D_INNER // GLA_HEADS
GLA_GATE_RANK = 16
GLA_GATE_NORMALIZER = 16.0
HGRN_HEADS = 16
HGRN_DK = 128
HGRN_DV = 128
DIFF_HD = 128
DIFF_HEADS = 8

LOG2_E = math.log2(math.e)
DIFF_Q_SCALE = (DIFF_HD ** -0.5) * LOG2_E

F32 = jnp.float32
BF16 = jnp.bfloat16

V7X_LANES = 128
V7X_VMEM_BYTES = 64 * 1024 * 1024
VMEM_LIMIT_BYTES = (V7X_VMEM_BYTES * 3) // 4
VMEM_LIMIT_WIDE_BYTES = (V7X_VMEM_BYTES * 7) // 8

PROJ_TM = 1024
PROJ_TN = 1024
IN_TN = 2048
DIFF_TN = 1024
OUT_TM = 512
REC_TOKENS = 256
GLA_HEADS_PER_STEP = 4
HGRN_HEADS_PER_STEP = 16
ATT_BQ = 1024
ATT_BK = 512
ATT_ROWS = 128


def _params(sem, vmem_limit=VMEM_LIMIT_BYTES):
    return pltpu.CompilerParams(dimension_semantics=sem, vmem_limit_bytes=vmem_limit)


def _dot(a, b):
    return jnp.dot(a, b, preferred_element_type=F32)


def _dot_nt(a, b):
    return lax.dot_general(a, b, (((1,), (1,)), ((), ())), preferred_element_type=F32)


def _dot_tn(a, b):
    return lax.dot_general(a, b, (((0,), (0,)), ((), ())), preferred_element_type=F32)


def _sigmoid(x):
    return 1.0 / (1.0 + jnp.exp(-x))


def _silu(x):
    return x * _sigmoid(x)


def _rms(x, w):
    ms = jnp.mean(x * x, axis=-1, keepdims=True)
    return x * lax.rsqrt(ms + NORM_EPS) * w


def _norm_proj_kernel(*refs, has_aux):
    if has_aux:
        x_ref, nw_ref, w_ref, aw_ref, o_ref, ao_ref, h_scr = refs
    else:
        x_ref, nw_ref, w_ref, o_ref, h_scr = refs

    @pl.when(pl.program_id(1) == 0)
    def _():
        h_scr[...] = _rms(x_ref[...], nw_ref[...]).astype(BF16)
        if has_aux:
            ao_ref[...] = _dot(h_scr[...], aw_ref[...])

    o_ref[...] = _dot(h_scr[...], w_ref[0])


def _norm_proj(x, nw, w_stack, layer, aux_w_bf=None):
    m, d = x.shape
    n = w_stack.shape[2]
    tm = min(PROJ_TM, m)
    tn = PROJ_TN
    in_specs = [
        pl.BlockSpec((tm, d), lambda i, j: (i, 0)),
        pl.BlockSpec((1, d), lambda i, j: (0, 0)),
        pl.BlockSpec((1, d, tn), lambda i, j: (layer, 0, j)),
    ]
    out_shape = [jax.ShapeDtypeStruct((m, n), F32)]
    out_specs = [pl.BlockSpec((tm, tn), lambda i, j: (i, j))]
    args = [x, nw.reshape(1, d), w_stack]
    if aux_w_bf is not None:
        na = aux_w_bf.shape[1]
        in_specs.append(pl.BlockSpec((d, na), lambda i, j: (0, 0)))
        out_shape.append(jax.ShapeDtypeStruct((m, na), F32))
        out_specs.append(pl.BlockSpec((tm, na), lambda i, j: (i, 0)))
        args.append(aux_w_bf)
    outs = pl.pallas_call(
        functools.partial(_norm_proj_kernel, has_aux=aux_w_bf is not None),
        grid=(m // tm, n // tn),
        in_specs=in_specs,
        out_specs=out_specs,
        out_shape=out_shape,
        scratch_shapes=[pltpu.VMEM((tm, d), BF16)],
        compiler_params=_params(("parallel", "arbitrary")),
        name="norm_proj",
    )(*args)
    return outs if aux_w_bf is not None else outs[0]


def _in_proj_kernel(*refs, has_aux):
    if has_aux:
        h_ref, w_ref, aw_ref, o_ref, ao_ref = refs

        @pl.when(pl.program_id(1) == 0)
        def _():
            ao_ref[...] = _dot(h_ref[...], aw_ref[...])
    else:
        h_ref, w_ref, o_ref = refs
    o_ref[...] = _dot(h_ref[...], w_ref[0])


def _in_proj(hn_bf, w_stack, layer, aux_w_bf=None):
    m, d = hn_bf.shape
    n = w_stack.shape[2]
    tm = min(PROJ_TM, m)
    tn = IN_TN
    in_specs = [
        pl.BlockSpec((tm, d), lambda i, j: (i, 0)),
        pl.BlockSpec((1, d, tn), lambda i, j: (layer, 0, j)),
    ]
    out_shape = [jax.ShapeDtypeStruct((m, n), F32)]
    out_specs = [pl.BlockSpec((tm, tn), lambda i, j: (i, j))]
    args = [hn_bf, w_stack]
    if aux_w_bf is not None:
        na = aux_w_bf.shape[1]
        in_specs.append(pl.BlockSpec((d, na), lambda i, j: (0, 0)))
        out_shape.append(jax.ShapeDtypeStruct((m, na), F32))
        out_specs.append(pl.BlockSpec((tm, na), lambda i, j: (i, 0)))
        args.append(aux_w_bf)
    outs = pl.pallas_call(
        functools.partial(_in_proj_kernel, has_aux=aux_w_bf is not None),
        grid=(m // tm, n // tn),
        in_specs=in_specs,
        out_specs=out_specs,
        out_shape=out_shape,
        compiler_params=_params(("parallel", "arbitrary")),
        name="in_proj",
    )(*args)
    return outs if aux_w_bf is not None else outs[0]


def _diff_proj_kernel(h_ref, w_ref, pm_ref, k_ref, v_ref, z_ref, *, group_tiles):
    j = pl.program_id(1)
    acc = _dot(h_ref[...], w_ref[0])
    pw = pm_ref.shape[2]

    def write_pairs(val):
        for p in range(pm_ref.shape[0]):
            pm_ref[p] = val[:, p * pw:(p + 1) * pw].astype(BF16)

    @pl.when(j < group_tiles)
    def _():
        write_pairs(acc * DIFF_Q_SCALE)

    @pl.when((j >= group_tiles) & (j < 2 * group_tiles))
    def _():
        write_pairs(acc)
        k_ref[...] = acc

    @pl.when((j >= 2 * group_tiles) & (j < 3 * group_tiles))
    def _():
        write_pairs(acc)
        v_ref[...] = acc

    @pl.when(j >= 3 * group_tiles)
    def _():
        z_ref[...] = acc


def _diff_proj(hn_bf, w_stack, layer):
    m, d = hn_bf.shape
    tm = min(PROJ_TM, m)
    tn = DIFF_TN
    pw = 2 * DIFF_HD
    g = D_INNER // tn
    ppt = tn // pw

    def rows_spec(group):
        return pl.BlockSpec((tm, tn), lambda i, j: (i, jnp.clip(j - group * g, 0, g - 1)))

    return pl.pallas_call(
        functools.partial(_diff_proj_kernel, group_tiles=g),
        grid=(m // tm, 4 * g),
        in_specs=[
            pl.BlockSpec((tm, d), lambda i, j: (i, 0)),
            pl.BlockSpec((1, d, tn), lambda i, j: (layer, 0, j)),
        ],
        out_specs=[
            pl.BlockSpec((ppt, tm, pw), lambda i, j: (jnp.minimum(j, 3 * g - 1), i, 0)),
            rows_spec(1), rows_spec(2), rows_spec(3),
        ],
        out_shape=[
            jax.ShapeDtypeStruct((3 * DIFF_HEADS, m, pw), BF16),
            jax.ShapeDtypeStruct((m, D_INNER), F32),
            jax.ShapeDtypeStruct((m, D_INNER), F32),
            jax.ShapeDtypeStruct((m, D_INNER), F32),
        ],
        compiler_params=_params(("parallel", "arbitrary"), VMEM_LIMIT_WIDE_BYTES),
        name="diff_proj",
    )(hn_bf, w_stack)


def _row_rms_scale(ssq_ref):
    ssq = ssq_ref[0]
    for gi in range(1, ssq_ref.shape[0]):
        ssq = ssq + ssq_ref[gi]
    return lax.rsqrt(ssq * (1.0 / D_INNER) + NORM_EPS)


def _out_proj_kernel(*refs, has_ssq, emit_x):
    h_ref = refs[0]
    ssq_ref = refs[1] if has_ssq else None
    res_ref, w_ref, nw_ref = refs[1 + has_ssq:4 + has_ssq]
    outs = refs[4 + has_ssq:]
    acc = _dot(h_ref[...], w_ref[0])
    if has_ssq:
        acc = acc * _row_rms_scale(ssq_ref)
    x_new = res_ref[...] + acc
    if emit_x:
        outs[0][...] = x_new
    outs[-1][...] = _rms(x_new, nw_ref[...]).astype(outs[-1].dtype)


def _out_proj(h_bf, res, w_stack, layer, norm_w, last, ssq=None):
    m, d = h_bf.shape
    n = w_stack.shape[2]
    tm = min(OUT_TM, m)
    in_specs = [pl.BlockSpec((tm, d), lambda i: (i, 0))]
    args = [h_bf]
    if ssq is not None:
        in_specs.append(pl.BlockSpec((ssq.shape[0], tm, 1), lambda i: (0, i, 0)))
        args.append(ssq)
    in_specs += [
        pl.BlockSpec((tm, n), lambda i: (i, 0)),
        pl.BlockSpec((1, d, n), lambda i: (layer, 0, 0)),
        pl.BlockSpec((1, n), lambda i: (0, 0)),
    ]
    args += [res, w_stack, norm_w.reshape(1, n)]
    out_shape = [jax.ShapeDtypeStruct((m, n), F32 if last else BF16)]
    if not last:
        out_shape.insert(0, jax.ShapeDtypeStruct((m, n), F32))
    outs = pl.pallas_call(
        functools.partial(_out_proj_kernel, has_ssq=ssq is not None, emit_x=not last),
        grid=(m // tm,),
        in_specs=in_specs,
        out_specs=[pl.BlockSpec((tm, n), lambda i: (i, 0)) for _ in out_shape],
        out_shape=out_shape,
        compiler_params=_params(("parallel",)),
        name="out_proj",
    )(*args)
    return (None, outs[0]) if last else (outs[0], outs[1])


def _block_constants(tb, chunk):
    row = np.arange(tb)[:, None]
    col = np.arange(tb)[None, :]
    rc, cc = row // chunk, col // chunk
    same = (rc == cc) & (row >= col)
    dist = np.where(same, 0, np.where(rc > cc, rc - cc, -1)).astype(np.int32)
    return jnp.asarray(same.astype(np.float32), BF16), jnp.asarray(dist)


def _chunk_cumsum(g, tri_bf):
    g_hi = g.astype(BF16)
    g_lo = (g - g_hi.astype(F32)).astype(BF16)
    return _dot(tri_bf, g_hi) + _dot(tri_bf, g_lo)


def _per_chunk(rows, chunk):
    parts = [jnp.broadcast_to(r, (chunk, r.shape[1])) for r in rows]
    return parts[0] if len(parts) == 1 else jnp.concatenate(parts, axis=0)


def _sum_rows(rows, width):
    out = jnp.zeros((1, width), F32)
    for r in rows:
        out = out + r
    return out


def _block_decay_column(b, chunk):
    tb, w = b.shape
    total = _sum_rows([b[(c + 1) * chunk - 1:(c + 1) * chunk, :] for c in range(tb // chunk)], w)
    return jnp.transpose(jnp.broadcast_to(jnp.exp2(total), (V7X_LANES, w)))[:, 0:1]


def _block_recurrence(q, k, b, v_bf, s0, dist, dec_col, chunk, q_log2_scale):
    w = q.shape[1]
    n = q.shape[0] // chunk
    mids = [b[c * chunk + chunk // 2 - 1:c * chunk + chunk // 2, :] for c in range(n)]
    lasts = [b[(c + 1) * chunk - 1:(c + 1) * chunk, :] for c in range(n)]
    q_mid = q * jnp.exp2(b - _per_chunk([m - q_log2_scale for m in mids], chunk))
    k_mid = k * jnp.exp2(_per_chunk(mids, chunk) - b)
    q_in = q_mid * _per_chunk([jnp.exp2(m) for m in mids], chunk)
    k_out = k_mid * _per_chunk([jnp.exp2(l - m) for l, m in zip(lasts, mids)], chunk)
    attn = jnp.where(dist == 0, _dot_nt(q_mid.astype(BF16), k_mid.astype(BF16)), 0.0)
    q_in_bf = q_in.astype(BF16)
    for d in range(1, n):
        gaps = [jnp.exp2(_sum_rows(lasts[j + 1:j + d], w)) if j + d < n else jnp.zeros((1, w), F32)
                for j in range(n)]
        k_d = k_out if d == 1 else k_out * _per_chunk(gaps, chunk)
        attn = jnp.where(dist == d, _dot_nt(q_in_bf, k_d.astype(BF16)), attn)
    head = _per_chunk([jnp.exp2(_sum_rows(lasts[:c], w)) for c in range(n)], chunk)
    tail = _per_chunk([jnp.exp2(_sum_rows(lasts[c + 1:], w)) for c in range(n)], chunk)
    o = _dot((q_in * head).astype(BF16), s0.astype(BF16)) + _dot(attn.astype(BF16), v_bf)
    s_end = dec_col * s0 + _dot_tn((k_out * tail).astype(BF16), v_bf)
    return o, s_end


def _log_sigmoid(x):
    return jnp.minimum(x, 0.0) - jnp.log(1.0 + jnp.exp(-jnp.abs(x)))


def _gla_kernel(*refs, chunk, zero_init, heads):
    if zero_init:
        (q_ref, k_ref, v_ref, z_ref, ga_ref, wa2_ref, ba_ref, nw_ref, tri_ref, dist_ref,
         h_ref, sout_ref, s_scr) = refs
    else:
        (q_ref, k_ref, v_ref, z_ref, ga_ref, wa2_ref, ba_ref, nw_ref, tri_ref, dist_ref, s0_ref,
         h_ref, sout_ref, s_scr) = refs
    t = pl.program_id(2)

    @pl.when(t == 0)
    def _():
        if zero_init:
            s_scr[...] = jnp.zeros_like(s_scr)
        else:
            s_scr[...] = s0_ref[0, 0]

    gate_in = _dot(ga_ref[...].astype(BF16), wa2_ref[...]) + ba_ref[...]
    g = _log_sigmoid(gate_in) * (LOG2_E / GLA_GATE_NORMALIZER)
    b = _chunk_cumsum(g, tri_ref[...])
    dec_col = _block_decay_column(b, chunk)
    dist = dist_ref[...]
    nw = nw_ref[...]
    for hi in range(heads):
        kc = slice(hi * GLA_DK, (hi + 1) * GLA_DK)
        vc = slice(hi * GLA_DV, (hi + 1) * GLA_DV)
        o, s = _block_recurrence(q_ref[:, kc], k_ref[:, kc], b[:, kc], v_ref[:, vc].astype(BF16),
                                 s_scr[hi], dist, dec_col[kc, :], chunk, math.log2(GLA_DK ** -0.5))
        s_scr[hi] = s
        h_ref[:, vc] = (_rms(o, nw) * _silu(z_ref[:, vc])).astype(BF16)

    @pl.when(t == pl.num_programs(2) - 1)
    def _():
        sout_ref[0] = s_scr[...]


def _gla_mix(proj, gate_lr, wa2_bf, b_a, norm_w, states, layer, batch, seq):
    chunk = min(CHUNK, seq)
    tb = min(REC_TOKENS, seq)
    nt = seq // tb
    hb = GLA_HEADS_PER_STEP
    kb = hb * GLA_DK
    vb = hb * GLA_DV
    k_off = GLA_KD // kb
    v_off = (2 * GLA_KD) // vb
    z_off = (2 * GLA_KD + D_INNER) // vb
    zero_init = states is None
    in_specs = [
        pl.BlockSpec((tb, kb), lambda b, h, t: (b * nt + t, h)),
        pl.BlockSpec((tb, kb), lambda b, h, t: (b * nt + t, k_off + h)),
        pl.BlockSpec((tb, vb), lambda b, h, t: (b * nt + t, v_off + h)),
        pl.BlockSpec((tb, vb), lambda b, h, t: (b * nt + t, z_off + h)),
        pl.BlockSpec((tb, V7X_LANES), lambda b, h, t: (b * nt + t, 0)),
        pl.BlockSpec((V7X_LANES, kb), lambda b, h, t: (0, h)),
        pl.BlockSpec((1, kb), lambda b, h, t: (0, h)),
        pl.BlockSpec((1, GLA_DV), lambda b, h, t: (0, 0)),
        pl.BlockSpec((tb, tb), lambda b, h, t: (0, 0)),
        pl.BlockSpec((tb, tb), lambda b, h, t: (0, 0)),
    ]
    args = [proj, proj, proj, proj, gate_lr, wa2_bf, b_a.reshape(1, GLA_KD), norm_w.reshape(1, GLA_DV),
            *_block_constants(tb, chunk)]
    if not zero_init:
        in_specs.append(pl.BlockSpec((1, 1, hb, GLA_DK, GLA_DV), lambda b, h, t: (layer, b, h, 0, 0)))
        args.append(states)
    h, s_out = pl.pallas_call(
        functools.partial(_gla_kernel, chunk=chunk, zero_init=zero_init, heads=hb),
        grid=(batch, GLA_HEADS // hb, nt),
        in_specs=in_specs,
        out_specs=[
            pl.BlockSpec((tb, vb), lambda b, h, t: (b * nt + t, h)),
            pl.BlockSpec((1, hb, GLA_DK, GLA_DV), lambda b, h, t: (b, h, 0, 0)),
        ],
        out_shape=[
            jax.ShapeDtypeStruct((batch * seq, D_INNER), BF16),
            jax.ShapeDtypeStruct((batch, GLA_HEADS, GLA_DK, GLA_DV), F32),
        ],
        scratch_shapes=[pltpu.VMEM((hb, GLA_DK, GLA_DV), F32)],
        compiler_params=_params(("parallel", "parallel", "arbitrary")),
        name="gla_mix",
    )(*args)
    return h, s_out


def _hgrn_kernel(*refs, chunk, zero_init, layer, heads):
    if zero_init:
        (q_ref, f_ref, i_ref, z_ref, lbp_ref, nw_ref, tri_ref, dist_ref,
         u_ref, ssq_ref, sout_ref, s_scr) = refs
    else:
        (q_ref, f_ref, i_ref, z_ref, lbp_ref, nw_ref, tri_ref, dist_ref, s0_ref,
         u_ref, ssq_ref, sout_ref, s_scr) = refs
    t = pl.program_id(2)

    @pl.when(t == 0)
    def _():
        if zero_init:
            s_scr[...] = jnp.zeros_like(s_scr)
        else:
            s_scr[...] = s0_ref[0, 0]

    lbp = lbp_ref[...]
    lbe = jnp.exp(lbp - jnp.max(lbp, axis=0, keepdims=True))
    lbs = lbe / jnp.sum(lbe, axis=0, keepdims=True)
    lb = jnp.zeros_like(lbs[0:1])
    for r in range(1, layer + 1):
        lb = lb + lbs[r:r + 1]

    q = _silu(q_ref[...])
    fgate = lb + (1.0 - lb) * _sigmoid(f_ref[...])
    k = 1.0 - fgate
    b = _chunk_cumsum(jnp.log(fgate) * LOG2_E, tri_ref[...])
    dec_col = _block_decay_column(b, chunk)
    v_bf = i_ref[...].astype(BF16)
    dist = dist_ref[...]
    ssq = jnp.zeros((q.shape[0], 1), F32)
    for hi in range(heads):
        cols = slice(hi * HGRN_DK, (hi + 1) * HGRN_DK)
        o, s = _block_recurrence(q[:, cols], k[:, cols], b[:, cols], v_bf[:, cols], s_scr[hi],
                                 dist, dec_col[cols, :], chunk, math.log2(HGRN_DK ** -0.5))
        s_scr[hi] = s
        ssq = ssq + jnp.sum(o * o, axis=-1, keepdims=True)
        u_ref[:, cols] = (o * nw_ref[:, cols] * _silu(z_ref[:, cols])).astype(BF16)
    ssq_ref[0] = ssq

    @pl.when(t == pl.num_programs(2) - 1)
    def _():
        sout_ref[0] = s_scr[...]


def _hgrn_mix(proj, lower_bounds, layer, norm_w, states, state_layer, batch, seq):
    chunk = min(CHUNK, seq)
    tb = min(REC_TOKENS, seq)
    nt = seq // tb
    hb = HGRN_HEADS_PER_STEP
    wb = hb * HGRN_DK
    ngroups = HGRN_HEADS // hb
    col_groups = D_INNER // wb
    zero_init = states is None
    in_specs = [
        pl.BlockSpec((tb, wb), lambda b, h, t: (b * nt + t, h)),
        pl.BlockSpec((tb, wb), lambda b, h, t: (b * nt + t, col_groups + h)),
        pl.BlockSpec((tb, wb), lambda b, h, t: (b * nt + t, 2 * col_groups + h)),
        pl.BlockSpec((tb, wb), lambda b, h, t: (b * nt + t, 3 * col_groups + h)),
        pl.BlockSpec((DEPTH, wb), lambda b, h, t: (0, h)),
        pl.BlockSpec((1, wb), lambda b, h, t: (0, h)),
        pl.BlockSpec((tb, tb), lambda b, h, t: (0, 0)),
        pl.BlockSpec((tb, tb), lambda b, h, t: (0, 0)),
    ]
    args = [proj, proj, proj, proj, lower_bounds, norm_w.reshape(1, D_INNER), *_block_constants(tb, chunk)]
    if not zero_init:
        in_specs.append(pl.BlockSpec((1, 1, hb, HGRN_DK, HGRN_DV), lambda b, h, t: (state_layer, b, h, 0, 0)))
        args.append(states)
    u, ssq, s_out = pl.pallas_call(
        functools.partial(_hgrn_kernel, chunk=chunk, zero_init=zero_init, layer=layer, heads=hb),
        grid=(batch, ngroups, nt),
        in_specs=in_specs,
        out_specs=[
            pl.BlockSpec((tb, wb), lambda b, h, t: (b * nt + t, h)),
            pl.BlockSpec((1, tb, 1), lambda b, h, t: (h, b * nt + t, 0)),
            pl.BlockSpec((1, hb, HGRN_DK, HGRN_DV), lambda b, h, t: (b, h, 0, 0)),
        ],
        out_shape=[
            jax.ShapeDtypeStruct((batch * seq, D_INNER), BF16),
            jax.ShapeDtypeStruct((ngroups, batch * seq, 1), F32),
            jax.ShapeDtypeStruct((batch, HGRN_HEADS, HGRN_DK, HGRN_DV), F32),
        ],
        scratch_shapes=[pltpu.VMEM((hb, HGRN_DK, HGRN_DV), F32)],
        compiler_params=_params(("parallel", "parallel", "arbitrary")),
        name="hgrn_mix",
    )(*args)
    return u, ssq, s_out


def _diff_lambda(lam_ref, lam_init):
    lp = lam_ref[...]
    a = jnp.sum(lp[0:1] * lp[1:2], axis=-1, keepdims=True)
    b = jnp.sum(lp[2:3] * lp[3:4], axis=-1, keepdims=True)
    return jnp.exp(a) - jnp.exp(b) + lam_init


def _diff_finish(o1, o2, lam_ref, sw_ref, z, lam_init):
    o = o1 - _diff_lambda(lam_ref, lam_init) * o2
    o = _rms(o, sw_ref[...]) * (1.0 - lam_init)
    return (o * _silu(z)).astype(BF16)


KIND_FIRST, KIND_PLAIN, KIND_MASKED, KIND_DRAIN = 0, 1, 3, 5


def _diff_prompt_kernel(qa_tab, ka_tab, qb_tab, kb_tab, kind_tab, last_tab,
                        q_ref, k_ref, v_ref, z_ref, lam_ref, sw_ref, h_ref, *scratch,
                        bq, bk, rows_per_tile, lam_init):
    per_head = len(scratch) // 2
    heads = []
    for h in range(2):
        s_scr, p0, p1, m_scr, l_scr, a0, a1, acc_scr = scratch[per_head * h:per_head * (h + 1)]
        heads.append((s_scr, (p0, p1), m_scr, l_scr, (a0, a1), acc_scr))
    p = pl.program_id(2)
    qi = qa_tab[p]
    ki = ka_tab[p]
    par_a = qi % 2
    par_b = qb_tab[p] % 2
    first = ki == 0
    reps = (1, 2 * DIFF_HD // V7X_LANES)

    def stage_a(masked, slot):
        for h, (s_scr, p_scr, m_scr, l_scr, a_scr, acc_scr) in enumerate(heads):
            cols = slice(h * DIFF_HD, (h + 1) * DIFF_HD)
            s_scr[...] = _dot_nt(q_ref[0, :, cols], k_ref[0, :, cols])
        for s_scr, p_scr, m_scr, l_scr, a_scr, acc_scr in heads:
            for r in range(bq // rows_per_tile):
                rows = pl.ds(r * rows_per_tile, rows_per_tile)
                s = s_scr[rows, :]
                if masked:
                    shape = (rows_per_tile, bk)
                    q_chunk = (qi * bq + r * rows_per_tile + lax.broadcasted_iota(jnp.int32, shape, 0)) // CHUNK
                    k_chunk = (ki * bk + lax.broadcasted_iota(jnp.int32, shape, 1)) // CHUNK
                    s = jnp.where(k_chunk <= q_chunk, s, -jnp.inf)
                m_prev = jnp.where(first, -jnp.inf, m_scr[rows, :])
                m_new = jnp.maximum(m_prev, jnp.max(s, axis=-1, keepdims=True))
                alpha = jnp.exp2(m_prev - m_new)
                pr = jnp.exp2(s - jnp.tile(m_new, (1, bk // V7X_LANES)))
                l_scr[par_a, rows, :] = alpha * l_scr[par_a, rows, :] + jnp.sum(pr, axis=-1, keepdims=True)
                m_scr[rows, :] = m_new
                a_scr[slot][rows, :] = alpha
                p_scr[slot][rows, :] = pr.astype(BF16)

    def stage_b(slot):
        for s_scr, p_scr, m_scr, l_scr, a_scr, acc_scr in heads:
            alpha = jnp.tile(a_scr[slot][...], reps)
            acc_scr[...] = alpha * acc_scr[...] + _dot(p_scr[slot][...], v_ref[0])

    kind = kind_tab[p]

    @pl.when(kind == KIND_FIRST)
    def _():
        for s_scr, p_scr, m_scr, l_scr, a_scr, acc_scr in heads:
            m_scr[...] = jnp.full_like(m_scr, -jnp.inf)
            l_scr[...] = jnp.zeros_like(l_scr)
            acc_scr[...] = jnp.zeros_like(acc_scr)
        stage_a(True, 0)

    for slot in range(2):
        @pl.when(kind == KIND_PLAIN + slot)
        def _():
            stage_a(False, slot)
            stage_b(1 - slot)

        @pl.when(kind == KIND_MASKED + slot)
        def _():
            stage_a(True, slot)
            stage_b(1 - slot)

        @pl.when(kind == KIND_DRAIN + slot)
        def _():
            stage_b(1 - slot)

    @pl.when(last_tab[p] == 1)
    def _():
        o1, o2 = [acc_scr[...] * jnp.tile(1.0 / l_scr[par_b], reps)
                  for _, _, _, l_scr, _, acc_scr in heads]
        h_ref[...] = _diff_finish(o1, o2, lam_ref, sw_ref, z_ref[...], lam_init)


def _diff_prompt_tables(seq, bq, bk):
    pairs = []
    for qi in range(seq // bq):
        q_lo, q_hi = qi * bq, (qi + 1) * bq
        nk = -(-q_hi // bk)
        for ki in range(nk):
            fully_visible = (ki + 1) * bk <= (q_lo // CHUNK + 1) * CHUNK
            pairs.append((qi, ki, not fully_visible, ki == nk - 1))
    assert pairs[0][2], "the first pair sits on the diagonal"
    n = len(pairs)
    qa = [pairs[min(p, n - 1)][0] for p in range(n + 1)]
    ka = [pairs[min(p, n - 1)][1] for p in range(n + 1)]
    qb = [pairs[max(p - 1, 0)][0] for p in range(n + 1)]
    kb = [pairs[max(p - 1, 0)][1] for p in range(n + 1)]
    kind = ([KIND_FIRST] + [(KIND_MASKED if pairs[p][2] else KIND_PLAIN) + p % 2 for p in range(1, n)]
            + [KIND_DRAIN + n % 2])
    last = [0] + [1 if pairs[p - 1][3] else 0 for p in range(1, n + 1)]
    return [jnp.asarray(np.asarray(t, np.int32)) for t in (qa, ka, qb, kb, kind, last)]


def _diff_prompt_attn(qkv, z, lam_p, subln_w, lam_init, batch, seq):
    bq = min(ATT_BQ, seq)
    bk = min(ATT_BK, seq)
    nq = seq // bq
    nkb = seq // bk
    pw = 2 * DIFF_HD
    tabs = _diff_prompt_tables(seq, bq, bk)
    n_steps = int(tabs[0].shape[0])

    def spec(shape, index):
        return pl.BlockSpec(shape, lambda b, h, p, qa, ka, qb, kb, kind, last: index(b, h, p, qa, ka, qb, kb))

    grid_spec = pltpu.PrefetchScalarGridSpec(
        num_scalar_prefetch=6,
        grid=(batch, DIFF_HEADS, n_steps),
        in_specs=[
            spec((1, bq, pw), lambda b, h, p, qa, ka, qb, kb: (h, b * nq + qa[p], 0)),
            spec((1, bk, pw), lambda b, h, p, qa, ka, qb, kb: (DIFF_HEADS + h, b * nkb + ka[p], 0)),
            spec((1, bk, pw), lambda b, h, p, qa, ka, qb, kb: (2 * DIFF_HEADS + h, b * nkb + kb[p], 0)),
            spec((bq, pw), lambda b, h, p, qa, ka, qb, kb: (b * nq + qb[p], h)),
            spec((4, DIFF_HD), lambda b, h, p, qa, ka, qb, kb: (0, 0)),
            spec((1, pw), lambda b, h, p, qa, ka, qb, kb: (0, 0)),
        ],
        out_specs=spec((bq, pw), lambda b, h, p, qa, ka, qb, kb: (b * nq + qb[p], h)),
        scratch_shapes=2 * [
            pltpu.VMEM((bq, bk), F32),
            pltpu.VMEM((bq, bk), BF16),
            pltpu.VMEM((bq, bk), BF16),
            pltpu.VMEM((bq, V7X_LANES), F32),
            pltpu.VMEM((2, bq, V7X_LANES), F32),
            pltpu.VMEM((bq, V7X_LANES), F32),
            pltpu.VMEM((bq, V7X_LANES), F32),
            pltpu.VMEM((bq, pw), F32),
        ],
    )
    return pl.pallas_call(
        functools.partial(_diff_prompt_kernel, bq=bq, bk=bk, rows_per_tile=min(ATT_ROWS, bq),
                          lam_init=lam_init),
        grid_spec=grid_spec,
        out_shape=jax.ShapeDtypeStruct((batch * seq, D_INNER), BF16),
        compiler_params=_params(("parallel", "parallel", "arbitrary")),
        name="diff_prompt_attn",
    )(*tabs, qkv, qkv, qkv, z, lam_p, subln_w.reshape(1, pw))


def _diff_sample_kernel(qkv_ref, z_ref, kc_ref, vc_ref, lam_ref, sw_ref, h_ref, *, lam_init):
    pw = 2 * DIFF_HD
    for pair in range(DIFF_HEADS):
        pcols = slice(pair * pw, (pair + 1) * pw)
        vc_bf = vc_ref[:, pcols].astype(BF16)
        vn_bf = qkv_ref[2 * DIFF_HEADS + pair]
        outs = []
        for h in range(2):
            cols = slice(h * DIFF_HD, (h + 1) * DIFF_HD)
            ccols = slice(pair * pw + h * DIFF_HD, pair * pw + (h + 1) * DIFF_HD)
            q_bf = qkv_ref[pair, :, cols]
            s_c = _dot_nt(q_bf, kc_ref[:, ccols].astype(BF16))
            s_n = _dot_nt(q_bf, qkv_ref[DIFF_HEADS + pair, :, cols])
            m = jnp.maximum(jnp.max(s_c, axis=-1, keepdims=True), jnp.max(s_n, axis=-1, keepdims=True))
            p_c = jnp.exp2(s_c - m)
            p_n = jnp.exp2(s_n - m)
            l = jnp.sum(p_c, axis=-1, keepdims=True) + jnp.sum(p_n, axis=-1, keepdims=True)
            acc = _dot(p_c.astype(BF16), vc_bf) + _dot(p_n.astype(BF16), vn_bf)
            outs.append(acc * (1.0 / l))
        h_ref[:, pcols] = _diff_finish(outs[0], outs[1], lam_ref, sw_ref, z_ref[:, pcols], lam_init)


def _diff_sample_attn(qkv, z, cache_k2, cache_v2, lam_p, subln_w, lam_init, batch, seq, past):
    pw = 2 * DIFF_HD
    return pl.pallas_call(
        functools.partial(_diff_sample_kernel, lam_init=lam_init),
        grid=(batch,),
        in_specs=[
            pl.BlockSpec((3 * DIFF_HEADS, seq, pw), lambda b: (0, b, 0)),
            pl.BlockSpec((seq, D_INNER), lambda b: (b, 0)),
            pl.BlockSpec((past, D_INNER), lambda b: (b, 0)),
            pl.BlockSpec((past, D_INNER), lambda b: (b, 0)),
            pl.BlockSpec((4, DIFF_HD), lambda b: (0, 0)),
            pl.BlockSpec((1, pw), lambda b: (0, 0)),
        ],
        out_specs=pl.BlockSpec((seq, D_INNER), lambda b: (b, 0)),
        out_shape=jax.ShapeDtypeStruct((batch * seq, D_INNER), BF16),
        compiler_params=_params(("parallel",)),
        name="diff_sample_attn",
    )(qkv, z, cache_k2, cache_v2, lam_p, subln_w.reshape(1, pw))


def kernel(x_prompt, x_sample, state_gla, state_hgrn, cache_k, cache_v, norm_w, final_norm_w,
           gla_w_in, gla_w_a1, gla_w_a2, gla_b_a, gla_norm_w, gla_w_out,
           hgrn_w_in, hgrn_lower_bounds, hgrn_norm_w, hgrn_w_out,
           diff_w_in, diff_lambda, diff_subln_w, diff_w_out):
    bp, tp, d = x_prompt.shape
    bs, ts, _ = x_sample.shape
    past = cache_k.shape[2]
    streams = [(bp, tp), (bs, ts)]
    xs = [x_prompt.reshape(bp * tp, d), x_sample.reshape(bs * ts, d)]

    def out_proj(h, res, w_stack, j, i, ssq=None):
        last = i == DEPTH - 1
        return _out_proj(h, res, w_stack, j, final_norm_w if last else norm_w[i + 1], last, ssq)

    gla_w_in_bf, gla_w_out_bf = gla_w_in.astype(BF16), gla_w_out.astype(BF16)
    hgrn_w_in_bf, hgrn_w_out_bf = hgrn_w_in.astype(BF16), hgrn_w_out.astype(BF16)
    diff_w_in_bf, diff_w_out_bf = diff_w_in.astype(BF16), diff_w_out.astype(BF16)

    hn = [None, None]
    gla_states = [[], []]
    hgrn_states = [[], []]
    k_rows = [[], []]
    v_rows = [[], []]
    ia = ib = ic = 0
    for i in range(DEPTH):
        kind = i % N_MIXERS
        if kind == 0:
            j = ia
            ia += 1
            w_a1 = jnp.pad(gla_w_a1[j], ((0, 0), (0, V7X_LANES - GLA_GATE_RANK))).astype(BF16)
            w_a2 = jnp.pad(gla_w_a2[j], ((0, V7X_LANES - GLA_GATE_RANK), (0, 0))).astype(BF16)
            for si, (batch, seq) in enumerate(streams):
                if i == 0:
                    proj, gate_lr = _norm_proj(xs[si], norm_w[i], gla_w_in_bf, j, w_a1)
                else:
                    proj, gate_lr = _in_proj(hn[si], gla_w_in_bf, j, w_a1)
                states = None if si == 0 else state_gla
                h, s_new = _gla_mix(proj, gate_lr, w_a2, gla_b_a[j], gla_norm_w[j], states, j, batch, seq)
                xs[si], hn[si] = out_proj(h, xs[si], gla_w_out_bf, j, i)
                gla_states[si].append(s_new)
        elif kind == 1:
            j = ib
            ib += 1
            for si, (batch, seq) in enumerate(streams):
                proj = _in_proj(hn[si], hgrn_w_in_bf, j)
                states = None if si == 0 else state_hgrn
                u, ssq, s_new = _hgrn_mix(proj, hgrn_lower_bounds, i, hgrn_norm_w[j], states, j, batch, seq)
                xs[si], hn[si] = out_proj(u, xs[si], hgrn_w_out_bf, j, i, ssq)
                hgrn_states[si].append(s_new)
        else:
            j = ic
            ic += 1
            lam_init = 0.8 - 0.6 * math.exp(-0.3 * i)
            for si, (batch, seq) in enumerate(streams):
                qkv, k_new, v_new, z = _diff_proj(hn[si], diff_w_in_bf, j)
                if si == 0:
                    h = _diff_prompt_attn(qkv, z, diff_lambda[j], diff_subln_w[j], lam_init, batch, seq)
                else:
                    ck = cache_k[j].reshape(batch * past, 2 * DIFF_HEADS * DIFF_HD)
                    cv = cache_v[j].reshape(batch * past, 2 * DIFF_HEADS * DIFF_HD)
                    h = _diff_sample_attn(qkv, z, ck, cv, diff_lambda[j], diff_subln_w[j], lam_init,
                                          batch, seq, past)
                xs[si], hn[si] = out_proj(h, xs[si], diff_w_out_bf, j, i)
                k_rows[si].append(k_new.reshape(batch, seq, 2 * DIFF_HEADS, DIFF_HD))
                v_rows[si].append(v_new.reshape(batch, seq, DIFF_HEADS, 2 * DIFF_HD))

    return (hn[0].reshape(bp, tp, d), hn[1].reshape(bs, ts, d),
            jnp.stack(gla_states[0]), jnp.stack(gla_states[1]),
            jnp.stack(hgrn_states[0]), jnp.stack(hgrn_states[1]),
            jnp.stack(k_rows[0]), jnp.stack(v_rows[0]),
            jnp.stack(k_rows[1]), jnp.stack(v_rows[1]))
```

```python
import functools
import math

import numpy as np
import jax
import jax.numpy as jnp
from jax import lax
from jax.experimental import pallas as pl
from jax.experimental.pallas import tpu as pltpu

D_MODEL = 2048
DEPTH = 4
CHUNK = 64
N_MIXERS = 3
D_INNER = D_MODEL
NORM_EPS = 1e-6
GLA_HEADS = 4
GLA_KD = D_INNER // 2
GLA_DK = GLA_KD // GLA_HEADS
GLA_DV = D_INNER // GLA_HEADS
GLA_GATE_RANK = 16
GLA_GATE_NORMALIZER = 16.0
HGRN_HEADS = 16
HGRN_DK = 128
HGRN_DV = 128
DIFF_HD = 128
DIFF_HEADS = 8

LOG2_E = math.log2(math.e)
DIFF_Q_SCALE = (DIFF_HD ** -0.5) * LOG2_E

F32 = jnp.float32
BF16 = jnp.bfloat16

V7X_LANES = 128
V7X_VMEM_BYTES = 64 * 1024 * 1024
VMEM_LIMIT_BYTES = (V7X_VMEM_BYTES * 3) // 4
VMEM_LIMIT_WIDE_BYTES = (V7X_VMEM_BYTES * 7) // 8

PROJ_TM = 1024
PROJ_TN = 1024
IN_TN = 2048
DIFF_TN = 1024
OUT_TM = 512
REC_TOKENS = 256
GLA_HEADS_PER_STEP = 4
GLA_BLOCKS_PER_STEP = 2
HGRN_HEADS_PER_STEP = 16
ATT_BQ = 1024
ATT_BK = 512
ATT_ROWS = 128


def _params(sem, vmem_limit=VMEM_LIMIT_BYTES):
    return pltpu.CompilerParams(dimension_semantics=sem, vmem_limit_bytes=vmem_limit)


def _dot(a, b):
    return jnp.dot(a, b, preferred_element_type=F32)


def _dot_nt(a, b):
    return lax.dot_general(a, b, (((1,), (1,)), ((), ())), preferred_element_type=F32)


def _dot_tn(a, b):
    return lax.dot_general(a, b, (((0,), (0,)), ((), ())), preferred_element_type=F32)


def _sigmoid(x):
    return 1.0 / (1.0 + jnp.exp(-x))


def _silu(x):
    return x * _sigmoid(x)


def _rms(x, w):
    ms = jnp.mean(x * x, axis=-1, keepdims=True)
    return x * lax.rsqrt(ms + NORM_EPS) * w


def _norm_proj_kernel(*refs, has_aux):
    if has_aux:
        x_ref, nw_ref, w_ref, aw_ref, o_ref, ao_ref, h_scr = refs
    else:
        x_ref, nw_ref, w_ref, o_ref, h_scr = refs

    @pl.when(pl.program_id(1) == 0)
    def _():
        h_scr[...] = _rms(x_ref[...], nw_ref[...]).astype(BF16)
        if has_aux:
            ao_ref[...] = _dot(h_scr[...], aw_ref[...])

    o_ref[...] = _dot(h_scr[...], w_ref[0])


def _norm_proj(x, nw, w_stack, layer, aux_w_bf=None):
    m, d = x.shape
    n = w_stack.shape[2]
    tm = min(PROJ_TM, m)
    tn = PROJ_TN
    in_specs = [
        pl.BlockSpec((tm, d), lambda i, j: (i, 0)),
        pl.BlockSpec((1, d), lambda i, j: (0, 0)),
        pl.BlockSpec((1, d, tn), lambda i, j: (layer, 0, j)),
    ]
    out_shape = [jax.ShapeDtypeStruct((m, n), F32)]
    out_specs = [pl.BlockSpec((tm, tn), lambda i, j: (i, j))]
    args = [x, nw.reshape(1, d), w_stack]
    if aux_w_bf is not None:
        na = aux_w_bf.shape[1]
        in_specs.append(pl.BlockSpec((d, na), lambda i, j: (0, 0)))
        out_shape.append(jax.ShapeDtypeStruct((m, na), F32))
        out_specs.append(pl.BlockSpec((tm, na), lambda i, j: (i, 0)))
        args.append(aux_w_bf)
    outs = pl.pallas_call(
        functools.partial(_norm_proj_kernel, has_aux=aux_w_bf is not None),
        grid=(m // tm, n // tn),
        in_specs=in_specs,
        out_specs=out_specs,
        out_shape=out_shape,
        scratch_shapes=[pltpu.VMEM((tm, d), BF16)],
        compiler_params=_params(("parallel", "arbitrary")),
        name="norm_proj",
    )(*args)
    return outs if aux_w_bf is not None else outs[0]


def _in_proj_kernel(*refs, has_aux):
    if has_aux:
        h_ref, w_ref, aw_ref, o_ref, ao_ref = refs

        @pl.when(pl.program_id(1) == 0)
        def _():
            ao_ref[...] = _dot(h_ref[...], aw_ref[...])
    else:
        h_ref, w_ref, o_ref = refs
    o_ref[...] = _dot(h_ref[...], w_ref[0])


def _in_proj(hn_bf, w_stack, layer, aux_w_bf=None):
    m, d = hn_bf.shape
    n = w_stack.shape[2]
    tm = min(PROJ_TM, m)
    tn = IN_TN
    in_specs = [
        pl.BlockSpec((tm, d), lambda i, j: (i, 0)),
        pl.BlockSpec((1, d, tn), lambda i, j: (layer, 0, j)),
    ]
    out_shape = [jax.ShapeDtypeStruct((m, n), F32)]
    out_specs = [pl.BlockSpec((tm, tn), lambda i, j: (i, j))]
    args = [hn_bf, w_stack]
    if aux_w_bf is not None:
        na = aux_w_bf.shape[1]
        in_specs.append(pl.BlockSpec((d, na), lambda i, j: (0, 0)))
        out_shape.append(jax.ShapeDtypeStruct((m, na), F32))
        out_specs.append(pl.BlockSpec((tm, na), lambda i, j: (i, 0)))
        args.append(aux_w_bf)
    outs = pl.pallas_call(
        functools.partial(_in_proj_kernel, has_aux=aux_w_bf is not None),
        grid=(m // tm, n // tn),
        in_specs=in_specs,
        out_specs=out_specs,
        out_shape=out_shape,
        compiler_params=_params(("parallel", "arbitrary")),
        name="in_proj",
    )(*args)
    return outs if aux_w_bf is not None else outs[0]


def _diff_proj_kernel(h_ref, w_ref, pm_ref, k_ref, v_ref, z_ref, *, group_tiles):
    j = pl.program_id(1)
    acc = _dot(h_ref[...], w_ref[0])
    pw = pm_ref.shape[2]

    def write_pairs(val):
        for p in range(pm_ref.shape[0]):
            pm_ref[p] = val[:, p * pw:(p + 1) * pw].astype(BF16)

    @pl.when(j < group_tiles)
    def _():
        write_pairs(acc * DIFF_Q_SCALE)

    @pl.when((j >= group_tiles) & (j < 2 * group_tiles))
    def _():
        write_pairs(acc)
        k_ref[...] = acc

    @pl.when((j >= 2 * group_tiles) & (j < 3 * group_tiles))
    def _():
        write_pairs(acc)
        v_ref[...] = acc

    @pl.when(j >= 3 * group_tiles)
    def _():
        z_ref[...] = acc


def _diff_proj(hn_bf, w_stack, layer):
    m, d = hn_bf.shape
    tm = min(PROJ_TM, m)
    tn = DIFF_TN
    pw = 2 * DIFF_HD
    g = D_INNER // tn
    ppt = tn // pw

    def rows_spec(group):
        return pl.BlockSpec((tm, tn), lambda i, j: (i, jnp.clip(j - group * g, 0, g - 1)))

    return pl.pallas_call(
        functools.partial(_diff_proj_kernel, group_tiles=g),
        grid=(m // tm, 4 * g),
        in_specs=[
            pl.BlockSpec((tm, d), lambda i, j: (i, 0)),
            pl.BlockSpec((1, d, tn), lambda i, j: (layer, 0, j)),
        ],
        out_specs=[
            pl.BlockSpec((ppt, tm, pw), lambda i, j: (jnp.minimum(j, 3 * g - 1), i, 0)),
            rows_spec(1), rows_spec(2), rows_spec(3),
        ],
        out_shape=[
            jax.ShapeDtypeStruct((3 * DIFF_HEADS, m, pw), BF16),
            jax.ShapeDtypeStruct((m, D_INNER), F32),
            jax.ShapeDtypeStruct((m, D_INNER), F32),
            jax.ShapeDtypeStruct((m, D_INNER), F32),
        ],
        compiler_params=_params(("parallel", "arbitrary"), VMEM_LIMIT_WIDE_BYTES),
        name="diff_proj",
    )(hn_bf, w_stack)


def _row_rms_scale(ssq_ref):
    ssq = ssq_ref[0]
    for gi in range(1, ssq_ref.shape[0]):
        ssq = ssq + ssq_ref[gi]
    return lax.rsqrt(ssq * (1.0 / D_INNER) + NORM_EPS)


def _out_proj_kernel(*refs, has_ssq, emit_x):
    h_ref = refs[0]
    ssq_ref = refs[1] if has_ssq else None
    res_ref, w_ref, nw_ref = refs[1 + has_ssq:4 + has_ssq]
    outs = refs[4 + has_ssq:]
    acc = _dot(h_ref[...], w_ref[0])
    if has_ssq:
        acc = acc * _row_rms_scale(ssq_ref)
    x_new = res_ref[...] + acc
    if emit_x:
        outs[0][...] = x_new
    outs[-1][...] = _rms(x_new, nw_ref[...]).astype(outs[-1].dtype)


def _out_proj(h_bf, res, w_stack, layer, norm_w, last, ssq=None):
    m, d = h_bf.shape
    n = w_stack.shape[2]
    tm = min(OUT_TM, m)
    in_specs = [pl.BlockSpec((tm, d), lambda i: (i, 0))]
    args = [h_bf]
    if ssq is not None:
        in_specs.append(pl.BlockSpec((ssq.shape[0], tm, 1), lambda i: (0, i, 0)))
        args.append(ssq)
    in_specs += [
        pl.BlockSpec((tm, n), lambda i: (i, 0)),
        pl.BlockSpec((1, d, n), lambda i: (layer, 0, 0)),
        pl.BlockSpec((1, n), lambda i: (0, 0)),
    ]
    args += [res, w_stack, norm_w.reshape(1, n)]
    out_shape = [jax.ShapeDtypeStruct((m, n), F32 if last else BF16)]
    if not last:
        out_shape.insert(0, jax.ShapeDtypeStruct((m, n), F32))
    outs = pl.pallas_call(
        functools.partial(_out_proj_kernel, has_ssq=ssq is not None, emit_x=not last),
        grid=(m // tm,),
        in_specs=in_specs,
        out_specs=[pl.BlockSpec((tm, n), lambda i: (i, 0)) for _ in out_shape],
        out_shape=out_shape,
        compiler_params=_params(("parallel",)),
        name="out_proj",
    )(*args)
    return (None, outs[0]) if last else (outs[0], outs[1])


def _block_constants(tb, chunk):
    row = np.arange(tb)[:, None]
    col = np.arange(tb)[None, :]
    rc, cc = row // chunk, col // chunk
    same = (rc == cc) & (row >= col)
    dist = np.where(same, 0, np.where(rc > cc, rc - cc, -1)).astype(np.int32)
    return jnp.asarray(same.astype(np.float32), BF16), jnp.asarray(dist)


def _chunk_cumsum(g, tri_bf):
    g_hi = g.astype(BF16)
    g_lo = (g - g_hi.astype(F32)).astype(BF16)
    return _dot(tri_bf, g_hi) + _dot(tri_bf, g_lo)


def _per_chunk(rows, chunk):
    parts = [jnp.broadcast_to(r, (chunk, r.shape[1])) for r in rows]
    return parts[0] if len(parts) == 1 else jnp.concatenate(parts, axis=0)


def _sum_rows(rows, width):
    out = jnp.zeros((1, width), F32)
    for r in rows:
        out = out + r
    return out


def _block_decay_column(b, chunk):
    tb, w = b.shape
    total = _sum_rows([b[(c + 1) * chunk - 1:(c + 1) * chunk, :] for c in range(tb // chunk)], w)
    return jnp.transpose(jnp.broadcast_to(jnp.exp2(total), (V7X_LANES, w)))[:, 0:1]


def _block_recurrence(q, k, b, v_bf, s0, dist, dec_col, chunk, q_log2_scale):
    w = q.shape[1]
    n = q.shape[0] // chunk
    mids = [b[c * chunk + chunk // 2 - 1:c * chunk + chunk // 2, :] for c in range(n)]
    lasts = [b[(c + 1) * chunk - 1:(c + 1) * chunk, :] for c in range(n)]
    q_mid = q * jnp.exp2(b - _per_chunk([m - q_log2_scale for m in mids], chunk))
    k_mid = k * jnp.exp2(_per_chunk(mids, chunk) - b)
    q_in = q_mid * _per_chunk([jnp.exp2(m) for m in mids], chunk)
    k_out = k_mid * _per_chunk([jnp.exp2(l - m) for l, m in zip(lasts, mids)], chunk)
    attn = jnp.where(dist == 0, _dot_nt(q_mid.astype(BF16), k_mid.astype(BF16)), 0.0)
    q_in_bf = q_in.astype(BF16)
    for d in range(1, n):
        gaps = [jnp.exp2(_sum_rows(lasts[j + 1:j + d], w)) if j + d < n else jnp.zeros((1, w), F32)
                for j in range(n)]
        k_d = k_out if d == 1 else k_out * _per_chunk(gaps, chunk)
        attn = jnp.where(dist == d, _dot_nt(q_in_bf, k_d.astype(BF16)), attn)
    head = _per_chunk([jnp.exp2(_sum_rows(lasts[:c], w)) for c in range(n)], chunk)
    tail = _per_chunk([jnp.exp2(_sum_rows(lasts[c + 1:], w)) for c in range(n)], chunk)
    o = _dot((q_in * head).astype(BF16), s0.astype(BF16)) + _dot(attn.astype(BF16), v_bf)
    s_end = dec_col * s0 + _dot_tn((k_out * tail).astype(BF16), v_bf)
    return o, s_end


def _log_sigmoid(x):
    return jnp.minimum(x, 0.0) - jnp.log(1.0 + jnp.exp(-jnp.abs(x)))


def _gla_kernel(*refs, chunk, zero_init, heads):
    if zero_init:
        (q_ref, k_ref, v_ref, z_ref, ga_ref, wa2_ref, ba_ref, nw_ref, tri_ref, dist_ref,
         h_ref, sout_ref, s_scr) = refs
    else:
        (q_ref, k_ref, v_ref, z_ref, ga_ref, wa2_ref, ba_ref, nw_ref, tri_ref, dist_ref, s0_ref,
         h_ref, sout_ref, s_scr) = refs
    t = pl.program_id(2)

    @pl.when(t == 0)
    def _():
        if zero_init:
            s_scr[...] = jnp.zeros_like(s_scr)
        else:
            s_scr[...] = s0_ref[0, 0]

    dist = dist_ref[...]
    nw = nw_ref[...]
    tb = dist.shape[0]
    states = [s_scr[hi] for hi in range(heads)]
    for blk in range(q_ref.shape[0] // tb):
        rows = pl.ds(blk * tb, tb)
        gate_in = _dot(ga_ref[rows, :].astype(BF16), wa2_ref[...]) + ba_ref[...]
        g = _log_sigmoid(gate_in) * (LOG2_E / GLA_GATE_NORMALIZER)
        b = _chunk_cumsum(g, tri_ref[...])
        dec_col = _block_decay_column(b, chunk)
        for hi in range(heads):
            kc = slice(hi * GLA_DK, (hi + 1) * GLA_DK)
            vc = slice(hi * GLA_DV, (hi + 1) * GLA_DV)
            o, states[hi] = _block_recurrence(
                q_ref[rows, kc], k_ref[rows, kc], b[:, kc], v_ref[rows, vc].astype(BF16),
                states[hi], dist, dec_col[kc, :], chunk, math.log2(GLA_DK ** -0.5))
            h_ref[rows, vc] = (_rms(o, nw) * _silu(z_ref[rows, vc])).astype(BF16)
    for hi in range(heads):
        s_scr[hi] = states[hi]

    @pl.when(t == pl.num_programs(2) - 1)
    def _():
        sout_ref[0] = s_scr[...]


def _gla_mix(proj, gate_lr, wa2_bf, b_a, norm_w, states, layer, batch, seq):
    chunk = min(CHUNK, seq)
    tblk = min(REC_TOKENS, seq)
    tb = min(GLA_BLOCKS_PER_STEP * tblk, seq)
    nt = seq // tb
    hb = GLA_HEADS_PER_STEP
    kb = hb * GLA_DK
    vb = hb * GLA_DV
    k_off = GLA_KD // kb
    v_off = (2 * GLA_KD) // vb
    z_off = (2 * GLA_KD + D_INNER) // vb
    zero_init = states is None
    in_specs = [
        pl.BlockSpec((tb, kb), lambda b, h, t: (b * nt + t, h)),
        pl.BlockSpec((tb, kb), lambda b, h, t: (b * nt + t, k_off + h)),
        pl.BlockSpec((tb, vb), lambda b, h, t: (b * nt + t, v_off + h)),
        pl.BlockSpec((tb, vb), lambda b, h, t: (b * nt + t, z_off + h)),
        pl.BlockSpec((tb, V7X_LANES), lambda b, h, t: (b * nt + t, 0)),
        pl.BlockSpec((V7X_LANES, kb), lambda b, h, t: (0, h)),
        pl.BlockSpec((1, kb), lambda b, h, t: (0, h)),
        pl.BlockSpec((1, GLA_DV), lambda b, h, t: (0, 0)),
        pl.BlockSpec((tblk, tblk), lambda b, h, t: (0, 0)),
        pl.BlockSpec((tblk, tblk), lambda b, h, t: (0, 0)),
    ]
    args = [proj, proj, proj, proj, gate_lr, wa2_bf, b_a.reshape(1, GLA_KD), norm_w.reshape(1, GLA_DV),
            *_block_constants(tblk, chunk)]
    if not zero_init:
        in_specs.append(pl.BlockSpec((1, 1, hb, GLA_DK, GLA_DV), lambda b, h, t: (layer, b, h, 0, 0)))
        args.append(states)
    h, s_out = pl.pallas_call(
        functools.partial(_gla_kernel, chunk=chunk, zero_init=zero_init, heads=hb),
        grid=(batch, GLA_HEADS // hb, nt),
        in_specs=in_specs,
        out_specs=[
            pl.BlockSpec((tb, vb), lambda b, h, t: (b * nt + t, h)),
            pl.BlockSpec((1, hb, GLA_DK, GLA_DV), lambda b, h, t: (b, h, 0, 0)),
        ],
        out_shape=[
            jax.ShapeDtypeStruct((batch * seq, D_INNER), BF16),
            jax.ShapeDtypeStruct((batch, GLA_HEADS, GLA_DK, GLA_DV), F32),
        ],
        scratch_shapes=[pltpu.VMEM((hb, GLA_DK, GLA_DV), F32)],
        compiler_params=_params(("parallel", "parallel", "arbitrary")),
        name="gla_mix",
    )(*args)
    return h, s_out


def _hgrn_kernel(*refs, chunk, zero_init, layer, heads):
    if zero_init:
        (q_ref, f_ref, i_ref, z_ref, lbp_ref, nw_ref, tri_ref, dist_ref,
         u_ref, ssq_ref, sout_ref, s_scr) = refs
    else:
        (q_ref, f_ref, i_ref, z_ref, lbp_ref, nw_ref, tri_ref, dist_ref, s0_ref,
         u_ref, ssq_ref, sout_ref, s_scr) = refs
    t = pl.program_id(2)

    @pl.when(t == 0)
    def _():
        if zero_init:
            s_scr[...] = jnp.zeros_like(s_scr)
        else:
            s_scr[...] = s0_ref[0, 0]

    lbp = lbp_ref[...]
    lbe = jnp.exp(lbp - jnp.max(lbp, axis=0, keepdims=True))
    lbs = lbe / jnp.sum(lbe, axis=0, keepdims=True)
    lb = jnp.zeros_like(lbs[0:1])
    for r in range(1, layer + 1):
        lb = lb + lbs[r:r + 1]

    q = _silu(q_ref[...])
    fgate = lb + (1.0 - lb) * _sigmoid(f_ref[...])
    k = 1.0 - fgate
    b = _chunk_cumsum(jnp.log(fgate) * LOG2_E, tri_ref[...])
    dec_col = _block_decay_column(b, chunk)
    v_bf = i_ref[...].astype(BF16)
    dist = dist_ref[...]
    ssq = jnp.zeros((q.shape[0], 1), F32)
    for hi in range(heads):
        cols = slice(hi * HGRN_DK, (hi + 1) * HGRN_DK)
        o, s = _block_recurrence(q[:, cols], k[:, cols], b[:, cols], v_bf[:, cols], s_scr[hi],
                                 dist, dec_col[cols, :], chunk, math.log2(HGRN_DK ** -0.5))
        s_scr[hi] = s
        ssq = ssq + jnp.sum(o * o, axis=-1, keepdims=True)
        u_ref[:, cols] = (o * nw_ref[:, cols] * _silu(z_ref[:, cols])).astype(BF16)
    ssq_ref[0] = ssq

    @pl.when(t == pl.num_programs(2) - 1)
    def _():
        sout_ref[0] = s_scr[...]


def _hgrn_mix(proj, lower_bounds, layer, norm_w, states, state_layer, batch, seq):
    chunk = min(CHUNK, seq)
    tb = min(REC_TOKENS, seq)
    nt = seq // tb
    hb = HGRN_HEADS_PER_STEP
    wb = hb * HGRN_DK
    ngroups = HGRN_HEADS // hb
    col_groups = D_INNER // wb
    zero_init = states is None
    in_specs = [
        pl.BlockSpec((tb, wb), lambda b, h, t: (b * nt + t, h)),
        pl.BlockSpec((tb, wb), lambda b, h, t: (b * nt + t, col_groups + h)),
        pl.BlockSpec((tb, wb), lambda b, h, t: (b * nt + t, 2 * col_groups + h)),
        pl.BlockSpec((tb, wb), lambda b, h, t: (b * nt + t, 3 * col_groups + h)),
        pl.BlockSpec((DEPTH, wb), lambda b, h, t: (0, h)),
        pl.BlockSpec((1, wb), lambda b, h, t: (0, h)),
        pl.BlockSpec((tb, tb), lambda b, h, t: (0, 0)),
        pl.BlockSpec((tb, tb), lambda b, h, t: (0, 0)),
    ]
    args = [proj, proj, proj, proj, lower_bounds, norm_w.reshape(1, D_INNER), *_block_constants(tb, chunk)]
    if not zero_init:
        in_specs.append(pl.BlockSpec((1, 1, hb, HGRN_DK, HGRN_DV), lambda b, h, t: (state_layer, b, h, 0, 0)))
        args.append(states)
    u, ssq, s_out = pl.pallas_call(
        functools.partial(_hgrn_kernel, chunk=chunk, zero_init=zero_init, layer=layer, heads=hb),
        grid=(batch, ngroups, nt),
        in_specs=in_specs,
        out_specs=[
            pl.BlockSpec((tb, wb), lambda b, h, t: (b * nt + t, h)),
            pl.BlockSpec((1, tb, 1), lambda b, h, t: (h, b * nt + t, 0)),
            pl.BlockSpec((1, hb, HGRN_DK, HGRN_DV), lambda b, h, t: (b, h, 0, 0)),
        ],
        out_shape=[
            jax.ShapeDtypeStruct((batch * seq, D_INNER), BF16),
            jax.ShapeDtypeStruct((ngroups, batch * seq, 1), F32),
            jax.ShapeDtypeStruct((batch, HGRN_HEADS, HGRN_DK, HGRN_DV), F32),
        ],
        scratch_shapes=[pltpu.VMEM((hb, HGRN_DK, HGRN_DV), F32)],
        compiler_params=_params(("parallel", "parallel", "arbitrary")),
        name="hgrn_mix",
    )(*args)
    return u, ssq, s_out


def _diff_lambda(lam_ref, lam_init):
    lp = lam_ref[...]
    a = jnp.sum(lp[0:1] * lp[1:2], axis=-1, keepdims=True)
    b = jnp.sum(lp[2:3] * lp[3:4], axis=-1, keepdims=True)
    return jnp.exp(a) - jnp.exp(b) + lam_init


def _diff_finish(o1, o2, lam_ref, sw_ref, z, lam_init):
    o = o1 - _diff_lambda(lam_ref, lam_init) * o2
    o = _rms(o, sw_ref[...]) * (1.0 - lam_init)
    return (o * _silu(z)).astype(BF16)


KIND_FIRST, KIND_PLAIN, KIND_MASKED, KIND_DRAIN = 0, 1, 3, 5


def _diff_prompt_kernel(qa_tab, ka_tab, qb_tab, kb_tab, kind_tab, last_tab,
                        q_ref, k_ref, v_ref, z_ref, lam_ref, sw_ref, h_ref, *scratch,
                        bq, bk, rows_per_tile, lam_init):
    per_head = len(scratch) // 2
    heads = []
    for h in range(2):
        s_scr, p0, p1, m_scr, l_scr, a0, a1, acc_scr = scratch[per_head * h:per_head * (h + 1)]
        heads.append((s_scr, (p0, p1), m_scr, l_scr, (a0, a1), acc_scr))
    p = pl.program_id(2)
    qi = qa_tab[p]
    ki = ka_tab[p]
    par_a = qi % 2
    par_b = qb_tab[p] % 2
    first = ki == 0
    reps = (1, 2 * DIFF_HD // V7X_LANES)

    def stage_a(masked, slot):
        for h, (s_scr, p_scr, m_scr, l_scr, a_scr, acc_scr) in enumerate(heads):
            cols = slice(h * DIFF_HD, (h + 1) * DIFF_HD)
            s_scr[...] = _dot_nt(q_ref[0, :, cols], k_ref[0, :, cols])
        for s_scr, p_scr, m_scr, l_scr, a_scr, acc_scr in heads:
            for r in range(bq // rows_per_tile):
                rows = pl.ds(r * rows_per_tile, rows_per_tile)
                s = s_scr[rows, :]
                if masked:
                    shape = (rows_per_tile, bk)
                    q_chunk = (qi * bq + r * rows_per_tile + lax.broadcasted_iota(jnp.int32, shape, 0)) // CHUNK
                    k_chunk = (ki * bk + lax.broadcasted_iota(jnp.int32, shape, 1)) // CHUNK
                    s = jnp.where(k_chunk <= q_chunk, s, -jnp.inf)
                m_prev = jnp.where(first, -jnp.inf, m_scr[rows, :])
                m_new = jnp.maximum(m_prev, jnp.max(s, axis=-1, keepdims=True))
                alpha = jnp.exp2(m_prev - m_new)
                pr = jnp.exp2(s - jnp.tile(m_new, (1, bk // V7X_LANES)))
                l_scr[par_a, rows, :] = alpha * l_scr[par_a, rows, :] + jnp.sum(pr, axis=-1, keepdims=True)
                m_scr[rows, :] = m_new
                a_scr[slot][rows, :] = alpha
                p_scr[slot][rows, :] = pr.astype(BF16)

    def stage_b(slot):
        for s_scr, p_scr, m_scr, l_scr, a_scr, acc_scr in heads:
            alpha = jnp.tile(a_scr[slot][...], reps)
            acc_scr[...] = alpha * acc_scr[...] + _dot(p_scr[slot][...], v_ref[0])

    kind = kind_tab[p]

    @pl.when(kind == KIND_FIRST)
    def _():
        for s_scr, p_scr, m_scr, l_scr, a_scr, acc_scr in heads:
            m_scr[...] = jnp.full_like(m_scr, -jnp.inf)
            l_scr[...] = jnp.zeros_like(l_scr)
            acc_scr[...] = jnp.zeros_like(acc_scr)
        stage_a(True, 0)

    for slot in range(2):
        @pl.when(kind == KIND_PLAIN + slot)
        def _():
            stage_a(False, slot)
            stage_b(1 - slot)

        @pl.when(kind == KIND_MASKED + slot)
        def _():
            stage_a(True, slot)
            stage_b(1 - slot)

        @pl.when(kind == KIND_DRAIN + slot)
        def _():
            stage_b(1 - slot)

    @pl.when(last_tab[p] == 1)
    def _():
        o1, o2 = [acc_scr[...] * jnp.tile(1.0 / l_scr[par_b], reps)
                  for _, _, _, l_scr, _, acc_scr in heads]
        h_ref[...] = _diff_finish(o1, o2, lam_ref, sw_ref, z_ref[...], lam_init)


def _diff_prompt_tables(seq, bq, bk):
    pairs = []
    for qi in range(seq // bq):
        q_lo, q_hi = qi * bq, (qi + 1) * bq
        nk = -(-q_hi // bk)
        for ki in range(nk):
            fully_visible = (ki + 1) * bk <= (q_lo // CHUNK + 1) * CHUNK
            pairs.append((qi, ki, not fully_visible, ki == nk - 1))
    assert pairs[0][2], "the first pair sits on the diagonal"
    n = len(pairs)
    qa = [pairs[min(p, n - 1)][0] for p in range(n + 1)]
    ka = [pairs[min(p, n - 1)][1] for p in range(n + 1)]
    qb = [pairs[max(p - 1, 0)][0] for p in range(n + 1)]
    kb = [pairs[max(p - 1, 0)][1] for p in range(n + 1)]
    kind = ([KIND_FIRST] + [(KIND_MASKED if pairs[p][2] else KIND_PLAIN) + p % 2 for p in range(1, n)]
            + [KIND_DRAIN + n % 2])
    last = [0] + [1 if pairs[p - 1][3] else 0 for p in range(1, n + 1)]
    return [jnp.asarray(np.asarray(t, np.int32)) for t in (qa, ka, qb, kb, kind, last)]


def _diff_prompt_attn(qkv, z, lam_p, subln_w, lam_init, batch, seq):
    bq = min(ATT_BQ, seq)
    bk = min(ATT_BK, seq)
    nq = seq // bq
    nkb = seq // bk
    pw = 2 * DIFF_HD
    tabs = _diff_prompt_tables(seq, bq, bk)
    n_steps = int(tabs[0].shape[0])

    def spec(shape, index):
        return pl.BlockSpec(shape, lambda b, h, p, qa, ka, qb, kb, kind, last: index(b, h, p, qa, ka, qb, kb))

    grid_spec = pltpu.PrefetchScalarGridSpec(
        num_scalar_prefetch=6,
        grid=(batch, DIFF_HEADS, n_steps),
        in_specs=[
            spec((1, bq, pw), lambda b, h, p, qa, ka, qb, kb: (h, b * nq + qa[p], 0)),
            spec((1, bk, pw), lambda b, h, p, qa, ka, qb, kb: (DIFF_HEADS + h, b * nkb + ka[p], 0)),
            spec((1, bk, pw), lambda b, h, p, qa, ka, qb, kb: (2 * DIFF_HEADS + h, b * nkb + kb[p], 0)),
            spec((bq, pw), lambda b, h, p, qa, ka, qb, kb: (b * nq + qb[p], h)),
            spec((4, DIFF_HD), lambda b, h, p, qa, ka, qb, kb: (0, 0)),
            spec((1, pw), lambda b, h, p, qa, ka, qb, kb: (0, 0)),
        ],
        out_specs=spec((bq, pw), lambda b, h, p, qa, ka, qb, kb: (b * nq + qb[p], h)),
        scratch_shapes=2 * [
            pltpu.VMEM((bq, bk), F32),
            pltpu.VMEM((bq, bk), BF16),
            pltpu.VMEM((bq, bk), BF16),
            pltpu.VMEM((bq, V7X_LANES), F32),
            pltpu.VMEM((2, bq, V7X_LANES), F32),
            pltpu.VMEM((bq, V7X_LANES), F32),
            pltpu.VMEM((bq, V7X_LANES), F32),
            pltpu.VMEM((bq, pw), F32),
        ],
    )
    return pl.pallas_call(
        functools.partial(_diff_prompt_kernel, bq=bq, bk=bk, rows_per_tile=min(ATT_ROWS, bq),
                          lam_init=lam_init),
        grid_spec=grid_spec,
        out_shape=jax.ShapeDtypeStruct((batch * seq, D_INNER), BF16),
        compiler_params=_params(("parallel", "parallel", "arbitrary")),
        name="diff_prompt_attn",
    )(*tabs, qkv, qkv, qkv, z, lam_p, subln_w.reshape(1, pw))


def _diff_sample_kernel(qkv_ref, z_ref, kc_ref, vc_ref, lam_ref, sw_ref, h_ref, *, lam_init):
    pw = 2 * DIFF_HD
    for pair in range(DIFF_HEADS):
        pcols = slice(pair * pw, (pair + 1) * pw)
        vc_bf = vc_ref[:, pcols].astype(BF16)
        vn_bf = qkv_ref[2 * DIFF_HEADS + pair]
        outs = []
        for h in range(2):
            cols = slice(h * DIFF_HD, (h + 1) * DIFF_HD)
            ccols = slice(pair * pw + h * DIFF_HD, pair * pw + (h + 1) * DIFF_HD)
            q_bf = qkv_ref[pair, :, cols]
            s_c = _dot_nt(q_bf, kc_ref[:, ccols].astype(BF16))
            s_n = _dot_nt(q_bf, qkv_ref[DIFF_HEADS + pair, :, cols])
            m = jnp.maximum(jnp.max(s_c, axis=-1, keepdims=True), jnp.max(s_n, axis=-1, keepdims=True))
            p_c = jnp.exp2(s_c - m)
            p_n = jnp.exp2(s_n - m)
            l = jnp.sum(p_c, axis=-1, keepdims=True) + jnp.sum(p_n, axis=-1, keepdims=True)
            acc = _dot(p_c.astype(BF16), vc_bf) + _dot(p_n.astype(BF16), vn_bf)
            outs.append(acc * (1.0 / l))
        h_ref[:, pcols] = _diff_finish(outs[0], outs[1], lam_ref, sw_ref, z_ref[:, pcols], lam_init)


def _diff_sample_attn(qkv, z, cache_k2, cache_v2, lam_p, subln_w, lam_init, batch, seq, past):
    pw = 2 * DIFF_HD
    return pl.pallas_call(
        functools.partial(_diff_sample_kernel, lam_init=lam_init),
        grid=(batch,),
        in_specs=[
            pl.BlockSpec((3 * DIFF_HEADS, seq, pw), lambda b: (0, b, 0)),
            pl.BlockSpec((seq, D_INNER), lambda b: (b, 0)),
            pl.BlockSpec((past, D_INNER), lambda b: (b, 0)),
            pl.BlockSpec((past, D_INNER), lambda b: (b, 0)),
            pl.BlockSpec((4, DIFF_HD), lambda b: (0, 0)),
            pl.BlockSpec((1, pw), lambda b: (0, 0)),
        ],
        out_specs=pl.BlockSpec((seq, D_INNER), lambda b: (b, 0)),
        out_shape=jax.ShapeDtypeStruct((batch * seq, D_INNER), BF16),
        compiler_params=_params(("parallel",)),
        name="diff_sample_attn",
    )(qkv, z, cache_k2, cache_v2, lam_p, subln_w.reshape(1, pw))


def kernel(x_prompt, x_sample, state_gla, state_hgrn, cache_k, cache_v, norm_w, final_norm_w,
           gla_w_in, gla_w_a1, gla_w_a2, gla_b_a, gla_norm_w, gla_w_out,
           hgrn_w_in, hgrn_lower_bounds, hgrn_norm_w, hgrn_w_out,
           diff_w_in, diff_lambda, diff_subln_w, diff_w_out):
    bp, tp, d = x_prompt.shape
    bs, ts, _ = x_sample.shape
    past = cache_k.shape[2]
    streams = [(bp, tp), (bs, ts)]
    xs = [x_prompt.reshape(bp * tp, d), x_sample.reshape(bs * ts, d)]

    def out_proj(h, res, w_stack, j, i, ssq=None):
        last = i == DEPTH - 1
        return _out_proj(h, res, w_stack, j, final_norm_w if last else norm_w[i + 1], last, ssq)

    gla_w_in_bf, gla_w_out_bf = gla_w_in.astype(BF16), gla_w_out.astype(BF16)
    hgrn_w_in_bf, hgrn_w_out_bf = hgrn_w_in.astype(BF16), hgrn_w_out.astype(BF16)
    diff_w_in_bf, diff_w_out_bf = diff_w_in.astype(BF16), diff_w_out.astype(BF16)

    hn = [None, None]
    gla_states = [[], []]
    hgrn_states = [[], []]
    k_rows = [[], []]
    v_rows = [[], []]
    ia = ib = ic = 0
    for i in range(DEPTH):
        kind = i % N_MIXERS
        if kind == 0:
            j = ia
            ia += 1
            w_a1 = jnp.pad(gla_w_a1[j], ((0, 0), (0, V7X_LANES - GLA_GATE_RANK))).astype(BF16)
            w_a2 = jnp.pad(gla_w_a2[j], ((0, V7X_LANES - GLA_GATE_RANK), (0, 0))).astype(BF16)
            for si, (batch, seq) in enumerate(streams):
                if i == 0:
                    proj, gate_lr = _norm_proj(xs[si], norm_w[i], gla_w_in_bf, j, w_a1)
                else:
                    proj, gate_lr = _in_proj(hn[si], gla_w_in_bf, j, w_a1)
                states = None if si == 0 else state_gla
                h, s_new = _gla_mix(proj, gate_lr, w_a2, gla_b_a[j], gla_norm_w[j], states, j, batch, seq)
                xs[si], hn[si] = out_proj(h, xs[si], gla_w_out_bf, j, i)
                gla_states[si].append(s_new)
        elif kind == 1:
            j = ib
            ib += 1
            for si, (batch, seq) in enumerate(streams):
                proj = _in_proj(hn[si], hgrn_w_in_bf, j)
                states = None if si == 0 else state_hgrn
                u, ssq, s_new = _hgrn_mix(proj, hgrn_lower_bounds, i, hgrn_norm_w[j], states, j, batch, seq)
                xs[si], hn[si] = out_proj(u, xs[si], hgrn_w_out_bf, j, i, ssq)
                hgrn_states[si].append(s_new)
        else:
            j = ic
            ic += 1
            lam_init = 0.8 - 0.6 * math.exp(-0.3 * i)
            for si, (batch, seq) in enumerate(streams):
                qkv, k_new, v_new, z = _diff_proj(hn[si], diff_w_in_bf, j)
                if si == 0:
                    h = _diff_prompt_attn(qkv, z, diff_lambda[j], diff_subln_w[j], lam_init, batch, seq)
                else:
                    ck = cache_k[j].reshape(batch * past, 2 * DIFF_HEADS * DIFF_HD)
                    cv = cache_v[j].reshape(batch * past, 2 * DIFF_HEADS * DIFF_HD)
                    h = _diff_sample_attn(qkv, z, ck, cv, diff_lambda[j], diff_subln_w[j], lam_init,
                                          batch, seq, past)
                xs[si], hn[si] = out_proj(h, xs[si], diff_w_out_bf, j, i)
                k_rows[si].append(k_new.reshape(batch, seq, 2 * DIFF_HEADS, DIFF_HD))
                v_rows[si].append(v_new.reshape(batch, seq, DIFF_HEADS, 2 * DIFF_HD))

    return (hn[0].reshape(bp, tp, d), hn[1].reshape(bs, ts, d),
            jnp.stack(gla_states[0]), jnp.stack(gla_states[1]),
            jnp.stack(hgrn_states[0]), jnp.stack(hgrn_states[1]),
            jnp.stack(k_rows[0]), jnp.stack(v_rows[0]),
            jnp.stack(k_rows[1]), jnp.stack(v_rows[1]))
```

```python
import functools
import math

import numpy as np
import jax
import jax.numpy as jnp
from jax import lax
from jax.experimental import pallas as pl
from jax.experimental.pallas import tpu as pltpu

D_MODEL = 2048
DEPTH = 4
CHUNK = 64
N_MIXERS = 3
D_INNER = D_MODEL
NORM_EPS = 1e-6
GLA_HEADS = 4
GLA_KD = D_INNER // 2
GLA_DK = GLA_KD // GLA_HEADS
GLA_DV = D_INNER // GLA_HEADS
GLA_GATE_RANK = 16
GLA_GATE_NORMALIZER = 16.0
HGRN_HEADS = 16
HGRN_DK = 128
HGRN_DV = 128
DIFF_HD = 128
DIFF_HEADS = 8

LOG2_E = math.log2(math.e)
DIFF_Q_SCALE = (DIFF_HD ** -0.5) * LOG2_E

F32 = jnp.float32
BF16 = jnp.bfloat16

V7X_LANES = 128
V7X_VMEM_BYTES = 64 * 1024 * 1024
VMEM_LIMIT_BYTES = (V7X_VMEM_BYTES * 3) // 4
VMEM_LIMIT_WIDE_BYTES = (V7X_VMEM_BYTES * 7) // 8

PROJ_TM = 1024
PROJ_TN = 1024
IN_TN = 2048
DIFF_TN = 1024
OUT_TM = 512
REC_TOKENS = 256
GLA_HEADS_PER_STEP = 4
HGRN_HEADS_PER_STEP = 16
ATT_BQ = 1024
ATT_BK = 1024
ATT_ROWS = 128


def _params(sem, vmem_limit=VMEM_LIMIT_BYTES):
    return pltpu.CompilerParams(dimension_semantics=sem, vmem_limit_bytes=vmem_limit)


def _dot(a, b):
    return jnp.dot(a, b, preferred_element_type=F32)


def _dot_nt(a, b):
    return lax.dot_general(a, b, (((1,), (1,)), ((), ())), preferred_element_type=F32)


def _dot_tn(a, b):
    return lax.dot_general(a, b, (((0,), (0,)), ((), ())), preferred_element_type=F32)


def _sigmoid(x):
    return 1.0 / (1.0 + jnp.exp(-x))


def _silu(x):
    return x * _sigmoid(x)


def _rms(x, w):
    ms = jnp.mean(x * x, axis=-1, keepdims=True)
    return x * lax.rsqrt(ms + NORM_EPS) * w


def _norm_proj_kernel(*refs, has_aux):
    if has_aux:
        x_ref, nw_ref, w_ref, aw_ref, o_ref, ao_ref, h_scr = refs
    else:
        x_ref, nw_ref, w_ref, o_ref, h_scr = refs

    @pl.when(pl.program_id(1) == 0)
    def _():
        h_scr[...] = _rms(x_ref[...], nw_ref[...]).astype(BF16)
        if has_aux:
            ao_ref[...] = _dot(h_scr[...], aw_ref[...])

    o_ref[...] = _dot(h_scr[...], w_ref[0])


def _norm_proj(x, nw, w_stack, layer, aux_w_bf=None):
    m, d = x.shape
    n = w_stack.shape[2]
    tm = min(PROJ_TM, m)
    tn = PROJ_TN
    in_specs = [
        pl.BlockSpec((tm, d), lambda i, j: (i, 0)),
        pl.BlockSpec((1, d), lambda i, j: (0, 0)),
        pl.BlockSpec((1, d, tn), lambda i, j: (layer, 0, j)),
    ]
    out_shape = [jax.ShapeDtypeStruct((m, n), F32)]
    out_specs = [pl.BlockSpec((tm, tn), lambda i, j: (i, j))]
    args = [x, nw.reshape(1, d), w_stack]
    if aux_w_bf is not None:
        na = aux_w_bf.shape[1]
        in_specs.append(pl.BlockSpec((d, na), lambda i, j: (0, 0)))
        out_shape.append(jax.ShapeDtypeStruct((m, na), F32))
        out_specs.append(pl.BlockSpec((tm, na), lambda i, j: (i, 0)))
        args.append(aux_w_bf)
    outs = pl.pallas_call(
        functools.partial(_norm_proj_kernel, has_aux=aux_w_bf is not None),
        grid=(m // tm, n // tn),
        in_specs=in_specs,
        out_specs=out_specs,
        out_shape=out_shape,
        scratch_shapes=[pltpu.VMEM((tm, d), BF16)],
        compiler_params=_params(("parallel", "arbitrary")),
        name="norm_proj",
    )(*args)
    return outs if aux_w_bf is not None else outs[0]


def _in_proj_kernel(*refs, has_aux):
    if has_aux:
        h_ref, w_ref, aw_ref, o_ref, ao_ref = refs

        @pl.when(pl.program_id(1) == 0)
        def _():
            ao_ref[...] = _dot(h_ref[...], aw_ref[...])
    else:
        h_ref, w_ref, o_ref = refs
    o_ref[...] = _dot(h_ref[...], w_ref[0])


def _in_proj(hn_bf, w_stack, layer, aux_w_bf=None):
    m, d = hn_bf.shape
    n = w_stack.shape[2]
    tm = min(PROJ_TM, m)
    tn = IN_TN
    in_specs = [
        pl.BlockSpec((tm, d), lambda i, j: (i, 0)),
        pl.BlockSpec((1, d, tn), lambda i, j: (layer, 0, j)),
    ]
    out_shape = [jax.ShapeDtypeStruct((m, n), F32)]
    out_specs = [pl.BlockSpec((tm, tn), lambda i, j: (i, j))]
    args = [hn_bf, w_stack]
    if aux_w_bf is not None:
        na = aux_w_bf.shape[1]
        in_specs.append(pl.BlockSpec((d, na), lambda i, j: (0, 0)))
        out_shape.append(jax.ShapeDtypeStruct((m, na), F32))
        out_specs.append(pl.BlockSpec((tm, na), lambda i, j: (i, 0)))
        args.append(aux_w_bf)
    outs = pl.pallas_call(
        functools.partial(_in_proj_kernel, has_aux=aux_w_bf is not None),
        grid=(m // tm, n // tn),
        in_specs=in_specs,
        out_specs=out_specs,
        out_shape=out_shape,
        compiler_params=_params(("parallel", "arbitrary")),
        name="in_proj",
    )(*args)
    return outs if aux_w_bf is not None else outs[0]


def _diff_proj_kernel(h_ref, w_ref, pm_ref, k_ref, v_ref, z_ref, *, group_tiles):
    j = pl.program_id(1)
    acc = _dot(h_ref[...], w_ref[0])
    pw = pm_ref.shape[2]

    def write_pairs(val):
        for p in range(pm_ref.shape[0]):
            pm_ref[p] = val[:, p * pw:(p + 1) * pw].astype(BF16)

    @pl.when(j < group_tiles)
    def _():
        write_pairs(acc * DIFF_Q_SCALE)

    @pl.when((j >= group_tiles) & (j < 2 * group_tiles))
    def _():
        write_pairs(acc)
        k_ref[...] = acc

    @pl.when((j >= 2 * group_tiles) & (j < 3 * group_tiles))
    def _():
        write_pairs(acc)
        v_ref[...] = acc

    @pl.when(j >= 3 * group_tiles)
    def _():
        z_ref[...] = acc


def _diff_proj(hn_bf, w_stack, layer):
    m, d = hn_bf.shape
    tm = min(PROJ_TM, m)
    tn = DIFF_TN
    pw = 2 * DIFF_HD
    g = D_INNER // tn
    ppt = tn // pw

    def rows_spec(group):
        return pl.BlockSpec((tm, tn), lambda i, j: (i, jnp.clip(j - group * g, 0, g - 1)))

    return pl.pallas_call(
        functools.partial(_diff_proj_kernel, group_tiles=g),
        grid=(m // tm, 4 * g),
        in_specs=[
            pl.BlockSpec((tm, d), lambda i, j: (i, 0)),
            pl.BlockSpec((1, d, tn), lambda i, j: (layer, 0, j)),
        ],
        out_specs=[
            pl.BlockSpec((ppt, tm, pw), lambda i, j: (jnp.minimum(j, 3 * g - 1), i, 0)),
            rows_spec(1), rows_spec(2), rows_spec(3),
        ],
        out_shape=[
            jax.ShapeDtypeStruct((3 * DIFF_HEADS, m, pw), BF16),
            jax.ShapeDtypeStruct((m, D_INNER), F32),
            jax.ShapeDtypeStruct((m, D_INNER), F32),
            jax.ShapeDtypeStruct((m, D_INNER), F32),
        ],
        compiler_params=_params(("parallel", "arbitrary"), VMEM_LIMIT_WIDE_BYTES),
        name="diff_proj",
    )(hn_bf, w_stack)


def _row_rms_scale(ssq_ref):
    ssq = ssq_ref[0]
    for gi in range(1, ssq_ref.shape[0]):
        ssq = ssq + ssq_ref[gi]
    return lax.rsqrt(ssq * (1.0 / D_INNER) + NORM_EPS)


def _out_proj_kernel(*refs, has_ssq, emit_x):
    h_ref = refs[0]
    ssq_ref = refs[1] if has_ssq else None
    res_ref, w_ref, nw_ref = refs[1 + has_ssq:4 + has_ssq]
    outs = refs[4 + has_ssq:]
    acc = _dot(h_ref[...], w_ref[0])
    if has_ssq:
        acc = acc * _row_rms_scale(ssq_ref)
    x_new = res_ref[...] + acc
    if emit_x:
        outs[0][...] = x_new
    outs[-1][...] = _rms(x_new, nw_ref[...]).astype(outs[-1].dtype)


def _out_proj(h_bf, res, w_stack, layer, norm_w, last, ssq=None):
    m, d = h_bf.shape
    n = w_stack.shape[2]
    tm = min(OUT_TM, m)
    in_specs = [pl.BlockSpec((tm, d), lambda i: (i, 0))]
    args = [h_bf]
    if ssq is not None:
        in_specs.append(pl.BlockSpec((ssq.shape[0], tm, 1), lambda i: (0, i, 0)))
        args.append(ssq)
    in_specs += [
        pl.BlockSpec((tm, n), lambda i: (i, 0)),
        pl.BlockSpec((1, d, n), lambda i: (layer, 0, 0)),
        pl.BlockSpec((1, n), lambda i: (0, 0)),
    ]
    args += [res, w_stack, norm_w.reshape(1, n)]
    out_shape = [jax.ShapeDtypeStruct((m, n), F32 if last else BF16)]
    if not last:
        out_shape.insert(0, jax.ShapeDtypeStruct((m, n), F32))
    outs = pl.pallas_call(
        functools.partial(_out_proj_kernel, has_ssq=ssq is not None, emit_x=not last),
        grid=(m // tm,),
        in_specs=in_specs,
        out_specs=[pl.BlockSpec((tm, n), lambda i: (i, 0)) for _ in out_shape],
        out_shape=out_shape,
        compiler_params=_params(("parallel",)),
        name="out_proj",
    )(*args)
    return (None, outs[0]) if last else (outs[0], outs[1])


def _block_constants(tb, chunk):
    n = tb // chunk
    assert n & (n - 1) == 0, "chunks per block must be a power of two"
    row = np.arange(tb)[:, None]
    col = np.arange(tb)[None, :]
    rc, cc = row // chunk, col // chunk
    same = (rc == cc) & (row >= col)
    level = np.where(same, 0, -1)
    for lv in range(1, n.bit_length()):
        group = 1 << lv
        meet = (rc > cc) & (rc // group == cc // group) & (rc // (group // 2) != cc // (group // 2))
        level = np.where(meet, lv, level)
    return jnp.asarray(same.astype(np.float32), BF16), jnp.asarray(level.astype(np.int32))


def _chunk_cumsum(g, tri_bf):
    g_hi = g.astype(BF16)
    g_lo = (g - g_hi.astype(F32)).astype(BF16)
    return _dot(tri_bf, g_hi) + _dot(tri_bf, g_lo)


def _per_chunk(rows, chunk):
    parts = [jnp.broadcast_to(r, (chunk, r.shape[1])) for r in rows]
    return parts[0] if len(parts) == 1 else jnp.concatenate(parts, axis=0)


def _sum_rows(rows, width):
    out = jnp.zeros((1, width), F32)
    for r in rows:
        out = out + r
    return out


def _block_decay_column(b, chunk):
    tb, w = b.shape
    total = _sum_rows([b[(c + 1) * chunk - 1:(c + 1) * chunk, :] for c in range(tb // chunk)], w)
    return jnp.transpose(jnp.broadcast_to(jnp.exp2(total), (V7X_LANES, w)))[:, 0:1]


def _block_recurrence(q, k, b, v_bf, s0, level, dec_col, chunk, q_log2_scale):
    w = q.shape[1]
    n = q.shape[0] // chunk
    mids = [b[c * chunk + chunk // 2 - 1:c * chunk + chunk // 2, :] for c in range(n)]
    lasts = [b[(c + 1) * chunk - 1:(c + 1) * chunk, :] for c in range(n)]
    q_mid = q * jnp.exp2(b - _per_chunk([m - q_log2_scale for m in mids], chunk))
    k_mid = k * jnp.exp2(_per_chunk(mids, chunk) - b)
    q_in = q_mid * _per_chunk([jnp.exp2(m) for m in mids], chunk)
    k_out = k_mid * _per_chunk([jnp.exp2(l - m) for l, m in zip(lasts, mids)], chunk)
    attn = jnp.where(level == 0, _dot_nt(q_mid.astype(BF16), k_mid.astype(BF16)), 0.0)
    zero_row = jnp.zeros((1, w), F32)
    for lv in range(1, n.bit_length()):
        group, half = 1 << lv, 1 << (lv - 1)
        if half == 1:
            q_l, k_l = q_in, k_out
        else:
            mid_of = lambda c: (c // group) * group + half
            q_l = q_in * _per_chunk([jnp.exp2(_sum_rows(lasts[mid_of(c):c], w)) if c % group >= half
                                     else zero_row for c in range(n)], chunk)
            k_l = k_out * _per_chunk([jnp.exp2(_sum_rows(lasts[j + 1:mid_of(j)], w)) if j % group < half
                                      else zero_row for j in range(n)], chunk)
        attn = jnp.where(level == lv, _dot_nt(q_l.astype(BF16), k_l.astype(BF16)), attn)
    head = _per_chunk([jnp.exp2(_sum_rows(lasts[:c], w)) for c in range(n)], chunk)
    tail = _per_chunk([jnp.exp2(_sum_rows(lasts[c + 1:], w)) for c in range(n)], chunk)
    o = _dot((q_in * head).astype(BF16), s0.astype(BF16)) + _dot(attn.astype(BF16), v_bf)
    s_end = dec_col * s0 + _dot_tn((k_out * tail).astype(BF16), v_bf)
    return o, s_end


def _log_sigmoid(x):
    return jnp.minimum(x, 0.0) - jnp.log(1.0 + jnp.exp(-jnp.abs(x)))


def _gla_kernel(*refs, chunk, zero_init, heads):
    if zero_init:
        (q_ref, k_ref, v_ref, z_ref, ga_ref, wa2_ref, ba_ref, nw_ref, tri_ref, level_ref,
         h_ref, sout_ref, s_scr) = refs
    else:
        (q_ref, k_ref, v_ref, z_ref, ga_ref, wa2_ref, ba_ref, nw_ref, tri_ref, level_ref, s0_ref,
         h_ref, sout_ref, s_scr) = refs
    t = pl.program_id(2)

    @pl.when(t == 0)
    def _():
        if zero_init:
            s_scr[...] = jnp.zeros_like(s_scr)
        else:
            s_scr[...] = s0_ref[0, 0]

    gate_in = _dot(ga_ref[...].astype(BF16), wa2_ref[...]) + ba_ref[...]
    g = _log_sigmoid(gate_in) * (LOG2_E / GLA_GATE_NORMALIZER)
    b = _chunk_cumsum(g, tri_ref[...])
    dec_col = _block_decay_column(b, chunk)
    level = level_ref[...]
    nw = nw_ref[...]
    for hi in range(heads):
        kc = slice(hi * GLA_DK, (hi + 1) * GLA_DK)
        vc = slice(hi * GLA_DV, (hi + 1) * GLA_DV)
        o, s = _block_recurrence(q_ref[:, kc], k_ref[:, kc], b[:, kc], v_ref[:, vc].astype(BF16),
                                 s_scr[hi], level, dec_col[kc, :], chunk, math.log2(GLA_DK ** -0.5))
        s_scr[hi] = s
        h_ref[:, vc] = (_rms(o, nw) * _silu(z_ref[:, vc])).astype(BF16)

    @pl.when(t == pl.num_programs(2) - 1)
    def _():
        sout_ref[0] = s_scr[...]


def _gla_mix(proj, gate_lr, wa2_bf, b_a, norm_w, states, layer, batch, seq):
    chunk = min(CHUNK, seq)
    tb = min(REC_TOKENS, seq)
    nt = seq // tb
    hb = GLA_HEADS_PER_STEP
    kb = hb * GLA_DK
    vb = hb * GLA_DV
    k_off = GLA_KD // kb
    v_off = (2 * GLA_KD) // vb
    z_off = (2 * GLA_KD + D_INNER) // vb
    zero_init = states is None
    in_specs = [
        pl.BlockSpec((tb, kb), lambda b, h, t: (b * nt + t, h)),
        pl.BlockSpec((tb, kb), lambda b, h, t: (b * nt + t, k_off + h)),
        pl.BlockSpec((tb, vb), lambda b, h, t: (b * nt + t, v_off + h)),
        pl.BlockSpec((tb, vb), lambda b, h, t: (b * nt + t, z_off + h)),
        pl.BlockSpec((tb, V7X_LANES), lambda b, h, t: (b * nt + t, 0)),
        pl.BlockSpec((V7X_LANES, kb), lambda b, h, t: (0, h)),
        pl.BlockSpec((1, kb), lambda b, h, t: (0, h)),
        pl.BlockSpec((1, GLA_DV), lambda b, h, t: (0, 0)),
        pl.BlockSpec((tb, tb), lambda b, h, t: (0, 0)),
        pl.BlockSpec((tb, tb), lambda b, h, t: (0, 0)),
    ]
    args = [proj, proj, proj, proj, gate_lr, wa2_bf, b_a.reshape(1, GLA_KD), norm_w.reshape(1, GLA_DV),
            *_block_constants(tb, chunk)]
    if not zero_init:
        in_specs.append(pl.BlockSpec((1, 1, hb, GLA_DK, GLA_DV), lambda b, h, t: (layer, b, h, 0, 0)))
        args.append(states)
    h, s_out = pl.pallas_call(
        functools.partial(_gla_kernel, chunk=chunk, zero_init=zero_init, heads=hb),
        grid=(batch, GLA_HEADS // hb, nt),
        in_specs=in_specs,
        out_specs=[
            pl.BlockSpec((tb, vb), lambda b, h, t: (b * nt + t, h)),
            pl.BlockSpec((1, hb, GLA_DK, GLA_DV), lambda b, h, t: (b, h, 0, 0)),
        ],
        out_shape=[
            jax.ShapeDtypeStruct((batch * seq, D_INNER), BF16),
            jax.ShapeDtypeStruct((batch, GLA_HEADS, GLA_DK, GLA_DV), F32),
        ],
        scratch_shapes=[pltpu.VMEM((hb, GLA_DK, GLA_DV), F32)],
        compiler_params=_params(("parallel", "parallel", "arbitrary")),
        name="gla_mix",
    )(*args)
    return h, s_out


def _hgrn_kernel(*refs, chunk, zero_init, layer, heads):
    if zero_init:
        (q_ref, f_ref, i_ref, z_ref, lbp_ref, nw_ref, tri_ref, level_ref,
         u_ref, ssq_ref, sout_ref, s_scr) = refs
    else:
        (q_ref, f_ref, i_ref, z_ref, lbp_ref, nw_ref, tri_ref, level_ref, s0_ref,
         u_ref, ssq_ref, sout_ref, s_scr) = refs
    t = pl.program_id(2)

    @pl.when(t == 0)
    def _():
        if zero_init:
            s_scr[...] = jnp.zeros_like(s_scr)
        else:
            s_scr[...] = s0_ref[0, 0]

    lbp = lbp_ref[...]
    lbe = jnp.exp(lbp - jnp.max(lbp, axis=0, keepdims=True))
    lbs = lbe / jnp.sum(lbe, axis=0, keepdims=True)
    lb = jnp.zeros_like(lbs[0:1])
    for r in range(1, layer + 1):
        lb = lb + lbs[r:r + 1]

    q = _silu(q_ref[...])
    fgate = lb + (1.0 - lb) * _sigmoid(f_ref[...])
    k = 1.0 - fgate
    b = _chunk_cumsum(jnp.log(fgate) * LOG2_E, tri_ref[...])
    dec_col = _block_decay_column(b, chunk)
    v_bf = i_ref[...].astype(BF16)
    level = level_ref[...]
    ssq =jnp.zeros((q.shape[0], 1), F32)
    for hi in range(heads):
        cols = slice(hi * HGRN_DK, (hi + 1) * HGRN_DK)
        o, s = _block_recurrence(q[:, cols], k[:, cols], b[:, cols], v_bf[:, cols], s_scr[hi],
                                 level, dec_col[cols, :], chunk, math.log2(HGRN_DK ** -0.5))
        s_scr[hi] = s
        ssq = ssq + jnp.sum(o * o, axis=-1, keepdims=True)
        u_ref[:, cols] = (o * nw_ref[:, cols] * _silu(z_ref[:, cols])).astype(BF16)
    ssq_ref[0] = ssq

    @pl.when(t == pl.num_programs(2) - 1)
    def _():
        sout_ref[0] = s_scr[...]


def _hgrn_mix(proj, lower_bounds, layer, norm_w, states, state_layer, batch, seq):
    chunk = min(CHUNK, seq)
    tb = min(REC_TOKENS, seq)
    nt = seq // tb
    hb = HGRN_HEADS_PER_STEP
    wb = hb * HGRN_DK
    ngroups = HGRN_HEADS // hb
    col_groups = D_INNER // wb
    zero_init = states is None
    in_specs = [
        pl.BlockSpec((tb, wb), lambda b, h, t: (b * nt + t, h)),
        pl.BlockSpec((tb, wb), lambda b, h, t: (b * nt + t, col_groups + h)),
        pl.BlockSpec((tb, wb), lambda b, h, t: (b * nt + t, 2 * col_groups + h)),
        pl.BlockSpec((tb, wb), lambda b, h, t: (b * nt + t, 3 * col_groups + h)),
        pl.BlockSpec((DEPTH, wb), lambda b, h, t: (0, h)),
        pl.BlockSpec((1, wb), lambda b, h, t: (0, h)),
        pl.BlockSpec((tb, tb), lambda b, h, t: (0, 0)),
        pl.BlockSpec((tb, tb), lambda b, h, t: (0, 0)),
    ]
    args = [proj, proj, proj, proj, lower_bounds, norm_w.reshape(1, D_INNER), *_block_constants(tb, chunk)]
    if not zero_init:
        in_specs.append(pl.BlockSpec((1, 1, hb, HGRN_DK, HGRN_DV), lambda b, h, t: (state_layer, b, h, 0, 0)))
        args.append(states)
    u, ssq, s_out = pl.pallas_call(
        functools.partial(_hgrn_kernel, chunk=chunk, zero_init=zero_init, layer=layer, heads=hb),
        grid=(batch, ngroups, nt),
        in_specs=in_specs,
        out_specs=[
            pl.BlockSpec((tb, wb), lambda b, h, t: (b * nt + t, h)),
            pl.BlockSpec((1, tb, 1), lambda b, h, t: (h, b * nt + t, 0)),
            pl.BlockSpec((1, hb, HGRN_DK, HGRN_DV), lambda b, h, t: (b, h, 0, 0)),
        ],
        out_shape=[
            jax.ShapeDtypeStruct((batch * seq, D_INNER), BF16),
            jax.ShapeDtypeStruct((ngroups, batch * seq, 1), F32),
            jax.ShapeDtypeStruct((batch, HGRN_HEADS, HGRN_DK, HGRN_DV), F32),
        ],
        scratch_shapes=[pltpu.VMEM((hb, HGRN_DK, HGRN_DV), F32)],
        compiler_params=_params(("parallel", "parallel", "arbitrary")),
        name="hgrn_mix",
    )(*args)
    return u, ssq, s_out


def _diff_lambda(lam_ref, lam_init):
    lp = lam_ref[...]
    a = jnp.sum(lp[0:1] * lp[1:2], axis=-1, keepdims=True)
    b = jnp.sum(lp[2:3] * lp[3:4], axis=-1, keepdims=True)
    return jnp.exp(a) - jnp.exp(b) + lam_init


def _diff_finish(o1, o2, lam_ref, sw_ref, z, lam_init):
    o = o1 - _diff_lambda(lam_ref, lam_init) * o2
    o = _rms(o, sw_ref[...]) * (1.0 - lam_init)
    return (o * _silu(z)).astype(BF16)


KIND_FIRST, KIND_PLAIN, KIND_MASKED, KIND_DRAIN = 0, 1, 3, 5


def _diff_prompt_kernel(qa_tab, ka_tab, qb_tab, kb_tab, kind_tab, last_tab,
                        q_ref, k_ref, v_ref, z_ref, lam_ref, sw_ref, h_ref, *scratch,
                        bq, bk, rows_per_tile, lam_init):
    per_head = len(scratch) // 2
    heads = []
    for h in range(2):
        s_scr, p0, p1, m_scr, l_scr, a0, a1, acc_scr = scratch[per_head * h:per_head * (h + 1)]
        heads.append((s_scr, (p0, p1), m_scr, l_scr, (a0, a1), acc_scr))
    p = pl.program_id(2)
    qi = qa_tab[p]
    ki = ka_tab[p]
    par_a = qi % 2
    par_b = qb_tab[p] % 2
    first = ki == 0
    reps = (1, 2 * DIFF_HD // V7X_LANES)

    def stage_a(masked, slot):
        for h, (s_scr, p_scr, m_scr, l_scr, a_scr, acc_scr) in enumerate(heads):
            cols = slice(h * DIFF_HD, (h + 1) * DIFF_HD)
            s_scr[...] = _dot_nt(q_ref[0, :, cols], k_ref[0, :, cols])
        for s_scr, p_scr, m_scr, l_scr, a_scr, acc_scr in heads:
            for r in range(bq // rows_per_tile):
                rows = pl.ds(r * rows_per_tile, rows_per_tile)
                s = s_scr[rows, :]
                if masked:
                    shape = (rows_per_tile, bk)
                    q_chunk = (qi * bq + r * rows_per_tile + lax.broadcasted_iota(jnp.int32, shape, 0)) // CHUNK
                    k_chunk = (ki * bk + lax.broadcasted_iota(jnp.int32, shape, 1)) // CHUNK
                    s = jnp.where(k_chunk <= q_chunk, s, -jnp.inf)
                m_prev = jnp.where(first, -jnp.inf, m_scr[rows, :])
                m_new = jnp.maximum(m_prev, jnp.max(s, axis=-1, keepdims=True))
                alpha = jnp.exp2(m_prev - m_new)
                pr = jnp.exp2(s - jnp.tile(m_new, (1, bk // V7X_LANES)))
                l_scr[par_a, rows, :] = alpha * l_scr[par_a, rows, :] + jnp.sum(pr, axis=-1, keepdims=True)
                m_scr[rows, :] = m_new
                a_scr[slot][rows, :] = alpha
                p_scr[slot][rows, :] = pr.astype(BF16)

    def stage_b(slot):
        for s_scr, p_scr, m_scr, l_scr, a_scr, acc_scr in heads:
            alpha = jnp.tile(a_scr[slot][...], reps)
            acc_scr[...] = alpha * acc_scr[...] + _dot(p_scr[slot][...], v_ref[0])

    kind = kind_tab[p]

    @pl.when(kind == KIND_FIRST)
    def _():
        for s_scr, p_scr, m_scr, l_scr, a_scr, acc_scr in heads:
            m_scr[...] = jnp.full_like(m_scr, -jnp.inf)
            l_scr[...] = jnp.zeros_like(l_scr)
            acc_scr[...] = jnp.zeros_like(acc_scr)
        stage_a(True, 0)

    for slot in range(2):
        @pl.when(kind == KIND_PLAIN + slot)
        def _():
            stage_a(False, slot)
            stage_b(1 - slot)

        @pl.when(kind == KIND_MASKED + slot)
        def _():
            stage_a(True, slot)
            stage_b(1 - slot)

        @pl.when(kind == KIND_DRAIN + slot)
        def _():
            stage_b(1 - slot)

    @pl.when(last_tab[p] == 1)
    def _():
        o1, o2 = [acc_scr[...] * jnp.tile(1.0 / l_scr[par_b], reps)
                  for _, _, _, l_scr, _, acc_scr in heads]
        h_ref[...] = _diff_finish(o1, o2, lam_ref, sw_ref, z_ref[...], lam_init)


def _diff_prompt_tables(seq, bq, bk):
    pairs = []
    for qi in range(seq // bq):
        q_lo, q_hi = qi * bq, (qi + 1) * bq
        nk = -(-q_hi // bk)
        for ki in range(nk):
            fully_visible = (ki + 1) * bk <= (q_lo // CHUNK + 1) * CHUNK
            pairs.append((qi, ki, not fully_visible, ki == nk - 1))
    assert pairs[0][2], "the first pair sits on the diagonal"
    n = len(pairs)
    qa = [pairs[min(p, n - 1)][0] for p in range(n + 1)]
    ka = [pairs[min(p, n - 1)][1] for p in range(n + 1)]
    qb = [pairs[max(p - 1, 0)][0] for p in range(n + 1)]
    kb = [pairs[max(p - 1, 0)][1] for p in range(n + 1)]
    kind = ([KIND_FIRST] + [(KIND_MASKED if pairs[p][2] else KIND_PLAIN) + p % 2 for p in range(1, n)]
            + [KIND_DRAIN + n % 2])
    last = [0] + [1 if pairs[p - 1][3] else 0 for p in range(1, n + 1)]
    return [jnp.asarray(np.asarray(t, np.int32)) for t in (qa, ka, qb, kb, kind, last)]


def _diff_prompt_attn(qkv, z, lam_p, subln_w, lam_init, batch, seq):
    bq = min(ATT_BQ, seq)
    bk = min(ATT_BK, seq)
    nq = seq // bq
    nkb = seq // bk
    pw = 2 * DIFF_HD
    tabs = _diff_prompt_tables(seq, bq, bk)
    n_steps = int(tabs[0].shape[0])

    def spec(shape, index):
        return pl.BlockSpec(shape, lambda b, h, p, qa, ka, qb, kb, kind, last: index(b, h, p, qa, ka, qb, kb))

    grid_spec = pltpu.PrefetchScalarGridSpec(
        num_scalar_prefetch=6,
        grid=(batch, DIFF_HEADS, n_steps),
        in_specs=[
            spec((1, bq, pw), lambda b, h, p, qa, ka, qb, kb: (h, b * nq + qa[p], 0)),
            spec((1, bk, pw), lambda b, h, p, qa, ka, qb, kb: (DIFF_HEADS + h, b * nkb + ka[p], 0)),
            spec((1, bk, pw), lambda b, h, p, qa, ka, qb, kb: (2 * DIFF_HEADS + h, b * nkb + kb[p], 0)),
            spec((bq, pw), lambda b, h, p, qa, ka, qb, kb: (b * nq + qb[p], h)),
            spec((4, DIFF_HD), lambda b, h, p, qa, ka, qb, kb: (0, 0)),
            spec((1, pw), lambda b, h, p, qa, ka, qb, kb: (0, 0)),
        ],
        out_specs=spec((bq, pw), lambda b, h, p, qa, ka, qb, kb: (b * nq + qb[p], h)),
        scratch_shapes=2 * [
            pltpu.VMEM((bq, bk), F32),
            pltpu.VMEM((bq, bk), BF16),
            pltpu.VMEM((bq, bk), BF16),
            pltpu.VMEM((bq, V7X_LANES), F32),
            pltpu.VMEM((2, bq, V7X_LANES), F32),
            pltpu.VMEM((bq, V7X_LANES), F32),
            pltpu.VMEM((bq, V7X_LANES), F32),
            pltpu.VMEM((bq, pw), F32),
        ],
    )
    return pl.pallas_call(
        functools.partial(_diff_prompt_kernel, bq=bq, bk=bk, rows_per_tile=min(ATT_ROWS, bq),
                          lam_init=lam_init),
        grid_spec=grid_spec,
        out_shape=jax.ShapeDtypeStruct((batch * seq, D_INNER), BF16),
        compiler_params=_params(("parallel", "parallel", "arbitrary")),
        name="diff_prompt_attn",
    )(*tabs, qkv, qkv, qkv, z, lam_p, subln_w.reshape(1, pw))


def _diff_sample_kernel(qkv_ref, z_ref, kc_ref, vc_ref, lam_ref, sw_ref, h_ref, *, lam_init):
    pw = 2 * DIFF_HD
    for pair in range(DIFF_HEADS):
        pcols = slice(pair * pw, (pair + 1) * pw)
        vc_bf = vc_ref[:, pcols].astype(BF16)
        vn_bf = qkv_ref[2 * DIFF_HEADS + pair]
        outs = []
        for h in range(2):
            cols = slice(h * DIFF_HD, (h + 1) * DIFF_HD)
            ccols = slice(pair * pw + h * DIFF_HD, pair * pw + (h + 1) * DIFF_HD)
            q_bf = qkv_ref[pair, :, cols]
            s_c = _dot_nt(q_bf, kc_ref[:, ccols].astype(BF16))
            s_n = _dot_nt(q_bf, qkv_ref[DIFF_HEADS + pair, :, cols])
            m = jnp.maximum(jnp.max(s_c, axis=-1, keepdims=True), jnp.max(s_n, axis=-1, keepdims=True))
            p_c = jnp.exp2(s_c - m)
            p_n = jnp.exp2(s_n - m)
            l = jnp.sum(p_c, axis=-1, keepdims=True) + jnp.sum(p_n, axis=-1, keepdims=True)
            acc = _dot(p_c.astype(BF16), vc_bf) + _dot(p_n.astype(BF16), vn_bf)
            outs.append(acc * (1.0 / l))
        h_ref[:, pcols] = _diff_finish(outs[0], outs[1], lam_ref, sw_ref, z_ref[:, pcols], lam_init)


def _diff_sample_attn(qkv, z, cache_k2, cache_v2, lam_p, subln_w, lam_init, batch, seq, past):
    pw = 2 * DIFF_HD
    return pl.pallas_call(
        functools.partial(_diff_sample_kernel, lam_init=lam_init),
        grid=(batch,),
        in_specs=[
            pl.BlockSpec((3 * DIFF_HEADS, seq, pw), lambda b: (0, b, 0)),
            pl.BlockSpec((seq, D_INNER), lambda b: (b, 0)),
            pl.BlockSpec((past, D_INNER), lambda b: (b, 0)),
            pl.BlockSpec((past, D_INNER), lambda b: (b, 0)),
            pl.BlockSpec((4, DIFF_HD), lambda b: (0, 0)),
            pl.BlockSpec((1, pw), lambda b: (0, 0)),
        ],
        out_specs=pl.BlockSpec((seq, D_INNER), lambda b: (b, 0)),
        out_shape=jax.ShapeDtypeStruct((batch * seq, D_INNER), BF16),
        compiler_params=_params(("parallel",)),
        name="diff_sample_attn",
    )(qkv, z, cache_k2, cache_v2, lam_p, subln_w.reshape(1, pw))


def kernel(x_prompt, x_sample, state_gla, state_hgrn, cache_k, cache_v, norm_w, final_norm_w,
           gla_w_in, gla_w_a1, gla_w_a2, gla_b_a, gla_norm_w, gla_w_out,
           hgrn_w_in, hgrn_lower_bounds, hgrn_norm_w, hgrn_w_out,
           diff_w_in, diff_lambda, diff_subln_w, diff_w_out):
    bp, tp, d = x_prompt.shape
    bs, ts, _ = x_sample.shape
    past = cache_k.shape[2]
    streams = [(bp, tp), (bs, ts)]
    xs = [x_prompt.reshape(bp * tp, d), x_sample.reshape(bs * ts, d)]

    def out_proj(h, res, w_stack, j, i, ssq=None):
        last = i == DEPTH - 1
        return _out_proj(h, res, w_stack, j, final_norm_w if last else norm_w[i + 1], last, ssq)

    gla_w_in_bf, gla_w_out_bf = gla_w_in.astype(BF16), gla_w_out.astype(BF16)
    hgrn_w_in_bf, hgrn_w_out_bf = hgrn_w_in.astype(BF16), hgrn_w_out.astype(BF16)
    diff_w_in_bf, diff_w_out_bf = diff_w_in.astype(BF16), diff_w_out.astype(BF16)

    hn = [None, None]
    gla_states = [[], []]
    hgrn_states = [[], []]
    k_rows = [[], []]
    v_rows = [[], []]
    ia = ib = ic = 0
    for i in range(DEPTH):
        kind = i % N_MIXERS
        if kind == 0:
            j = ia
            ia += 1
            w_a1 = jnp.pad(gla_w_a1[j], ((0, 0), (0, V7X_LANES - GLA_GATE_RANK))).astype(BF16)
            w_a2 = jnp.pad(gla_w_a2[j], ((0, V7X_LANES - GLA_GATE_RANK), (0, 0))).astype(BF16)
            for si, (batch, seq) in enumerate(streams):
                if i == 0:
                    proj, gate_lr = _norm_proj(xs[si], norm_w[i], gla_w_in_bf, j, w_a1)
                else:
                    proj, gate_lr = _in_proj(hn[si], gla_w_in_bf, j, w_a1)
                states = None if si == 0 else state_gla
                h, s_new = _gla_mix(proj, gate_lr, w_a2, gla_b_a[j], gla_norm_w[j], states, j, batch, seq)
                xs[si], hn[si] = out_proj(h, xs[si], gla_w_out_bf, j, i)
                gla_states[si].append(s_new)
        elif kind == 1:
            j = ib
            ib += 1
            for si, (batch, seq) in enumerate(streams):
                proj = _in_proj(hn[si], hgrn_w_in_bf, j)
                states = None if si == 0 else state_hgrn
                u, ssq, s_new = _hgrn_mix(proj, hgrn_lower_bounds, i, hgrn_norm_w[j], states, j, batch, seq)
                xs[si], hn[si] = out_proj(u, xs[si], hgrn_w_out_bf, j, i, ssq)
                hgrn_states[si].append(s_new)
        else:
            j = ic
            ic += 1
            lam_init = 0.8 - 0.6 * math.exp(-0.3 * i)
            for si, (batch, seq) in enumerate(streams):
                qkv, k_new, v_new, z = _diff_proj(hn[si], diff_w_in_bf, j)
                if si == 0:
                    h = _diff_prompt_attn(qkv, z, diff_lambda[j], diff_subln_w[j], lam_init, batch, seq)
                else:
                    ck = cache_k[j].reshape(batch * past, 2 * DIFF_HEADS * DIFF_HD)
                    cv = cache_v[j].reshape(batch * past, 2 * DIFF_HEADS * DIFF_HD)
                    h = _diff_sample_attn(qkv, z, ck, cv, diff_lambda[j], diff_subln_w[j], lam_init,
                                          batch, seq, past)
                xs[si], hn[si] = out_proj(h, xs[si], diff_w_out_bf, j, i)
                k_rows[si].append(k_new.reshape(batch, seq, 2 * DIFF_HEADS, DIFF_HD))
                v_rows[si].append(v_new.reshape(batch, seq, DIFF_HEADS, 2 * DIFF_HD))

    return (hn[0].reshape(bp, tp, d), hn[1].reshape(bs, ts, d),
            jnp.stack(gla_states[0]), jnp.stack(gla_states[1]),
            jnp.stack(hgrn_states[0]), jnp.stack(hgrn_states[1]),
            jnp.stack(k_rows[0]), jnp.stack(v_rows[0]),
            jnp.stack(k_rows[1]), jnp.stack(v_rows[1]))
```

```python
import functools
import math

import numpy as np
import jax
import jax.numpy as jnp
from jax import lax
from jax.experimental import pallas as pl
from jax.experimental.pallas import tpu as pltpu

D_MODEL = 2048
DEPTH = 4
CHUNK = 64
N_MIXERS = 3
D_INNER = D_MODEL
NORM_EPS = 1e-6
GLA_HEADS = 4
GLA_KD = D_INNER // 2
GLA_DK = GLA_KD // GLA_HEADS
GLA_DV = D_INNER // GLA_HEADS
GLA_GATE_RANK = 16
GLA_GATE_NORMALIZER = 16.0
HGRN_HEADS = 16
HGRN_DK = 128
HGRN_DV = 128
DIFF_HD = 128
DIFF_HEADS = 8

LOG2_E = math.log2(math.e)
DIFF_Q_SCALE = (DIFF_HD ** -0.5) * LOG2_E

F32 = jnp.float32
BF16 = jnp.bfloat16

V7X_LANES = 128
V7X_VMEM_BYTES = 64 * 1024 * 1024
VMEM_LIMIT_BYTES = (V7X_VMEM_BYTES * 3) // 4
VMEM_LIMIT_WIDE_BYTES = (V7X_VMEM_BYTES * 7) // 8

PROJ_TM = 1024
PROJ_TN = 1024
IN_TN = 2048
DIFF_TN = 1024
OUT_TM = 512
REC_TOKENS = 256
GLA_HEADS_PER_STEP = 4
HGRN_HEADS_PER_STEP = 16
ATT_BQ = 1024
ATT_BK = 1024
ATT_ROWS = 128


def _params(sem, vmem_limit=VMEM_LIMIT_BYTES):
    return pltpu.CompilerParams(dimension_semantics=sem, vmem_limit_bytes=vmem_limit)


def _dot(a, b):
    return jnp.dot(a, b, preferred_element_type=F32)


def _dot_nt(a, b):
    return lax.dot_general(a, b, (((1,), (1,)), ((), ())), preferred_element_type=F32)


def _dot_tn(a, b):
    return lax.dot_general(a, b, (((0,), (0,)), ((), ())), preferred_element_type=F32)


def _sigmoid(x):
    return 1.0 / (1.0 + jnp.exp(-x))


def _silu(x):
    return x * _sigmoid(x)


def _rms(x, w):
    ms = jnp.mean(x * x, axis=-1, keepdims=True)
    return x * lax.rsqrt(ms + NORM_EPS) * w


def _norm_proj_kernel(*refs, has_aux):
    if has_aux:
        x_ref, nw_ref, w_ref, aw_ref, o_ref, ao_ref, h_scr = refs
    else:
        x_ref, nw_ref, w_ref, o_ref, h_scr = refs

    @pl.when(pl.program_id(1) == 0)
    def _():
        h_scr[...] = _rms(x_ref[...], nw_ref[...]).astype(BF16)
        if has_aux:
            ao_ref[...] = _dot(h_scr[...], aw_ref[...])

    o_ref[...] = _dot(h_scr[...], w_ref[0])


def _norm_proj(x, nw, w_stack, layer, aux_w_bf=None):
    m, d = x.shape
    n = w_stack.shape[2]
    tm = min(PROJ_TM, m)
    tn = PROJ_TN
    in_specs = [
        pl.BlockSpec((tm, d), lambda i, j: (i, 0)),
        pl.BlockSpec((1, d), lambda i, j: (0, 0)),
        pl.BlockSpec((1, d, tn), lambda i, j: (layer, 0, j)),
    ]
    out_shape = [jax.ShapeDtypeStruct((m, n), F32)]
    out_specs = [pl.BlockSpec((tm, tn), lambda i, j: (i, j))]
    args = [x, nw.reshape(1, d), w_stack]
    if aux_w_bf is not None:
        na = aux_w_bf.shape[1]
        in_specs.append(pl.BlockSpec((d, na), lambda i, j: (0, 0)))
        out_shape.append(jax.ShapeDtypeStruct((m, na), F32))
        out_specs.append(pl.BlockSpec((tm, na), lambda i, j: (i, 0)))
        args.append(aux_w_bf)
    outs = pl.pallas_call(
        functools.partial(_norm_proj_kernel, has_aux=aux_w_bf is not None),
        grid=(m // tm, n // tn),
        in_specs=in_specs,
        out_specs=out_specs,
        out_shape=out_shape,
        scratch_shapes=[pltpu.VMEM((tm, d), BF16)],
        compiler_params=_params(("parallel", "arbitrary")),
        name="norm_proj",
    )(*args)
    return outs if aux_w_bf is not None else outs[0]


def _in_proj_kernel(*refs, has_aux):
    if has_aux:
        h_ref, w_ref, aw_ref, o_ref, ao_ref = refs

        @pl.when(pl.program_id(1) == 0)
        def _():
            ao_ref[...] = _dot(h_ref[...], aw_ref[...])
    else:
        h_ref, w_ref, o_ref = refs
    o_ref[...] = _dot(h_ref[...], w_ref[0])


def _in_proj(hn_bf, w_stack, layer, aux_w_bf=None):
    m, d = hn_bf.shape
    n = w_stack.shape[2]
    tm = min(PROJ_TM, m)
    tn = IN_TN
    in_specs = [
        pl.BlockSpec((tm, d), lambda i, j: (i, 0)),
        pl.BlockSpec((1, d, tn), lambda i, j: (layer, 0, j)),
    ]
    out_shape = [jax.ShapeDtypeStruct((m, n), F32)]
    out_specs = [pl.BlockSpec((tm, tn), lambda i, j: (i, j))]
    args = [hn_bf, w_stack]
    if aux_w_bf is not None:
        na = aux_w_bf.shape[1]
        in_specs.append(pl.BlockSpec((d, na), lambda i, j: (0, 0)))
        out_shape.append(jax.ShapeDtypeStruct((m, na), F32))
        out_specs.append(pl.BlockSpec((tm, na), lambda i, j: (i, 0)))
        args.append(aux_w_bf)
    outs = pl.pallas_call(
        functools.partial(_in_proj_kernel, has_aux=aux_w_bf is not None),
        grid=(m // tm, n // tn),
        in_specs=in_specs,
        out_specs=out_specs,
        out_shape=out_shape,
        compiler_params=_params(("parallel", "arbitrary")),
        name="in_proj",
    )(*args)
    return outs if aux_w_bf is not None else outs[0]


def _diff_proj_kernel(h_ref, w_ref, pm_ref, k_ref, v_ref, z_ref, *, group_tiles):
    j = pl.program_id(1)
    acc = _dot(h_ref[...], w_ref[0])
    pw = pm_ref.shape[2]

    def write_pairs(val):
        for p in range(pm_ref.shape[0]):
            pm_ref[p] = val[:, p * pw:(p + 1) * pw].astype(BF16)

    @pl.when(j < group_tiles)
    def _():
        write_pairs(acc * DIFF_Q_SCALE)

    @pl.when((j >= group_tiles) & (j < 2 * group_tiles))
    def _():
        write_pairs(acc)
        k_ref[...] = acc

    @pl.when((j >= 2 * group_tiles) & (j < 3 * group_tiles))
    def _():
        write_pairs(acc)
        v_ref[...] = acc

    @pl.when(j >= 3 * group_tiles)
    def _():
        z_ref[...] = acc


def _diff_proj(hn_bf, w_stack, layer):
    m, d = hn_bf.shape
    tm = min(PROJ_TM, m)
    tn = DIFF_TN
    pw = 2 * DIFF_HD
    g = D_INNER // tn
    ppt = tn // pw

    def rows_spec(group):
        return pl.BlockSpec((tm, tn), lambda i, j: (i, jnp.clip(j - group * g, 0, g - 1)))

    return pl.pallas_call(
        functools.partial(_diff_proj_kernel, group_tiles=g),
        grid=(m // tm, 4 * g),
        in_specs=[
            pl.BlockSpec((tm, d), lambda i, j: (i, 0)),
            pl.BlockSpec((1, d, tn), lambda i, j: (layer, 0, j)),
        ],
        out_specs=[
            pl.BlockSpec((ppt, tm, pw), lambda i, j: (jnp.minimum(j, 3 * g - 1), i, 0)),
            rows_spec(1), rows_spec(2), rows_spec(3),
        ],
        out_shape=[
            jax.ShapeDtypeStruct((3 * DIFF_HEADS, m, pw), BF16),
            jax.ShapeDtypeStruct((m, D_INNER), F32),
            jax.ShapeDtypeStruct((m, D_INNER), F32),
            jax.ShapeDtypeStruct((m, D_INNER), F32),
        ],
        compiler_params=_params(("parallel", "arbitrary"), VMEM_LIMIT_WIDE_BYTES),
        name="diff_proj",
    )(hn_bf, w_stack)


def _row_rms_scale(ssq_ref):
    ssq = ssq_ref[0]
    for gi in range(1, ssq_ref.shape[0]):
        ssq = ssq + ssq_ref[gi]
    return lax.rsqrt(ssq * (1.0 / D_INNER) + NORM_EPS)


def _out_proj_kernel(*refs, has_ssq, emit_x):
    h_ref = refs[0]
    ssq_ref = refs[1] if has_ssq else None
    res_ref, w_ref, nw_ref = refs[1 + has_ssq:4 + has_ssq]
    outs = refs[4 + has_ssq:]
    acc = _dot(h_ref[...], w_ref[0])
    if has_ssq:
        acc = acc * _row_rms_scale(ssq_ref)
    x_new = res_ref[...] + acc
    if emit_x:
        outs[0][...] = x_new
    outs[-1][...] = _rms(x_new, nw_ref[...]).astype(outs[-1].dtype)


def _out_proj(h_bf, res, w_stack, layer, norm_w, last, ssq=None):
    m, d = h_bf.shape
    n = w_stack.shape[2]
    tm = min(OUT_TM, m)
    in_specs = [pl.BlockSpec((tm, d), lambda i: (i, 0))]
    args = [h_bf]
    if ssq is not None:
        in_specs.append(pl.BlockSpec((ssq.shape[0], tm, 1), lambda i: (0, i, 0)))
        args.append(ssq)
    in_specs += [
        pl.BlockSpec((tm, n), lambda i: (i, 0)),
        pl.BlockSpec((1, d, n), lambda i: (layer, 0, 0)),
        pl.BlockSpec((1, n), lambda i: (0, 0)),
    ]
    args += [res, w_stack, norm_w.reshape(1, n)]
    out_shape = [jax.ShapeDtypeStruct((m, n), F32 if last else BF16)]
    if not last:
        out_shape.insert(0, jax.ShapeDtypeStruct((m, n), F32))
    outs = pl.pallas_call(
        functools.partial(_out_proj_kernel, has_ssq=ssq is not None, emit_x=not last),
        grid=(m // tm,),
        in_specs=in_specs,
        out_specs=[pl.BlockSpec((tm, n), lambda i: (i, 0)) for _ in out_shape],
        out_shape=out_shape,
        compiler_params=_params(("parallel",)),
        name="out_proj",
    )(*args)
    return (None, outs[0]) if last else (outs[0], outs[1])


def _block_constants(tb, chunk):
    n = tb // chunk
    assert n & (n - 1) == 0, "chunks per block must be a power of two"
    row = np.arange(tb)[:, None]
    col = np.arange(tb)[None, :]
    rc, cc = row // chunk, col // chunk
    same = (rc == cc) & (row >= col)
    level = np.where(same, 0, -1)
    for lv in range(1, n.bit_length()):
        group = 1 << lv
        meet = (rc > cc) & (rc // group == cc // group) & (rc // (group // 2) != cc // (group // 2))
        level = np.where(meet, lv, level)
    return jnp.asarray(same.astype(np.float32), BF16), jnp.asarray(level.astype(np.int32))


def _chunk_cumsum(g, tri_bf):
    g_hi = g.astype(BF16)
    g_lo = (g - g_hi.astype(F32)).astype(BF16)
    return _dot(tri_bf, g_hi) + _dot(tri_bf, g_lo)


def _per_chunk(rows, chunk):
    parts = [jnp.broadcast_to(r, (chunk, r.shape[1])) for r in rows]
    return parts[0] if len(parts) == 1 else jnp.concatenate(parts, axis=0)


def _sum_rows(rows, width):
    out = jnp.zeros((1, width), F32)
    for r in rows:
        out = out + r
    return out


def _block_decay_column(b, chunk):
    tb, w = b.shape
    total = _sum_rows([b[(c + 1) * chunk - 1:(c + 1) * chunk, :] for c in range(tb // chunk)], w)
    return jnp.transpose(jnp.broadcast_to(jnp.exp2(total), (V7X_LANES, w)))[:, 0:1]


def _block_recurrence(q, k, b, v_bf, s0, level, dec_col, chunk, q_log2_scale):
    w = q.shape[1]
    n = q.shape[0] // chunk
    mids = [b[c * chunk + chunk // 2 - 1:c * chunk + chunk // 2, :] for c in range(n)]
    lasts = [b[(c + 1) * chunk - 1:(c + 1) * chunk, :] for c in range(n)]
    q_mid = q * jnp.exp2(b - _per_chunk([m - q_log2_scale for m in mids], chunk))
    k_mid = k * jnp.exp2(_per_chunk(mids, chunk) - b)
    q_in = q_mid * _per_chunk([jnp.exp2(m) for m in mids], chunk)
    k_out = k_mid * _per_chunk([jnp.exp2(l - m) for l, m in zip(lasts, mids)], chunk)
    attn = jnp.where(level == 0, _dot_nt(q_mid.astype(BF16), k_mid.astype(BF16)), 0.0)
    zero_row = jnp.zeros((1, w), F32)
    for lv in range(1, n.bit_length()):
        group, half = 1 << lv, 1 << (lv - 1)
        if half == 1:
            q_l, k_l = q_in, k_out
        else:
            mid_of = lambda c: (c // group) * group + half
            q_l = q_in * _per_chunk([jnp.exp2(_sum_rows(lasts[mid_of(c):c], w)) if c % group >= half
                                     else zero_row for c in range(n)], chunk)
            k_l = k_out * _per_chunk([jnp.exp2(_sum_rows(lasts[j + 1:mid_of(j)], w)) if j % group < half
                                      else zero_row for j in range(n)], chunk)
        attn = jnp.where(level == lv, _dot_nt(q_l.astype(BF16), k_l.astype(BF16)), attn)
    head = _per_chunk([jnp.exp2(_sum_rows(lasts[:c], w)) for c in range(n)], chunk)
    tail = _per_chunk([jnp.exp2(_sum_rows(lasts[c + 1:], w)) for c in range(n)], chunk)
    o = _dot((q_in * head).astype(BF16), s0.astype(BF16)) + _dot(attn.astype(BF16), v_bf)
    s_end = dec_col * s0 + _dot_tn((k_out * tail).astype(BF16), v_bf)
    return o, s_end


def _log_sigmoid(x):
    return jnp.minimum(x, 0.0) - jnp.log(1.0 + jnp.exp(-jnp.abs(x)))


def _gla_kernel(*refs, chunk, zero_init, heads):
    if zero_init:
        (q_ref, k_ref, v_ref, z_ref, ga_ref, wa2_ref, ba_ref, nw_ref, tri_ref, level_ref,
         h_ref, sout_ref, s_scr) = refs
    else:
        (q_ref, k_ref, v_ref, z_ref, ga_ref, wa2_ref, ba_ref, nw_ref, tri_ref, level_ref, s0_ref,
         h_ref, sout_ref, s_scr) = refs
    t = pl.program_id(2)

    @pl.when(t == 0)
    def _():
        if zero_init:
            s_scr[...] = jnp.zeros_like(s_scr)
        else:
            s_scr[...] = s0_ref[0, 0]

    gate_in = _dot(ga_ref[...].astype(BF16), wa2_ref[...]) + ba_ref[...]
    g = _log_sigmoid(gate_in) * (LOG2_E / GLA_GATE_NORMALIZER)
    b = _chunk_cumsum(g, tri_ref[...])
    dec_col = _block_decay_column(b, chunk)
    level = level_ref[...]
    nw = nw_ref[...]
    for hi in range(heads):
        kc = slice(hi * GLA_DK, (hi + 1) * GLA_DK)
        vc = slice(hi * GLA_DV, (hi + 1) * GLA_DV)
        o, s = _block_recurrence(q_ref[:, kc], k_ref[:, kc], b[:, kc], v_ref[:, vc].astype(BF16),
                                 s_scr[hi], level, dec_col[kc, :], chunk, math.log2(GLA_DK ** -0.5))
        s_scr[hi] = s
        h_ref[:, vc] = (_rms(o, nw) * _silu(z_ref[:, vc])).astype(BF16)

    @pl.when(t == pl.num_programs(2) - 1)
    def _():
        sout_ref[0] = s_scr[...]


def _gla_mix(proj, gate_lr, wa2_bf, b_a, norm_w, states, layer, batch, seq):
    chunk = min(CHUNK, seq)
    tb = min(REC_TOKENS, seq)
    nt = seq // tb
    hb = GLA_HEADS_PER_STEP
    kb = hb * GLA_DK
    vb = hb * GLA_DV
    k_off = GLA_KD // kb
    v_off = (2 * GLA_KD) // vb
    z_off = (2 * GLA_KD + D_INNER) // vb
    zero_init = states is None
    in_specs = [
        pl.BlockSpec((tb, kb), lambda b, h, t: (b * nt + t, h)),
        pl.BlockSpec((tb, kb), lambda b, h, t: (b * nt + t, k_off + h)),
        pl.BlockSpec((tb, vb), lambda b, h, t: (b * nt + t, v_off + h)),
        pl.BlockSpec((tb, vb), lambda b, h, t: (b * nt + t, z_off + h)),
        pl.BlockSpec((tb, V7X_LANES), lambda b, h, t: (b * nt + t, 0)),
        pl.BlockSpec((V7X_LANES, kb), lambda b, h, t: (0, h)),
        pl.BlockSpec((1, kb), lambda b, h, t: (0, h)),
        pl.BlockSpec((1, GLA_DV), lambda b, h, t: (0, 0)),
        pl.BlockSpec((tb, tb), lambda b, h, t: (0, 0)),
        pl.BlockSpec((tb, tb), lambda b, h, t: (0, 0)),
    ]
    args = [proj, proj, proj, proj, gate_lr, wa2_bf, b_a.reshape(1, GLA_KD), norm_w.reshape(1, GLA_DV),
            *_block_constants(tb, chunk)]
    if not zero_init:
        in_specs.append(pl.BlockSpec((1, 1, hb, GLA_DK, GLA_DV), lambda b, h, t: (layer, b, h, 0, 0)))
        args.append(states)
    h, s_out = pl.pallas_call(
        functools.partial(_gla_kernel, chunk=chunk, zero_init=zero_init, heads=hb),
        grid=(batch, GLA_HEADS // hb, nt),
        in_specs=in_specs,
        out_specs=[
            pl.BlockSpec((tb, vb), lambda b, h, t: (b * nt + t, h)),
            pl.BlockSpec((1, hb, GLA_DK, GLA_DV), lambda b, h, t: (b, h, 0, 0)),
        ],
        out_shape=[
            jax.ShapeDtypeStruct((batch * seq, D_INNER), BF16),
            jax.ShapeDtypeStruct((batch, GLA_HEADS, GLA_DK, GLA_DV), F32),
        ],
        scratch_shapes=[pltpu.VMEM((hb, GLA_DK, GLA_DV), F32)],
        compiler_params=_params(("parallel", "parallel", "arbitrary")),
        name="gla_mix",
    )(*args)
    return h, s_out


def _hgrn_kernel(*refs, chunk, zero_init, layer, heads):
    if zero_init:
        (q_ref, f_ref, i_ref, z_ref, lbp_ref, nw_ref, tri_ref, level_ref,
         u_ref, ssq_ref, sout_ref, s_scr) = refs
    else:
        (q_ref, f_ref, i_ref, z_ref, lbp_ref, nw_ref, tri_ref, level_ref, s0_ref,
         u_ref, ssq_ref, sout_ref, s_scr) = refs
    t = pl.program_id(2)

    @pl.when(t == 0)
    def _():
        if zero_init:
            s_scr[...] = jnp.zeros_like(s_scr)
        else:
            s_scr[...] = s0_ref[0, 0]

    lbp = lbp_ref[...]
    lbe = jnp.exp(lbp - jnp.max(lbp, axis=0, keepdims=True))
    lbs = lbe / jnp.sum(lbe, axis=0, keepdims=True)
    lb = jnp.zeros_like(lbs[0:1])
    for r in range(1, layer + 1):
        lb = lb + lbs[r:r + 1]

    q = _silu(q_ref[...])
    fgate = lb + (1.0 - lb) * _sigmoid(f_ref[...])
    k = 1.0 - fgate
    b = _chunk_cumsum(jnp.log(fgate) * LOG2_E, tri_ref[...])
    dec_col = _block_decay_column(b, chunk)
    v_bf = i_ref[...].astype(BF16)
    level = level_ref[...]
    ssq =jnp.zeros((q.shape[0], 1), F32)
    for hi in range(heads):
        cols = slice(hi * HGRN_DK, (hi + 1) * HGRN_DK)
        o, s = _block_recurrence(q[:, cols], k[:, cols], b[:, cols], v_bf[:, cols], s_scr[hi],
                                 level, dec_col[cols, :], chunk, math.log2(HGRN_DK ** -0.5))
        s_scr[hi] = s
        ssq = ssq + jnp.sum(o * o, axis=-1, keepdims=True)
        u_ref[:, cols] = (o * nw_ref[:, cols] * _silu(z_ref[:, cols])).astype(BF16)
    ssq_ref[0] = ssq

    @pl.when(t == pl.num_programs(2) - 1)
    def _():
        sout_ref[0] = s_scr[...]


def _hgrn_mix(proj, lower_bounds, layer, norm_w, states, state_layer, batch, seq):
    chunk = min(CHUNK, seq)
    tb = min(REC_TOKENS, seq)
    nt = seq // tb
    hb = HGRN_HEADS_PER_STEP
    wb = hb * HGRN_DK
    ngroups = HGRN_HEADS // hb
    col_groups = D_INNER // wb
    zero_init = states is None
    in_specs = [
        pl.BlockSpec((tb, wb), lambda b, h, t: (b * nt + t, h)),
        pl.BlockSpec((tb, wb), lambda b, h, t: (b * nt + t, col_groups + h)),
        pl.BlockSpec((tb, wb), lambda b, h, t: (b * nt + t, 2 * col_groups + h)),
        pl.BlockSpec((tb, wb), lambda b, h, t: (b * nt + t, 3 * col_groups + h)),
        pl.BlockSpec((DEPTH, wb), lambda b, h, t: (0, h)),
        pl.BlockSpec((1, wb), lambda b, h, t: (0, h)),
        pl.BlockSpec((tb, tb), lambda b, h, t: (0, 0)),
        pl.BlockSpec((tb, tb), lambda b, h, t: (0, 0)),
    ]
    args = [proj, proj, proj, proj, lower_bounds, norm_w.reshape(1, D_INNER), *_block_constants(tb, chunk)]
    if not zero_init:
        in_specs.append(pl.BlockSpec((1, 1, hb, HGRN_DK, HGRN_DV), lambda b, h, t: (state_layer, b, h, 0, 0)))
        args.append(states)
    u, ssq, s_out = pl.pallas_call(
        functools.partial(_hgrn_kernel, chunk=chunk, zero_init=zero_init, layer=layer, heads=hb),
        grid=(batch, ngroups, nt),
        in_specs=in_specs,
        out_specs=[
            pl.BlockSpec((tb, wb), lambda b, h, t: (b * nt + t, h)),
            pl.BlockSpec((1, tb, 1), lambda b, h, t: (h, b * nt + t, 0)),
            pl.BlockSpec((1, hb, HGRN_DK, HGRN_DV), lambda b, h, t: (b, h, 0, 0)),
        ],
        out_shape=[
            jax.ShapeDtypeStruct((batch * seq, D_INNER), BF16),
            jax.ShapeDtypeStruct((ngroups, batch * seq, 1), F32),
            jax.ShapeDtypeStruct((batch, HGRN_HEADS, HGRN_DK, HGRN_DV), F32),
        ],
        scratch_shapes=[pltpu.VMEM((hb, HGRN_DK, HGRN_DV), F32)],
        compiler_params=_params(("parallel", "parallel", "arbitrary")),
        name="hgrn_mix",
    )(*args)
    return u, ssq, s_out


def _diff_lambda(lam_ref, lam_init):
    lp = lam_ref[...]
    a = jnp.sum(lp[0:1] * lp[1:2], axis=-1, keepdims=True)
    b = jnp.sum(lp[2:3] * lp[3:4], axis=-1, keepdims=True)
    return jnp.exp(a) - jnp.exp(b) + lam_init


def _diff_finish(o1, o2, lam_ref, sw_ref, z, lam_init):
    o = o1 - _diff_lambda(lam_ref, lam_init) * o2
    o = _rms(o, sw_ref[...]) * (1.0 - lam_init)
    return (o * _silu(z)).astype(BF16)


KIND_FIRST, KIND_PLAIN, KIND_MASKED, KIND_DRAIN = 0, 1, 3, 5


def _diff_prompt_kernel(qa_tab, ka_tab, qb_tab, kb_tab, kind_tab, last_tab,
                        q_ref, k_ref, v_ref, z_ref, lam_ref, sw_ref, h_ref, *scratch,
                        bq, bk, rows_per_tile, lam_init):
    per_head = len(scratch) // 2
    heads = []
    for h in range(2):
        s_scr, p0, p1, m_scr, l_scr, a0, a1, acc_scr = scratch[per_head * h:per_head * (h + 1)]
        heads.append((s_scr, (p0, p1), m_scr, l_scr, (a0, a1), acc_scr))
    p = pl.program_id(2)
    qi = qa_tab[p]
    ki = ka_tab[p]
    par_a = qi % 2
    par_b = qb_tab[p] % 2
    first = ki == 0
    reps = (1, 2 * DIFF_HD // V7X_LANES)

    def stage_a(masked, slot):
        for h, (s_scr, p_scr, m_scr, l_scr, a_scr, acc_scr) in enumerate(heads):
            cols = slice(h * DIFF_HD, (h + 1) * DIFF_HD)
            s_scr[...] = _dot_nt(q_ref[0, :, cols], k_ref[0, :, cols])
        trim = masked and bq == bk and rows_per_tile % V7X_LANES == 0
        for s_scr, p_scr, m_scr, l_scr, a_scr, acc_scr in heads:
            for r in range(bq // rows_per_tile):
                rows = pl.ds(r * rows_per_tile, rows_per_tile)
                ncols = min(bk, (r + 1) * rows_per_tile) if trim else bk
                s = s_scr[rows, 0:ncols]
                if masked:
                    shape = (rows_per_tile, ncols)
                    q_chunk = (qi * bq + r * rows_per_tile + lax.broadcasted_iota(jnp.int32, shape, 0)) // CHUNK
                    k_chunk = (ki * bk + lax.broadcasted_iota(jnp.int32, shape, 1)) // CHUNK
                    s = jnp.where(k_chunk <= q_chunk, s, -jnp.inf)
                m_prev = jnp.where(first, -jnp.inf, m_scr[rows, :])
                m_new = jnp.maximum(m_prev, jnp.max(s, axis=-1, keepdims=True))
                alpha = jnp.exp2(m_prev - m_new)
                pr = jnp.exp2(s - jnp.tile(m_new, (1, ncols // V7X_LANES)))
                l_scr[par_a, rows, :] = alpha * l_scr[par_a, rows, :] + jnp.sum(pr, axis=-1, keepdims=True)
                m_scr[rows, :] = m_new
                a_scr[slot][rows, :] = alpha
                p_scr[slot][rows, 0:ncols] = pr.astype(BF16)
                if ncols < bk:
                    p_scr[slot][rows, ncols:bk] = jnp.zeros((rows_per_tile, bk - ncols), BF16)

    def stage_b(slot):
        for s_scr, p_scr, m_scr, l_scr, a_scr, acc_scr in heads:
            alpha = jnp.tile(a_scr[slot][...], reps)
            acc_scr[...] = alpha * acc_scr[...] + _dot(p_scr[slot][...], v_ref[0])

    kind = kind_tab[p]

    @pl.when(kind == KIND_FIRST)
    def _():
        for s_scr, p_scr, m_scr, l_scr, a_scr, acc_scr in heads:
            m_scr[...] = jnp.full_like(m_scr, -jnp.inf)
            l_scr[...] = jnp.zeros_like(l_scr)
            acc_scr[...] = jnp.zeros_like(acc_scr)
        stage_a(True, 0)

    for slot in range(2):
        @pl.when(kind == KIND_PLAIN + slot)
        def _():
            stage_a(False, slot)
            stage_b(1 - slot)

        @pl.when(kind == KIND_MASKED + slot)
        def _():
            stage_a(True, slot)
            stage_b(1 - slot)

        @pl.when(kind == KIND_DRAIN + slot)
        def _():
            stage_b(1 - slot)

    @pl.when(last_tab[p] == 1)
    def _():
        o1, o2 = [acc_scr[...] * jnp.tile(1.0 / l_scr[par_b], reps)
                  for _, _, _, l_scr, _, acc_scr in heads]
        h_ref[...] = _diff_finish(o1, o2, lam_ref, sw_ref, z_ref[...], lam_init)


def _diff_prompt_tables(seq, bq, bk):
    pairs = []
    for qi in range(seq // bq):
        q_lo, q_hi = qi * bq, (qi + 1) * bq
        nk = -(-q_hi // bk)
        for ki in range(nk):
            fully_visible = (ki + 1) * bk <= (q_lo // CHUNK + 1) * CHUNK
            pairs.append((qi, ki, not fully_visible, ki == nk - 1))
    assert pairs[0][2], "the first pair sits on the diagonal"
    n = len(pairs)
    qa = [pairs[min(p, n - 1)][0] for p in range(n + 1)]
    ka = [pairs[min(p, n - 1)][1] for p in range(n + 1)]
    qb = [pairs[max(p - 1, 0)][0] for p in range(n + 1)]
    kb = [pairs[max(p - 1, 0)][1] for p in range(n + 1)]
    kind = ([KIND_FIRST] + [(KIND_MASKED if pairs[p][2] else KIND_PLAIN) + p % 2 for p in range(1, n)]
            + [KIND_DRAIN + n % 2])
    last = [0] + [1 if pairs[p - 1][3] else 0 for p in range(1, n + 1)]
    return [jnp.asarray(np.asarray(t, np.int32)) for t in (qa, ka, qb, kb, kind, last)]


def _diff_prompt_attn(qkv, z, lam_p, subln_w, lam_init, batch, seq):
    bq = min(ATT_BQ, seq)
    bk = min(ATT_BK, seq)
    nq = seq // bq
    nkb = seq // bk
    pw = 2 * DIFF_HD
    tabs = _diff_prompt_tables(seq, bq, bk)
    n_steps = int(tabs[0].shape[0])

    def spec(shape, index):
        return pl.BlockSpec(shape, lambda b, h, p, qa, ka, qb, kb, kind, last: index(b, h, p, qa, ka, qb, kb))

    grid_spec = pltpu.PrefetchScalarGridSpec(
        num_scalar_prefetch=6,
        grid=(batch, DIFF_HEADS, n_steps),
        in_specs=[
            spec((1, bq, pw), lambda b, h, p, qa, ka, qb, kb: (h, b * nq + qa[p], 0)),
            spec((1, bk, pw), lambda b, h, p, qa, ka, qb, kb: (DIFF_HEADS + h, b * nkb + ka[p], 0)),
            spec((1, bk, pw), lambda b, h, p, qa, ka, qb, kb: (2 * DIFF_HEADS + h, b * nkb + kb[p], 0)),
            spec((bq, pw), lambda b, h, p, qa, ka, qb, kb: (b * nq + qb[p], h)),
            spec((4, DIFF_HD), lambda b, h, p, qa, ka, qb, kb: (0, 0)),
            spec((1, pw), lambda b, h, p, qa, ka, qb, kb: (0, 0)),
        ],
        out_specs=spec((bq, pw), lambda b, h, p, qa, ka, qb, kb: (b * nq + qb[p], h)),
        scratch_shapes=2 * [
            pltpu.VMEM((bq, bk), F32),
            pltpu.VMEM((bq, bk), BF16),
            pltpu.VMEM((bq, bk), BF16),
            pltpu.VMEM((bq, V7X_LANES), F32),
            pltpu.VMEM((2, bq, V7X_LANES), F32),
            pltpu.VMEM((bq, V7X_LANES), F32),
            pltpu.VMEM((bq, V7X_LANES), F32),
            pltpu.VMEM((bq, pw), F32),
        ],
    )
    return pl.pallas_call(
        functools.partial(_diff_prompt_kernel, bq=bq, bk=bk, rows_per_tile=min(ATT_ROWS, bq),
                          lam_init=lam_init),
        grid_spec=grid_spec,
        out_shape=jax.ShapeDtypeStruct((batch * seq, D_INNER), BF16),
        compiler_params=_params(("parallel", "parallel", "arbitrary")),
        name="diff_prompt_attn",
    )(*tabs, qkv, qkv, qkv, z, lam_p, subln_w.reshape(1, pw))


def _diff_sample_kernel(qkv_ref, z_ref, kc_ref, vc_ref, lam_ref, sw_ref, h_ref, *, lam_init):
    pw = 2 * DIFF_HD
    for pair in range(DIFF_HEADS):
        pcols = slice(pair * pw, (pair + 1) * pw)
        vc_bf = vc_ref[:, pcols].astype(BF16)
        vn_bf = qkv_ref[2 * DIFF_HEADS + pair]
        outs = []
        for h in range(2):
            cols = slice(h * DIFF_HD, (h + 1) * DIFF_HD)
            ccols = slice(pair * pw + h * DIFF_HD, pair * pw + (h + 1) * DIFF_HD)
            q_bf = qkv_ref[pair, :, cols]
            s_c = _dot_nt(q_bf, kc_ref[:, ccols].astype(BF16))
            s_n = _dot_nt(q_bf, qkv_ref[DIFF_HEADS + pair, :, cols])
            m = jnp.maximum(jnp.max(s_c, axis=-1, keepdims=True), jnp.max(s_n, axis=-1, keepdims=True))
            p_c = jnp.exp2(s_c - m)
            p_n = jnp.exp2(s_n - m)
            l = jnp.sum(p_c, axis=-1, keepdims=True) + jnp.sum(p_n, axis=-1, keepdims=True)
            acc = _dot(p_c.astype(BF16), vc_bf) + _dot(p_n.astype(BF16), vn_bf)
            outs.append(acc * (1.0 / l))
        h_ref[:, pcols] = _diff_finish(outs[0], outs[1], lam_ref, sw_ref, z_ref[:, pcols], lam_init)


def _diff_sample_attn(qkv, z, cache_k2, cache_v2, lam_p, subln_w, lam_init, batch, seq, past):
    pw = 2 * DIFF_HD
    return pl.pallas_call(
        functools.partial(_diff_sample_kernel, lam_init=lam_init),
        grid=(batch,),
        in_specs=[
            pl.BlockSpec((3 * DIFF_HEADS, seq, pw), lambda b: (0, b, 0)),
            pl.BlockSpec((seq, D_INNER), lambda b: (b, 0)),
            pl.BlockSpec((past, D_INNER), lambda b: (b, 0)),
            pl.BlockSpec((past, D_INNER), lambda b: (b, 0)),
            pl.BlockSpec((4, DIFF_HD), lambda b: (0, 0)),
            pl.BlockSpec((1, pw), lambda b: (0, 0)),
        ],
        out_specs=pl.BlockSpec((seq, D_INNER), lambda b: (b, 0)),
        out_shape=jax.ShapeDtypeStruct((batch * seq, D_INNER), BF16),
        compiler_params=_params(("parallel",)),
        name="diff_sample_attn",
    )(qkv, z, cache_k2, cache_v2, lam_p, subln_w.reshape(1, pw))


def kernel(x_prompt, x_sample, state_gla, state_hgrn, cache_k, cache_v, norm_w, final_norm_w,
           gla_w_in, gla_w_a1, gla_w_a2, gla_b_a, gla_norm_w, gla_w_out,
           hgrn_w_in, hgrn_lower_bounds, hgrn_norm_w, hgrn_w_out,
           diff_w_in, diff_lambda, diff_subln_w, diff_w_out):
    bp, tp, d = x_prompt.shape
    bs, ts, _ = x_sample.shape
    past = cache_k.shape[2]
    streams = [(bp, tp), (bs, ts)]
    xs = [x_prompt.reshape(bp * tp, d), x_sample.reshape(bs * ts, d)]

    def out_proj(h, res, w_stack, j, i, ssq=None):
        last = i == DEPTH - 1
        return _out_proj(h, res, w_stack, j, final_norm_w if last else norm_w[i + 1], last, ssq)

    gla_w_in_bf, gla_w_out_bf = gla_w_in.astype(BF16), gla_w_out.astype(BF16)
    hgrn_w_in_bf, hgrn_w_out_bf = hgrn_w_in.astype(BF16), hgrn_w_out.astype(BF16)
    diff_w_in_bf, diff_w_out_bf = diff_w_in.astype(BF16), diff_w_out.astype(BF16)

    hn = [None, None]
    gla_states = [[], []]
    hgrn_states = [[], []]
    k_rows = [[], []]
    v_rows = [[], []]
    ia = ib = ic = 0
    for i in range(DEPTH):
        kind = i % N_MIXERS
        if kind == 0:
            j = ia
            ia += 1
            w_a1 = jnp.pad(gla_w_a1[j], ((0, 0), (0, V7X_LANES - GLA_GATE_RANK))).astype(BF16)
            w_a2 = jnp.pad(gla_w_a2[j], ((0, V7X_LANES - GLA_GATE_RANK), (0, 0))).astype(BF16)
            for si, (batch, seq) in enumerate(streams):
                if i == 0:
                    proj, gate_lr = _norm_proj(xs[si], norm_w[i], gla_w_in_bf, j, w_a1)
                else:
                    proj, gate_lr = _in_proj(hn[si], gla_w_in_bf, j, w_a1)
                states = None if si == 0 else state_gla
                h, s_new = _gla_mix(proj, gate_lr, w_a2, gla_b_a[j], gla_norm_w[j], states, j, batch, seq)
                xs[si], hn[si] = out_proj(h, xs[si], gla_w_out_bf, j, i)
                gla_states[si].append(s_new)
        elif kind == 1:
            j = ib
            ib += 1
            for si, (batch, seq) in enumerate(streams):
                proj = _in_proj(hn[si], hgrn_w_in_bf, j)
                states = None if si == 0 else state_hgrn
                u, ssq, s_new = _hgrn_mix(proj, hgrn_lower_bounds, i, hgrn_norm_w[j], states, j, batch, seq)
                xs[si], hn[si] = out_proj(u, xs[si], hgrn_w_out_bf, j, i, ssq)
                hgrn_states[si].append(s_new)
        else:
            j = ic
            ic += 1
            lam_init = 0.8 - 0.6 * math.exp(-0.3 * i)
            for si, (batch, seq) in enumerate(streams):
                qkv, k_new, v_new, z = _diff_proj(hn[si], diff_w_in_bf, j)
                if si == 0:
                    h = _diff_prompt_attn(qkv, z, diff_lambda[j], diff_subln_w[j], lam_init, batch, seq)
                else:
                    ck = cache_k[j].reshape(batch * past, 2 * DIFF_HEADS * DIFF_HD)
                    cv = cache_v[j].reshape(batch * past, 2 * DIFF_HEADS * DIFF_HD)
                    h = _diff_sample_attn(qkv, z, ck, cv, diff_lambda[j], diff_subln_w[j], lam_init,
                                          batch, seq, past)
                xs[si], hn[si] = out_proj(h, xs[si], diff_w_out_bf, j, i)
                k_rows[si].append(k_new.reshape(batch, seq, 2 * DIFF_HEADS, DIFF_HD))
                v_rows[si].append(v_new.reshape(batch, seq, DIFF_HEADS, 2 * DIFF_HD))

    return (hn[0].reshape(bp, tp, d), hn[1].reshape(bs, ts, d),
            jnp.stack(gla_states[0]), jnp.stack(gla_states[1]),
            jnp.stack(hgrn_states[0]), jnp.stack(hgrn_states[1]),
            jnp.stack(k_rows[0]), jnp.stack(v_rows[0]),
            jnp.stack(k_rows[1]), jnp.stack(v_rows[1]))
```

```python
import functools
import math

import numpy as np
import jax
import jax.numpy as jnp
from jax import lax
from jax.experimental import pallas as pl
from jax.experimental.pallas import tpu as pltpu

D_MODEL = 2048
DEPTH = 4
CHUNK = 64
N_MIXERS = 3
D_INNER = D_MODEL
NORM_EPS = 1e-6
GLA_HEADS = 4
GLA_KD = D_INNER // 2
GLA_DK = GLA_KD // GLA_HEADS
GLA_DV = D_INNER // GLA_HEADS
GLA_GATE_RANK = 16
GLA_GATE_NORMALIZER = 16.0
HGRN_HEADS = 16
HGRN_DK = 128
HGRN_DV = 128
DIFF_HD = 128
DIFF_HEADS = 8

LOG2_E = math.log2(math.e)
DIFF_Q_SCALE = (DIFF_HD ** -0.5) * LOG2_E

F32 = jnp.float32
BF16 = jnp.bfloat16

V7X_LANES = 128
V7X_VMEM_BYTES = 64 * 1024 * 1024
VMEM_LIMIT_BYTES = (V7X_VMEM_BYTES * 3) // 4

PROJ_TM = 1024
PROJ_TN = 1024
IN_TN = 2048
DIFF_TN = 1024
OUT_TM = 512
REC_TOKENS = 256
GLA_HEADS_PER_STEP = 4
HGRN_HEADS_PER_STEP = 16
ATT_BQ = 1024
ATT_BK = 1024
ATT_ROWS = 128


def _params(sem):
    return pltpu.CompilerParams(dimension_semantics=sem, vmem_limit_bytes=VMEM_LIMIT_BYTES)


def _dot(a, b):
    return jnp.dot(a, b, preferred_element_type=F32)


def _dot_nt(a, b):
    return lax.dot_general(a, b, (((1,), (1,)), ((), ())), preferred_element_type=F32)


def _dot_tn(a, b):
    return lax.dot_general(a, b, (((0,), (0,)), ((), ())), preferred_element_type=F32)


def _sigmoid(x):
    return 1.0 / (1.0 + jnp.exp(-x))


def _silu(x):
    return x * _sigmoid(x)


def _rms(x, w):
    ms = jnp.mean(x * x, axis=-1, keepdims=True)
    return x * lax.rsqrt(ms + NORM_EPS) * w


def _norm_proj_kernel(*refs, has_aux):
    if has_aux:
        x_ref, nw_ref, w_ref, aw_ref, o_ref, ao_ref, h_scr = refs
    else:
        x_ref, nw_ref, w_ref, o_ref, h_scr = refs

    @pl.when(pl.program_id(1) == 0)
    def _():
        h_scr[...] = _rms(x_ref[...], nw_ref[...]).astype(BF16)
        if has_aux:
            ao_ref[...] = _dot(h_scr[...], aw_ref[...])

    o_ref[...] = _dot(h_scr[...], w_ref[0])


def _norm_proj(x, nw, w_stack, layer, aux_w_bf=None):
    m, d = x.shape
    n = w_stack.shape[2]
    tm = min(PROJ_TM, m)
    tn = PROJ_TN
    in_specs = [
        pl.BlockSpec((tm, d), lambda i, j: (i, 0)),
        pl.BlockSpec((1, d), lambda i, j: (0, 0)),
        pl.BlockSpec((1, d, tn), lambda i, j: (layer, 0, j)),
    ]
    out_shape = [jax.ShapeDtypeStruct((m, n), F32)]
    out_specs = [pl.BlockSpec((tm, tn), lambda i, j: (i, j))]
    args = [x, nw.reshape(1, d), w_stack]
    if aux_w_bf is not None:
        na = aux_w_bf.shape[1]
        in_specs.append(pl.BlockSpec((d, na), lambda i, j: (0, 0)))
        out_shape.append(jax.ShapeDtypeStruct((m, na), F32))
        out_specs.append(pl.BlockSpec((tm, na), lambda i, j: (i, 0)))
        args.append(aux_w_bf)
    outs = pl.pallas_call(
        functools.partial(_norm_proj_kernel, has_aux=aux_w_bf is not None),
        grid=(m // tm, n // tn),
        in_specs=in_specs,
        out_specs=out_specs,
        out_shape=out_shape,
        scratch_shapes=[pltpu.VMEM((tm, d), BF16)],
        compiler_params=_params(("parallel", "arbitrary")),
        name="norm_proj",
    )(*args)
    return outs if aux_w_bf is not None else outs[0]


def _in_proj_kernel(*refs, has_aux):
    if has_aux:
        h_ref, w_ref, aw_ref, o_ref, ao_ref = refs

        @pl.when(pl.program_id(1) == 0)
        def _():
            ao_ref[...] = _dot(h_ref[...], aw_ref[...])
    else:
        h_ref, w_ref, o_ref = refs
    o_ref[...] = _dot(h_ref[...], w_ref[0])


def _in_proj(hn_bf, w_stack, layer, aux_w_bf=None):
    m, d = hn_bf.shape
    n = w_stack.shape[2]
    tm = min(PROJ_TM, m)
    tn = IN_TN
    in_specs = [
        pl.BlockSpec((tm, d), lambda i, j: (i, 0)),
        pl.BlockSpec((1, d, tn), lambda i, j: (layer, 0, j)),
    ]
    out_shape = [jax.ShapeDtypeStruct((m, n), F32)]
    out_specs = [pl.BlockSpec((tm, tn), lambda i, j: (i, j))]
    args = [hn_bf, w_stack]
    if aux_w_bf is not None:
        na = aux_w_bf.shape[1]
        in_specs.append(pl.BlockSpec((d, na), lambda i, j: (0, 0)))
        out_shape.append(jax.ShapeDtypeStruct((m, na), F32))
        out_specs.append(pl.BlockSpec((tm, na), lambda i, j: (i, 0)))
        args.append(aux_w_bf)
    outs = pl.pallas_call(
        functools.partial(_in_proj_kernel, has_aux=aux_w_bf is not None),
        grid=(m // tm, n // tn),
        in_specs=in_specs,
        out_specs=out_specs,
        out_shape=out_shape,
        compiler_params=_params(("parallel", "arbitrary")),
        name="in_proj",
    )(*args)
    return outs if aux_w_bf is not None else outs[0]


def _diff_group_kernel(*refs, scale, write_pairs, write_rows):
    h_ref, w_ref = refs[:2]
    outs = refs[2:]
    acc = _dot(h_ref[...], w_ref[0])
    if write_pairs:
        pm_ref = outs[0]
        pw = pm_ref.shape[2]
        val = acc if scale is None else acc * scale
        for p in range(pm_ref.shape[0]):
            pm_ref[p] = val[:, p * pw:(p + 1) * pw].astype(BF16)
    if write_rows:
        outs[-1][...] = acc


def _diff_group_proj(hn_bf, w_stack, layer, group, scale, write_pairs, write_rows):
    m, d = hn_bf.shape
    tm = min(PROJ_TM, m)
    tn = DIFF_TN
    pw = 2 * DIFF_HD
    g = D_INNER // tn
    ppt = tn // pw
    out_specs, out_shape = [], []
    if write_pairs:
        out_specs.append(pl.BlockSpec((ppt, tm, pw), lambda i, j: (j, i, 0)))
        out_shape.append(jax.ShapeDtypeStruct((DIFF_HEADS, m, pw), BF16))
    if write_rows:
        out_specs.append(pl.BlockSpec((tm, tn), lambda i, j: (i, j)))
        out_shape.append(jax.ShapeDtypeStruct((m, D_INNER), F32))
    return pl.pallas_call(
        functools.partial(_diff_group_kernel, scale=scale, write_pairs=write_pairs, write_rows=write_rows),
        grid=(m // tm, g),
        in_specs=[
            pl.BlockSpec((tm, d), lambda i, j: (i, 0)),
            pl.BlockSpec((1, d, tn), lambda i, j: (layer, 0, group * g + j)),
        ],
        out_specs=out_specs,
        out_shape=out_shape,
        compiler_params=_params(("parallel", "arbitrary")),
        name="diff_proj",
    )(hn_bf, w_stack)


def _diff_proj(hn_bf, w_stack, layer):
    (q_pm,) = _diff_group_proj(hn_bf, w_stack, layer, 0, DIFF_Q_SCALE, True, False)
    k_pm, k_rows = _diff_group_proj(hn_bf, w_stack, layer, 1, None, True, True)
    v_pm, v_rows = _diff_group_proj(hn_bf, w_stack, layer, 2, None, True, True)
    (z_rows,) = _diff_group_proj(hn_bf, w_stack, layer, 3, None, False, True)
    return q_pm, k_pm, v_pm, k_rows, v_rows, z_rows


def _row_rms_scale(ssq_ref):
    ssq = ssq_ref[0]
    for gi in range(1, ssq_ref.shape[0]):
        ssq = ssq + ssq_ref[gi]
    return lax.rsqrt(ssq * (1.0 / D_INNER) + NORM_EPS)


def _out_proj_kernel(*refs, has_ssq, emit_x):
    h_ref = refs[0]
    ssq_ref = refs[1] if has_ssq else None
    res_ref, w_ref, nw_ref = refs[1 + has_ssq:4 + has_ssq]
    outs = refs[4 + has_ssq:]
    acc = _dot(h_ref[...], w_ref[0])
    if has_ssq:
        acc = acc * _row_rms_scale(ssq_ref)
    x_new = res_ref[...] + acc
    if emit_x:
        outs[0][...] = x_new
    outs[-1][...] = _rms(x_new, nw_ref[...]).astype(outs[-1].dtype)


def _out_proj(h_bf, res, w_stack, layer, norm_w, last, ssq=None):
    m, d = h_bf.shape
    n = w_stack.shape[2]
    tm = min(OUT_TM, m)
    in_specs = [pl.BlockSpec((tm, d), lambda i: (i, 0))]
    args = [h_bf]
    if ssq is not None:
        in_specs.append(pl.BlockSpec((ssq.shape[0], tm, 1), lambda i: (0, i, 0)))
        args.append(ssq)
    in_specs += [
        pl.BlockSpec((tm, n), lambda i: (i, 0)),
        pl.BlockSpec((1, d, n), lambda i: (layer, 0, 0)),
        pl.BlockSpec((1, n), lambda i: (0, 0)),
    ]
    args += [res, w_stack, norm_w.reshape(1, n)]
    out_shape = [jax.ShapeDtypeStruct((m, n), F32 if last else BF16)]
    if not last:
        out_shape.insert(0, jax.ShapeDtypeStruct((m, n), F32))
    outs = pl.pallas_call(
        functools.partial(_out_proj_kernel, has_ssq=ssq is not None, emit_x=not last),
        grid=(m // tm,),
        in_specs=in_specs,
        out_specs=[pl.BlockSpec((tm, n), lambda i: (i, 0)) for _ in out_shape],
        out_shape=out_shape,
        compiler_params=_params(("parallel",)),
        name="out_proj",
    )(*args)
    return (None, outs[0]) if last else (outs[0], outs[1])


def _block_constants(tb, chunk):
    n = tb // chunk
    assert n & (n - 1) == 0, "chunks per block must be a power of two"
    row = np.arange(tb)[:, None]
    col = np.arange(tb)[None, :]
    rc, cc = row // chunk, col // chunk
    same = (rc == cc) & (row >= col)
    level = np.where(same, 0, -1)
    for lv in range(1, n.bit_length()):
        group = 1 << lv
        meet = (rc > cc) & (rc // group == cc // group) & (rc // (group // 2) != cc // (group // 2))
        level = np.where(meet, lv, level)
    return jnp.asarray(same.astype(np.float32), BF16), jnp.asarray(level.astype(np.int32))


def _chunk_cumsum(g, tri_bf):
    g_hi = g.astype(BF16)
    g_lo = (g - g_hi.astype(F32)).astype(BF16)
    return _dot(tri_bf, g_hi) + _dot(tri_bf, g_lo)


def _per_chunk(rows, chunk):
    parts = [jnp.broadcast_to(r, (chunk, r.shape[1])) for r in rows]
    return parts[0] if len(parts) == 1 else jnp.concatenate(parts, axis=0)


def _sum_rows(rows, width):
    out = jnp.zeros((1, width), F32)
    for r in rows:
        out = out + r
    return out


def _block_decay_column(b, chunk):
    tb, w = b.shape
    total = _sum_rows([b[(c + 1) * chunk - 1:(c + 1) * chunk, :] for c in range(tb // chunk)], w)
    return jnp.transpose(jnp.broadcast_to(jnp.exp2(total), (V7X_LANES, w)))[:, 0:1]


def _block_recurrence(q, k, b, v_bf, s0, level, dec_col, chunk, q_log2_scale):
    w = q.shape[1]
    n = q.shape[0] // chunk
    mids = [b[c * chunk + chunk // 2 - 1:c * chunk + chunk // 2, :] for c in range(n)]
    lasts = [b[(c + 1) * chunk - 1:(c + 1) * chunk, :] for c in range(n)]
    q_mid = q * jnp.exp2(b - _per_chunk([m - q_log2_scale for m in mids], chunk))
    k_mid = k * jnp.exp2(_per_chunk(mids, chunk) - b)
    q_in = q_mid * _per_chunk([jnp.exp2(m) for m in mids], chunk)
    k_out = k_mid * _per_chunk([jnp.exp2(l - m) for l, m in zip(lasts, mids)], chunk)
    attn = jnp.where(level == 0, _dot_nt(q_mid.astype(BF16), k_mid.astype(BF16)), 0.0)
    zero_row = jnp.zeros((1, w), F32)
    for lv in range(1, n.bit_length()):
        group, half = 1 << lv, 1 << (lv - 1)
        if half == 1:
            q_l, k_l = q_in, k_out
        else:
            mid_of = lambda c: (c // group) * group + half
            q_l = q_in * _per_chunk([jnp.exp2(_sum_rows(lasts[mid_of(c):c], w)) if c % group >= half
                                     else zero_row for c in range(n)], chunk)
            k_l = k_out * _per_chunk([jnp.exp2(_sum_rows(lasts[j + 1:mid_of(j)], w)) if j % group < half
                                      else zero_row for j in range(n)], chunk)
        attn = jnp.where(level == lv, _dot_nt(q_l.astype(BF16), k_l.astype(BF16)), attn)
    head = _per_chunk([jnp.exp2(_sum_rows(lasts[:c], w)) for c in range(n)], chunk)
    tail = _per_chunk([jnp.exp2(_sum_rows(lasts[c + 1:], w)) for c in range(n)], chunk)
    o = _dot((q_in * head).astype(BF16), s0.astype(BF16)) + _dot(attn.astype(BF16), v_bf)
    s_end = dec_col * s0 + _dot_tn((k_out * tail).astype(BF16), v_bf)
    return o, s_end


def _log_sigmoid(x):
    return jnp.minimum(x, 0.0) - jnp.log(1.0 + jnp.exp(-jnp.abs(x)))


def _gla_kernel(*refs, chunk, zero_init, heads):
    if zero_init:
        (q_ref, k_ref, v_ref, z_ref, ga_ref, wa2_ref, ba_ref, nw_ref, tri_ref, level_ref,
         h_ref, sout_ref, s_scr) = refs
    else:
        (q_ref, k_ref, v_ref, z_ref, ga_ref, wa2_ref, ba_ref, nw_ref, tri_ref, level_ref, s0_ref,
         h_ref, sout_ref, s_scr) = refs
    t = pl.program_id(2)

    @pl.when(t == 0)
    def _():
        if zero_init:
            s_scr[...] = jnp.zeros_like(s_scr)
        else:
            s_scr[...] = s0_ref[0, 0]

    gate_in = _dot(ga_ref[...].astype(BF16), wa2_ref[...]) + ba_ref[...]
    g = _log_sigmoid(gate_in) * (LOG2_E / GLA_GATE_NORMALIZER)
    b = _chunk_cumsum(g, tri_ref[...])
    dec_col = _block_decay_column(b, chunk)
    level = level_ref[...]
    nw = nw_ref[...]
    for hi in range(heads):
        kc = slice(hi * GLA_DK, (hi + 1) * GLA_DK)
        vc = slice(hi * GLA_DV, (hi + 1) * GLA_DV)
        o, s = _block_recurrence(q_ref[:, kc], k_ref[:, kc], b[:, kc], v_ref[:, vc].astype(BF16),
                                 s_scr[hi], level, dec_col[kc, :], chunk, math.log2(GLA_DK ** -0.5))
        s_scr[hi] = s
        h_ref[:, vc] = (_rms(o, nw) * _silu(z_ref[:, vc])).astype(BF16)

    @pl.when(t == pl.num_programs(2) - 1)
    def _():
        sout_ref[0] = s_scr[...]


def _gla_mix(proj, gate_lr, wa2_bf, b_a, norm_w, states, layer, batch, seq):
    chunk = min(CHUNK, seq)
    tb = min(REC_TOKENS, seq)
    nt = seq // tb
    hb = GLA_HEADS_PER_STEP
    kb = hb * GLA_DK
    vb = hb * GLA_DV
    k_off = GLA_KD // kb
    v_off = (2 * GLA_KD) // vb
    z_off = (2 * GLA_KD + D_INNER) // vb
    zero_init = states is None
    in_specs = [
        pl.BlockSpec((tb, kb), lambda b, h, t: (b * nt + t, h)),
        pl.BlockSpec((tb, kb), lambda b, h, t: (b * nt + t, k_off + h)),
        pl.BlockSpec((tb, vb), lambda b, h, t: (b * nt + t, v_off + h)),
        pl.BlockSpec((tb, vb), lambda b, h, t: (b * nt + t, z_off + h)),
        pl.BlockSpec((tb, V7X_LANES), lambda b, h, t: (b * nt + t, 0)),
        pl.BlockSpec((V7X_LANES, kb), lambda b, h, t: (0, h)),
        pl.BlockSpec((1, kb), lambda b, h, t: (0, h)),
        pl.BlockSpec((1, GLA_DV), lambda b, h, t: (0, 0)),
        pl.BlockSpec((tb, tb), lambda b, h, t: (0, 0)),
        pl.BlockSpec((tb, tb), lambda b, h, t: (0, 0)),
    ]
    args = [proj, proj, proj, proj, gate_lr, wa2_bf, b_a.reshape(1, GLA_KD), norm_w.reshape(1, GLA_DV),
            *_block_constants(tb, chunk)]
    if not zero_init:
        in_specs.append(pl.BlockSpec((1, 1, hb, GLA_DK, GLA_DV), lambda b, h, t: (layer, b, h, 0, 0)))
        args.append(states)
    h, s_out = pl.pallas_call(
        functools.partial(_gla_kernel, chunk=chunk, zero_init=zero_init, heads=hb),
        grid=(batch, GLA_HEADS // hb, nt),
        in_specs=in_specs,
        out_specs=[
            pl.BlockSpec((tb, vb), lambda b, h, t: (b * nt + t, h)),
            pl.BlockSpec((1, hb, GLA_DK, GLA_DV), lambda b, h, t: (b, h, 0, 0)),
        ],
        out_shape=[
            jax.ShapeDtypeStruct((batch * seq, D_INNER), BF16),
            jax.ShapeDtypeStruct((batch, GLA_HEADS, GLA_DK, GLA_DV), F32),
        ],
        scratch_shapes=[pltpu.VMEM((hb, GLA_DK, GLA_DV), F32)],
        compiler_params=_params(("parallel", "parallel", "arbitrary")),
        name="gla_mix",
    )(*args)
    return h, s_out


def _hgrn_kernel(*refs, chunk, zero_init, layer, heads):
    if zero_init:
        (q_ref, f_ref, i_ref, z_ref, lbp_ref, nw_ref, tri_ref, level_ref,
         u_ref, ssq_ref, sout_ref, s_scr) = refs
    else:
        (q_ref, f_ref, i_ref, z_ref, lbp_ref, nw_ref, tri_ref, level_ref, s0_ref,
         u_ref, ssq_ref, sout_ref, s_scr) = refs
    t = pl.program_id(2)

    @pl.when(t == 0)
    def _():
        if zero_init:
            s_scr[...] = jnp.zeros_like(s_scr)
        else:
            s_scr[...] = s0_ref[0, 0]

    lbp = lbp_ref[...]
    lbe = jnp.exp(lbp - jnp.max(lbp, axis=0, keepdims=True))
    lbs = lbe / jnp.sum(lbe, axis=0, keepdims=True)
    lb = jnp.zeros_like(lbs[0:1])
    for r in range(1, layer + 1):
        lb = lb + lbs[r:r + 1]

    q = _silu(q_ref[...])
    fgate = lb + (1.0 - lb) * _sigmoid(f_ref[...])
    k = 1.0 - fgate
    b = _chunk_cumsum(jnp.log(fgate) * LOG2_E, tri_ref[...])
    dec_col = _block_decay_column(b, chunk)
    v_bf = i_ref[...].astype(BF16)
    level = level_ref[...]
    ssq =jnp.zeros((q.shape[0], 1), F32)
    for hi in range(heads):
        cols = slice(hi * HGRN_DK, (hi + 1) * HGRN_DK)
        o, s = _block_recurrence(q[:, cols], k[:, cols], b[:, cols], v_bf[:, cols], s_scr[hi],
                                 level, dec_col[cols, :], chunk, math.log2(HGRN_DK ** -0.5))
        s_scr[hi] = s
        ssq = ssq + jnp.sum(o * o, axis=-1, keepdims=True)
        u_ref[:, cols] = (o * nw_ref[:, cols] * _silu(z_ref[:, cols])).astype(BF16)
    ssq_ref[0] = ssq

    @pl.when(t == pl.num_programs(2) - 1)
    def _():
        sout_ref[0] = s_scr[...]


def _hgrn_mix(proj, lower_bounds, layer, norm_w, states, state_layer, batch, seq):
    chunk = min(CHUNK, seq)
    tb = min(REC_TOKENS, seq)
    nt = seq // tb
    hb = HGRN_HEADS_PER_STEP
    wb = hb * HGRN_DK
    ngroups = HGRN_HEADS // hb
    col_groups = D_INNER // wb
    zero_init = states is None
    in_specs = [
        pl.BlockSpec((tb, wb), lambda b, h, t: (b * nt + t, h)),
        pl.BlockSpec((tb, wb), lambda b, h, t: (b * nt + t, col_groups + h)),
        pl.BlockSpec((tb, wb), lambda b, h, t: (b * nt + t, 2 * col_groups + h)),
        pl.BlockSpec((tb, wb), lambda b, h, t: (b * nt + t, 3 * col_groups + h)),
        pl.BlockSpec((DEPTH, wb), lambda b, h, t: (0, h)),
        pl.BlockSpec((1, wb), lambda b, h, t: (0, h)),
        pl.BlockSpec((tb, tb), lambda b, h, t: (0, 0)),
        pl.BlockSpec((tb, tb), lambda b, h, t: (0, 0)),
    ]
    args = [proj, proj, proj, proj, lower_bounds, norm_w.reshape(1, D_INNER), *_block_constants(tb, chunk)]
    if not zero_init:
        in_specs.append(pl.BlockSpec((1, 1, hb, HGRN_DK, HGRN_DV), lambda b, h, t: (state_layer, b, h, 0, 0)))
        args.append(states)
    u, ssq, s_out = pl.pallas_call(
        functools.partial(_hgrn_kernel, chunk=chunk, zero_init=zero_init, layer=layer, heads=hb),
        grid=(batch, ngroups, nt),
        in_specs=in_specs,
        out_specs=[
            pl.BlockSpec((tb, wb), lambda b, h, t: (b * nt + t, h)),
            pl.BlockSpec((1, tb, 1), lambda b, h, t: (h, b * nt + t, 0)),
            pl.BlockSpec((1, hb, HGRN_DK, HGRN_DV), lambda b, h, t: (b, h, 0, 0)),
        ],
        out_shape=[
            jax.ShapeDtypeStruct((batch * seq, D_INNER), BF16),
            jax.ShapeDtypeStruct((ngroups, batch * seq, 1), F32),
            jax.ShapeDtypeStruct((batch, HGRN_HEADS, HGRN_DK, HGRN_DV), F32),
        ],
        scratch_shapes=[pltpu.VMEM((hb, HGRN_DK, HGRN_DV), F32)],
        compiler_params=_params(("parallel", "parallel", "arbitrary")),
        name="hgrn_mix",
    )(*args)
    return u, ssq, s_out


def _diff_lambda(lam_ref, lam_init):
    lp = lam_ref[...]
    a = jnp.sum(lp[0:1] * lp[1:2], axis=-1, keepdims=True)
    b = jnp.sum(lp[2:3] * lp[3:4], axis=-1, keepdims=True)
    return jnp.exp(a) - jnp.exp(b) + lam_init


def _diff_finish(o1, o2, lam_ref, sw_ref, z, lam_init):
    o = o1 - _diff_lambda(lam_ref, lam_init) * o2
    o = _rms(o, sw_ref[...]) * (1.0 - lam_init)
    return (o * _silu(z)).astype(BF16)


KIND_FIRST, KIND_PLAIN, KIND_MASKED, KIND_DRAIN = 0, 1, 3, 5


def _diff_prompt_kernel(qa_tab, ka_tab, qb_tab, kb_tab, kind_tab, last_tab,
                        q_ref, k_ref, v_ref, z_ref, lam_ref, sw_ref, h_ref, *scratch,
                        bq, bk, rows_per_tile, lam_init):
    per_head = len(scratch) // 2
    heads = []
    for h in range(2):
        s_scr, p0, p1, m_scr, l_scr, a0, a1, acc_scr = scratch[per_head * h:per_head * (h + 1)]
        heads.append((s_scr, (p0, p1), m_scr, l_scr, (a0, a1), acc_scr))
    p = pl.program_id(2)
    qi = qa_tab[p]
    ki = ka_tab[p]
    par_a = qi % 2
    par_b = qb_tab[p] % 2
    first = ki == 0
    reps = (1, 2 * DIFF_HD // V7X_LANES)

    def stage_a(masked, slot):
        for h, (s_scr, p_scr, m_scr, l_scr, a_scr, acc_scr) in enumerate(heads):
            cols = slice(h * DIFF_HD, (h + 1) * DIFF_HD)
            s_scr[...] = _dot_nt(q_ref[0, :, cols], k_ref[0, :, cols])
        trim = masked and bq == bk and rows_per_tile % V7X_LANES == 0
        for s_scr, p_scr, m_scr, l_scr, a_scr, acc_scr in heads:
            for r in range(bq // rows_per_tile):
                rows = pl.ds(r * rows_per_tile, rows_per_tile)
                ncols = min(bk, (r + 1) * rows_per_tile) if trim else bk
                s = s_scr[rows, 0:ncols]
                if masked:
                    shape = (rows_per_tile, ncols)
                    q_chunk = (qi * bq + r * rows_per_tile + lax.broadcasted_iota(jnp.int32, shape, 0)) // CHUNK
                    k_chunk = (ki * bk + lax.broadcasted_iota(jnp.int32, shape, 1)) // CHUNK
                    s = jnp.where(k_chunk <= q_chunk, s, -jnp.inf)
                m_prev = jnp.where(first, -jnp.inf, m_scr[rows, :])
                m_new = jnp.maximum(m_prev, jnp.max(s, axis=-1, keepdims=True))
                alpha = jnp.exp2(m_prev - m_new)
                pr = jnp.exp2(s - jnp.tile(m_new, (1, ncols // V7X_LANES)))
                l_scr[par_a, rows, :] = alpha * l_scr[par_a, rows, :] + jnp.sum(pr, axis=-1, keepdims=True)
                m_scr[rows, :] = m_new
                a_scr[slot][rows, :] = alpha
                p_scr[slot][rows, 0:ncols] = pr.astype(BF16)
                if ncols < bk:
                    p_scr[slot][rows, ncols:bk] = jnp.zeros((rows_per_tile, bk - ncols), BF16)

    def stage_b(slot):
        for s_scr, p_scr, m_scr, l_scr, a_scr, acc_scr in heads:
            alpha = jnp.tile(a_scr[slot][...], reps)
            acc_scr[...] = alpha * acc_scr[...] + _dot(p_scr[slot][...], v_ref[0])

    kind = kind_tab[p]

    @pl.when(kind == KIND_FIRST)
    def _():
        for s_scr, p_scr, m_scr, l_scr, a_scr, acc_scr in heads:
            m_scr[...] = jnp.full_like(m_scr, -jnp.inf)
            l_scr[...] = jnp.zeros_like(l_scr)
            acc_scr[...] = jnp.zeros_like(acc_scr)
        stage_a(True, 0)

    for slot in range(2):
        @pl.when(kind == KIND_PLAIN + slot)
        def _():
            stage_a(False, slot)
            stage_b(1 - slot)

        @pl.when(kind == KIND_MASKED + slot)
        def _():
            stage_a(True, slot)
            stage_b(1 - slot)

        @pl.when(kind == KIND_DRAIN + slot)
        def _():
            stage_b(1 - slot)

    @pl.when(last_tab[p] == 1)
    def _():
        o1, o2 = [acc_scr[...] * jnp.tile(1.0 / l_scr[par_b], reps)
                  for _, _, _, l_scr, _, acc_scr in heads]
        h_ref[...] = _diff_finish(o1, o2, lam_ref, sw_ref, z_ref[...], lam_init)


def _diff_prompt_tables(seq, bq, bk):
    pairs = []
    for qi in range(seq // bq):
        q_lo, q_hi = qi * bq, (qi + 1) * bq
        nk = -(-q_hi // bk)
        for ki in range(nk):
            fully_visible = (ki + 1) * bk <= (q_lo // CHUNK + 1) * CHUNK
            pairs.append((qi, ki, not fully_visible, ki == nk - 1))
    assert pairs[0][2], "the first pair sits on the diagonal"
    n = len(pairs)
    qa = [pairs[min(p, n - 1)][0] for p in range(n + 1)]
    ka = [pairs[min(p, n - 1)][1] for p in range(n + 1)]
    qb = [pairs[max(p - 1, 0)][0] for p in range(n + 1)]
    kb = [pairs[max(p - 1, 0)][1] for p in range(n + 1)]
    kind = ([KIND_FIRST] + [(KIND_MASKED if pairs[p][2] else KIND_PLAIN) + p % 2 for p in range(1, n)]
            + [KIND_DRAIN + n % 2])
    last = [0] + [1 if pairs[p - 1][3] else 0 for p in range(1, n + 1)]
    return [jnp.asarray(np.asarray(t, np.int32)) for t in (qa, ka, qb, kb, kind, last)]


def _diff_prompt_attn(q_pm, k_pm, v_pm, z, lam_p, subln_w, lam_init, batch, seq):
    bq = min(ATT_BQ, seq)
    bk = min(ATT_BK, seq)
    nq = seq // bq
    nkb = seq // bk
    pw = 2 * DIFF_HD
    tabs = _diff_prompt_tables(seq, bq, bk)
    n_steps = int(tabs[0].shape[0])

    def spec(shape, index):
        return pl.BlockSpec(shape, lambda b, h, p, qa, ka, qb, kb, kind, last: index(b, h, p, qa, ka, qb, kb))

    grid_spec = pltpu.PrefetchScalarGridSpec(
        num_scalar_prefetch=6,
        grid=(batch, DIFF_HEADS, n_steps),
        in_specs=[
            spec((1, bq, pw), lambda b, h, p, qa, ka, qb, kb: (h, b * nq + qa[p], 0)),
            spec((1, bk, pw), lambda b, h, p, qa, ka, qb, kb: (h, b * nkb + ka[p], 0)),
            spec((1, bk, pw), lambda b, h, p, qa, ka, qb, kb: (h, b * nkb + kb[p], 0)),
            spec((bq, pw), lambda b, h, p, qa, ka, qb, kb: (b * nq + qb[p], h)),
            spec((4, DIFF_HD), lambda b, h, p, qa, ka, qb, kb: (0, 0)),
            spec((1, pw), lambda b, h, p, qa, ka, qb, kb: (0, 0)),
        ],
        out_specs=spec((bq, pw), lambda b, h, p, qa, ka, qb, kb: (b * nq + qb[p], h)),
        scratch_shapes=2 * [
            pltpu.VMEM((bq, bk), F32),
            pltpu.VMEM((bq, bk), BF16),
            pltpu.VMEM((bq, bk), BF16),
            pltpu.VMEM((bq, V7X_LANES), F32),
            pltpu.VMEM((2, bq, V7X_LANES), F32),
            pltpu.VMEM((bq, V7X_LANES), F32),
            pltpu.VMEM((bq, V7X_LANES), F32),
            pltpu.VMEM((bq, pw), F32),
        ],
    )
    return pl.pallas_call(
        functools.partial(_diff_prompt_kernel, bq=bq, bk=bk, rows_per_tile=min(ATT_ROWS, bq),
                          lam_init=lam_init),
        grid_spec=grid_spec,
        out_shape=jax.ShapeDtypeStruct((batch * seq, D_INNER), BF16),
        compiler_params=_params(("parallel", "parallel", "arbitrary")),
        name="diff_prompt_attn",
    )(*tabs, q_pm, k_pm, v_pm, z, lam_p, subln_w.reshape(1, pw))


def _diff_sample_kernel(q_ref, kn_ref, vn_ref, z_ref, kc_ref, vc_ref, lam_ref, sw_ref, h_ref, *, lam_init):
    pw = 2 * DIFF_HD
    for pair in range(DIFF_HEADS):
        pcols = slice(pair * pw, (pair + 1) * pw)
        vc_bf = vc_ref[:, pcols].astype(BF16)
        vn_bf = vn_ref[pair]
        outs = []
        for h in range(2):
            cols = slice(h * DIFF_HD, (h + 1) * DIFF_HD)
            ccols = slice(pair * pw + h * DIFF_HD, pair * pw + (h + 1) * DIFF_HD)
            q_bf = q_ref[pair, :, cols]
            s_c = _dot_nt(q_bf, kc_ref[:, ccols].astype(BF16))
            s_n = _dot_nt(q_bf, kn_ref[pair, :, cols])
            m = jnp.maximum(jnp.max(s_c, axis=-1, keepdims=True), jnp.max(s_n, axis=-1, keepdims=True))
            p_c = jnp.exp2(s_c - m)
            p_n = jnp.exp2(s_n - m)
            l = jnp.sum(p_c, axis=-1, keepdims=True) + jnp.sum(p_n, axis=-1, keepdims=True)
            acc = _dot(p_c.astype(BF16), vc_bf) + _dot(p_n.astype(BF16), vn_bf)
            outs.append(acc * (1.0 / l))
        h_ref[:, pcols] = _diff_finish(outs[0], outs[1], lam_ref, sw_ref, z_ref[:, pcols], lam_init)


def _diff_sample_attn(q_pm, k_pm, v_pm, z, cache_k2, cache_v2, lam_p, subln_w, lam_init, batch, seq, past):
    pw = 2 * DIFF_HD
    return pl.pallas_call(
        functools.partial(_diff_sample_kernel, lam_init=lam_init),
        grid=(batch,),
        in_specs=[
            pl.BlockSpec((DIFF_HEADS, seq, pw), lambda b: (0, b, 0)),
            pl.BlockSpec((DIFF_HEADS, seq, pw), lambda b: (0, b, 0)),
            pl.BlockSpec((DIFF_HEADS, seq, pw), lambda b: (0, b, 0)),
            pl.BlockSpec((seq, D_INNER), lambda b: (b, 0)),
            pl.BlockSpec((past, D_INNER), lambda b: (b, 0)),
            pl.BlockSpec((past, D_INNER), lambda b: (b, 0)),
            pl.BlockSpec((4, DIFF_HD), lambda b: (0, 0)),
            pl.BlockSpec((1, pw), lambda b: (0, 0)),
        ],
        out_specs=pl.BlockSpec((seq, D_INNER), lambda b: (b, 0)),
        out_shape=jax.ShapeDtypeStruct((batch * seq, D_INNER), BF16),
        compiler_params=_params(("parallel",)),
        name="diff_sample_attn",
    )(q_pm, k_pm, v_pm, z, cache_k2, cache_v2, lam_p, subln_w.reshape(1, pw))


def kernel(x_prompt, x_sample, state_gla, state_hgrn, cache_k, cache_v, norm_w, final_norm_w,
           gla_w_in, gla_w_a1, gla_w_a2, gla_b_a, gla_norm_w, gla_w_out,
           hgrn_w_in, hgrn_lower_bounds, hgrn_norm_w, hgrn_w_out,
           diff_w_in, diff_lambda, diff_subln_w, diff_w_out):
    bp, tp, d = x_prompt.shape
    bs, ts, _ = x_sample.shape
    past = cache_k.shape[2]
    streams = [(bp, tp), (bs, ts)]
    xs = [x_prompt.reshape(bp * tp, d), x_sample.reshape(bs * ts, d)]

    def out_proj(h, res, w_stack, j, i, ssq=None):
        last = i == DEPTH - 1
        return _out_proj(h, res, w_stack, j, final_norm_w if last else norm_w[i + 1], last, ssq)

    gla_w_in_bf, gla_w_out_bf = gla_w_in.astype(BF16), gla_w_out.astype(BF16)
    hgrn_w_in_bf, hgrn_w_out_bf = hgrn_w_in.astype(BF16), hgrn_w_out.astype(BF16)
    diff_w_in_bf, diff_w_out_bf = diff_w_in.astype(BF16), diff_w_out.astype(BF16)

    hn = [None, None]
    gla_states = [[], []]
    hgrn_states = [[], []]
    k_rows = [[], []]
    v_rows = [[], []]
    ia = ib = ic = 0
    for i in range(DEPTH):
        kind = i % N_MIXERS
        if kind == 0:
            j = ia
            ia += 1
            w_a1 = jnp.pad(gla_w_a1[j], ((0, 0), (0, V7X_LANES - GLA_GATE_RANK))).astype(BF16)
            w_a2 = jnp.pad(gla_w_a2[j], ((0, V7X_LANES - GLA_GATE_RANK), (0, 0))).astype(BF16)
            for si, (batch, seq) in enumerate(streams):
                if i == 0:
                    proj, gate_lr = _norm_proj(xs[si], norm_w[i], gla_w_in_bf, j, w_a1)
                else:
                    proj, gate_lr = _in_proj(hn[si], gla_w_in_bf, j, w_a1)
                states = None if si == 0 else state_gla
                h, s_new = _gla_mix(proj, gate_lr, w_a2, gla_b_a[j], gla_norm_w[j], states, j, batch, seq)
                xs[si], hn[si] = out_proj(h, xs[si], gla_w_out_bf, j, i)
                gla_states[si].append(s_new)
        elif kind == 1:
            j = ib
            ib += 1
            for si, (batch, seq) in enumerate(streams):
                proj = _in_proj(hn[si], hgrn_w_in_bf, j)
                states = None if si == 0 else state_hgrn
                u, ssq, s_new = _hgrn_mix(proj, hgrn_lower_bounds, i, hgrn_norm_w[j], states, j, batch, seq)
                xs[si], hn[si] = out_proj(u, xs[si], hgrn_w_out_bf, j, i, ssq)
                hgrn_states[si].append(s_new)
        else:
            j = ic
            ic += 1
            lam_init = 0.8 - 0.6 * math.exp(-0.3 * i)
            for si, (batch, seq) in enumerate(streams):
                q_pm, k_pm, v_pm, k_new, v_new, z = _diff_proj(hn[si], diff_w_in_bf, j)
                if si == 0:
                    h = _diff_prompt_attn(q_pm, k_pm, v_pm, z, diff_lambda[j], diff_subln_w[j], lam_init,
                                          batch, seq)
                else:
                    ck = cache_k[j].reshape(batch * past, 2 * DIFF_HEADS * DIFF_HD)
                    cv = cache_v[j].reshape(batch * past, 2 * DIFF_HEADS * DIFF_HD)
                    h = _diff_sample_attn(q_pm, k_pm, v_pm, z, ck, cv, diff_lambda[j], diff_subln_w[j],
                                          lam_init, batch, seq, past)
                xs[si], hn[si] = out_proj(h, xs[si], diff_w_out_bf, j, i)
                k_rows[si].append(k_new.reshape(batch, seq, 2 * DIFF_HEADS, DIFF_HD))
                v_rows[si].append(v_new.reshape(batch, seq, DIFF_HEADS, 2 * DIFF_HD))

    return (hn[0].reshape(bp, tp, d), hn[1].reshape(bs, ts, d),
            jnp.stack(gla_states[0]), jnp.stack(gla_states[1]),
            jnp.stack(hgrn_states[0]), jnp.stack(hgrn_states[1]),
            jnp.stack(k_rows[0]), jnp.stack(v_rows[0]),
            jnp.stack(k_rows[1]), jnp.stack(v_rows[1]))
```
